```python
import jax, jax.numpy as jnp
from jax import lax
import numpy as np

D_MODEL = 2048
BATCH = 4
SEQ = 4096
DEPTH = 2

GRID_W = 64
CTX_LEN = 256
N_MIXERS = 2
N_GLA_LAYERS = (DEPTH + N_MIXERS - 1) // N_MIXERS
N_NAT_LAYERS = DEPTH // N_MIXERS

GLA_HEADS = 4
GLA_DK = D_MODEL // (2 * GLA_HEADS)
GLA_DV = D_MODEL // GLA_HEADS
GLA_CHUNK = 64
GLA_GATE_RANK = 16
GLA_GATE_NORM = 16.0
GLA_QK = GLA_HEADS * GLA_DK
GLA_VD = GLA_HEADS * GLA_DV
GLA_IN = 2 * GLA_QK + 2 * GLA_VD + 2 * GLA_GATE_RANK
ROPE_BASE = 10000.0

NAT_HEADS = 16
NAT_DH = D_MODEL // NAT_HEADS
WIN_R = 8
WIN_C = 16
NAT_QC = 16

N_GROUPS = 4
EXPERTS_PER_GROUP = 4
N_EXPERTS = N_GROUPS * EXPERTS_PER_GROUP
TOP_K_IN_GROUP = 2
D_EXPERT = D_MODEL // 2
MOE_BLOCK = 256

DEEPNORM_ALPHA = (2 * DEPTH) ** 0.25
DEEPNORM_BETA = (8 * DEPTH) ** -0.25
LN_EPS = 1e-5
NEG_INF = -1e30

kernel_name = "hybrid_gla_natten_hmoe_dit"


def _layer_norm(x, g, b):
    xf = x.astype(jnp.float32)
    mu = jnp.mean(xf, axis=-1, keepdims=True)
    var = jnp.mean(jnp.square(xf - mu), axis=-1, keepdims=True)
    return ((xf - mu) * lax.rsqrt(var + LN_EPS) * g + b).astype(x.dtype)


def _rope_axis(x, pos):
    half = x.shape[-1] // 2
    freqs = ROPE_BASE ** (-jnp.arange(half, dtype=jnp.float32) / half)
    ang = pos.astype(jnp.float32)[:, None] * freqs
    cos, sin = jnp.cos(ang), jnp.sin(ang)
    x1, x2 = x[..., :half], x[..., half:]
    return jnp.concatenate([x1 * cos - x2 * sin, x1 * sin + x2 * cos], axis=-1)


def _rope_2d(x):
    t = jnp.arange(x.shape[-2])
    d_ax = x.shape[-1] // 2
    return jnp.concatenate([_rope_axis(x[..., :d_ax], t // GRID_W),
                            _rope_axis(x[..., d_ax:], t % GRID_W)], axis=-1)


def _gla_chunked(q, k, v, g, s0):
    b_, h_, n, _ = q.shape
    dv = v.shape[-1]
    nc = n // GLA_CHUNK
    ch = lambda t: t.reshape(b_, h_, nc, GLA_CHUNK, t.shape[-1])
    q, k, v, g = ch(q), ch(k), ch(v), ch(g)
    cum = jnp.cumsum(g, axis=3)
    total = cum[:, :, :, -1]
    q_dec = q * jnp.exp(cum)
    k_inv = k * jnp.exp(-cum)
    k_end = k * jnp.exp(total[:, :, :, None] - cum)
    tri = jnp.tril(jnp.ones((GLA_CHUNK, GLA_CHUNK), dtype=bool))
    att = jnp.where(tri, jnp.einsum('bhncd,bhnsd->bhncs', q_dec, k_inv), 0.0)
    o_intra = jnp.einsum('bhncs,bhnsv->bhncv', att, v)

    def step(state, xs):
        q_c, k_c, v_c, d_c = xs
        o_c = jnp.einsum('bhcd,bhdv->bhcv', q_c, state)
        state = state * d_c[..., None] + jnp.einsum('bhcd,bhcv->bhdv', k_c, v_c)
        return state, o_c

    xs = (jnp.moveaxis(q_dec, 2, 0), jnp.moveaxis(k_end, 2, 0), jnp.moveaxis(v, 2, 0),
          jnp.moveaxis(jnp.exp(total), 2, 0))
    s_fin, o_inter = lax.scan(step, s0, xs)
    o = o_intra + jnp.moveaxis(o_inter, 0, 2)
    return o.reshape(b_, h_, n, dv), s_fin


def _gla_mixer(h_lat, h_ctx, w_in, w_gate, b_gate, norm_g, w_out):
    def project(h):
        bsz, n, _ = h.shape
        p = h @ w_in
        q, k, v, r, glr = jnp.split(p, [GLA_QK, 2 * GLA_QK, 2 * GLA_QK + GLA_VD, 2 * GLA_QK + 2 * GLA_VD], axis=-1)
        heads = lambda t, d: t.reshape(bsz, n, GLA_HEADS, d).transpose(0, 2, 1, 3).astype(jnp.float32)
        glr = glr.reshape(bsz, n, 2, GLA_GATE_RANK)
        z = jnp.einsum('bndr,drk->bndk', glr, w_gate) + b_gate
        logdecay = jax.nn.log_sigmoid(z.astype(jnp.float32)) / GLA_GATE_NORM
        return (heads(q, GLA_DK) * GLA_DK ** -0.5, heads(k, GLA_DK), heads(v, GLA_DV), r,
                heads(logdecay[:, :, 0], GLA_DK), heads(logdecay[:, :, 1], GLA_DK))

    ql, kl, vl, rl, gfl, gbl = project(h_lat)
    ql, kl = _rope_2d(ql), _rope_2d(kl)
    qc, kc, vc, rc, gfc, gbc = project(h_ctx)
    bsz = h_lat.shape[0]
    s0 = jnp.zeros((bsz, GLA_HEADS, GLA_DK, GLA_DV), jnp.float32)
    flip = lambda t: jnp.flip(t, axis=2)
    o_cf, s_f = _gla_chunked(qc, kc, vc, gfc, s0)
    o_lf, _ = _gla_chunked(ql, kl, vl, gfl, s_f)
    o_cb, s_b = _gla_chunked(flip(qc), flip(kc), flip(vc), flip(gbc), s0)
    o_lb, _ = _gla_chunked(flip(ql), flip(kl), flip(vl), flip(gbl), s_b)

    def finish(o, r):
        o = o * lax.rsqrt(jnp.mean(jnp.square(o), axis=-1, keepdims=True) + LN_EPS) * norm_g
        b_, _, n, _ = o.shape
        o = o.transpose(0, 2, 1, 3).reshape(b_, n, GLA_VD).astype(r.dtype)
        return (o * jax.nn.silu(r)) @ w_out

    return finish(o_lf + flip(o_lb), rl), finish(o_cf + flip(o_cb), rc)


def _natten_layout(rows):
    kr = min(WIN_R, rows)
    qr = max(d for d in (8, 4, 2, 1) if rows % d == 0)
    krb = min(qr + kr - 1, rows)
    kcb = min(NAT_QC + WIN_C - 1, GRID_W)
    n_rb, n_cb = rows // qr, GRID_W // NAT_QC
    rs = np.clip(np.arange(rows) - kr // 2, 0, rows - kr)
    cs = np.clip(np.arange(GRID_W) - WIN_C // 2, 0, GRID_W - WIN_C)
    r0 = np.clip(rs[np.arange(n_rb) * qr], 0, rows - krb)
    c0 = np.clip(cs[np.arange(n_cb) * NAT_QC], 0, GRID_W - kcb)
    q_r = (np.arange(n_rb)[:, None] * qr + np.arange(qr))[:, None, :, None, None, None]
    q_c = (np.arange(n_cb)[:, None] * NAT_QC + np.arange(NAT_QC))[None, :, None, :, None, None]
    k_r = (r0[:, None] + np.arange(krb))[:, None, None, None, :, None]
    k_c_all = c0[:, None] + np.arange(kcb)
    k_c = k_c_all[None, :, None, None, None, :]
    shape = (n_rb, n_cb, qr, NAT_QC, krb, kcb)
    mask = (k_r >= rs[q_r]) & (k_r < rs[q_r] + kr) & (k_c >= cs[q_c]) & (k_c < cs[q_c] + WIN_C)
    dr = np.clip(k_r - q_r + WIN_R - 1, 0, 2 * WIN_R - 2)
    dc = np.clip(k_c - q_c + WIN_C - 1, 0, 2 * WIN_C - 2)
    flat = lambda a: np.broadcast_to(a, shape).reshape(n_rb, n_cb, qr * NAT_QC, krb * kcb)
    return (qr, krb, kcb, r0.astype(np.int32), k_c_all.astype(np.int32),
            flat(mask), flat(dr).astype(np.int32), flat(dc).astype(np.int32))


def _natten_mixer(h_lat, h_ctx, w_in, rpb, w_out, with_ctx_out):
    bsz, n, d = h_lat.shape
    rows = n // GRID_W
    qr, krb, kcb, r0, k_cols, mask, dr, dc = _natten_layout(rows)
    n_rb, n_cb = rows // qr, GRID_W // NAT_QC
    scale = NAT_DH ** -0.5

    def heads(h):
        p = (h @ w_in).reshape(h.shape[0], h.shape[1], 3, NAT_HEADS, NAT_DH).transpose(2, 0, 3, 1, 4)
        return p[0], p[1], p[2]

    ql, kl, vl = heads(h_lat)
    qc, kc, vc = heads(h_ctx)
    k_grid = kl.reshape(bsz, NAT_HEADS, rows, GRID_W, NAT_DH)
    v_grid = vl.reshape(bsz, NAT_HEADS, rows, GRID_W, NAT_DH)
    q_blk = ql.reshape(bsz, NAT_HEADS, n_rb, qr, n_cb, NAT_QC, NAT_DH).transpose(2, 0, 1, 4, 3, 5, 6)
    q_blk = q_blk.reshape(n_rb, bsz, NAT_HEADS, n_cb, qr * NAT_QC, NAT_DH)
    col_idx = jnp.asarray(k_cols)
    kloc = krb * kcb

    def band(t_rows):
        t = t_rows[:, :, :, col_idx]
        return t.transpose(0, 1, 3, 2, 4, 5).reshape(bsz, NAT_HEADS, n_cb, kloc, NAT_DH)

    def row_block(xs):
        q_b, r_start, m_b, dr_b, dc_b = xs
        k_b = band(lax.dynamic_slice_in_dim(k_grid, r_start, krb, axis=2))
        v_b = band(lax.dynamic_slice_in_dim(v_grid, r_start, krb, axis=2))
        bias = rpb[:, dr_b, dc_b].astype(jnp.float32)
        s_loc = jnp.einsum('bhnqd,bhnkd->bhnqk', q_b, k_b).astype(jnp.float32) * scale + bias
        s_loc = jnp.where(m_b, s_loc, NEG_INF)
        s_ctx = jnp.einsum('bhnqd,bhkd->bhnqk', q_b, kc).astype(jnp.float32) * scale
        p = jax.nn.softmax(jnp.concatenate([s_loc, s_ctx], axis=-1), axis=-1).astype(v_b.dtype)
        return (jnp.einsum('bhnqk,bhnkd->bhnqd', p[..., :kloc], v_b)
                + jnp.einsum('bhnqk,bhkd->bhnqd', p[..., kloc:], vc))

    o = lax.map(row_block, (q_blk, jnp.asarray(r0), jnp.asarray(mask), jnp.asarray(dr), jnp.asarray(dc)))
    o = o.reshape(n_rb, bsz, NAT_HEADS, n_cb, qr, NAT_QC, NAT_DH).transpose(1, 0, 4, 3, 5, 2, 6)
    y_lat = o.reshape(bsz, n, d) @ w_out
    if not with_ctx_out:
        return y_lat, None
    s_c = jnp.einsum('bhqd,bhkd->bhqk', qc, kc).astype(jnp.float32) * scale
    o_c = jnp.einsum('bhqk,bhkd->bhqd', jax.nn.softmax(s_c, axis=-1).astype(vc.dtype), vc)
    y_ctx = o_c.transpose(0, 2, 1, 3).reshape(h_ctx.shape) @ w_out
    return y_lat, y_ctx


def _hier_moe(tok, w_group, b_group, w_expert, b_expert, w1, w3, w2):
    t_count, d = tok.shape
    g_logits = (tok @ w_group).astype(jnp.float32) + b_group
    g_p, g_idx = lax.top_k(jax.nn.softmax(g_logits, axis=-1), 1)
    e_logits = ((tok @ w_expert).astype(jnp.float32) + b_expert).reshape(t_count, N_GROUPS, EXPERTS_PER_GROUP)
    e_in = jnp.take_along_axis(e_logits, g_idx[:, :, None], axis=1)[:, 0]
    e_p, e_idx = lax.top_k(jax.nn.softmax(e_in, axis=-1), TOP_K_IN_GROUP)
    e_p = e_p / jnp.sum(e_p, axis=-1, keepdims=True)
    weight = (g_p * e_p).reshape(-1)
    expert = (g_idx * EXPERTS_PER_GROUP + e_idx).reshape(-1)
    token = jnp.repeat(jnp.arange(t_count, dtype=jnp.int32), TOP_K_IN_GROUP)
    n_assign = t_count * TOP_K_IN_GROUP
    order = jnp.argsort(expert)
    e_sorted = expert[order]
    counts = jnp.bincount(expert, length=N_EXPERTS)
    starts = jnp.cumsum(counts) - counts
    padded = (counts + MOE_BLOCK - 1) // MOE_BLOCK * MOE_BLOCK
    pad_end = jnp.cumsum(padded)
    pad_start = pad_end - padded
    dest = pad_start[e_sorted] + jnp.arange(n_assign) - starts[e_sorted]
    n_blocks = -(-n_assign // MOE_BLOCK) + N_EXPERTS
    cap = n_blocks * MOE_BLOCK
    slot_token = jnp.full((cap,), t_count, jnp.int32).at[dest].set(token[order])
    slot_weight = jnp.zeros((cap,), jnp.float32).at[dest].set(weight[order])
    block_expert = jnp.minimum(jnp.searchsorted(pad_end, jnp.arange(n_blocks) * MOE_BLOCK, side='right'),
                               N_EXPERTS - 1)
    tok_pad = jnp.concatenate([tok, jnp.zeros((1, d), tok.dtype)], axis=0)

    def run(xs):
        idx, e = xs
        xb = tok_pad[idx]
        return (jax.nn.silu(xb @ w1[e]) * (xb @ w3[e])) @ w2[e]

    y = lax.map(run, (slot_token.reshape(n_blocks, MOE_BLOCK), block_expert))
    y = y.reshape(cap, d).astype(jnp.float32) * slot_weight[:, None]
    out = jax.ops.segment_sum(y, slot_token, num_segments=t_count + 1)[:t_count]
    return out.astype(tok.dtype)


def setup_inputs(seed: int = 0) -> dict:
    key = jax.random.key(seed)
    ks = jax.random.split(key, 24)
    f32 = jnp.float32
    nrm = lambda k, shape, s: jax.random.normal(k, shape, f32) * s
    D = D_MODEL
    return {
        "x": nrm(ks[0], (BATCH, SEQ, D), 1.0),
        "c": nrm(ks[1], (BATCH, D), 1.0),
        "ctx": nrm(ks[2], (BATCH, CTX_LEN, D), 1.0),
        "c_ctx": nrm(ks[3], (D,), 1.0),
        "ada_w": nrm(ks[4], (DEPTH, D, 6 * D), 0.5 * D ** -0.5),
        "ada_b": nrm(ks[5], (DEPTH, 6 * D), 0.02),
        "ln_g": 1.0 + nrm(ks[6], (DEPTH, 2, D), 0.02),
        "ln_b": nrm(ks[7], (DEPTH, 2, D), 0.02),
        "gla_w_in": nrm(ks[8], (N_GLA_LAYERS, D, GLA_IN), D ** -0.5),
        "gla_w_gate": nrm(ks[9], (N_GLA_LAYERS, 2, GLA_GATE_RANK, GLA_QK), GLA_GATE_RANK ** -0.5),
        "gla_b_gate": nrm(ks[10], (N_GLA_LAYERS, 2, GLA_QK), 0.5),
        "gla_norm_g": 1.0 + nrm(ks[11], (N_GLA_LAYERS, GLA_DV), 0.02),
        "gla_w_out": nrm(ks[12], (N_GLA_LAYERS, GLA_VD, D), DEEPNORM_BETA * GLA_VD ** -0.5),
        "nat_w_in": nrm(ks[13], (N_NAT_LAYERS, D, 3 * D), D ** -0.5),
        "nat_rpb": nrm(ks[14], (N_NAT_LAYERS, NAT_HEADS, 2 * WIN_R - 1, 2 * WIN_C - 1), 0.1),
        "nat_w_out": nrm(ks[15], (N_NAT_LAYERS, D, D), DEEPNORM_BETA * D ** -0.5),
        "moe_w_group": nrm(ks[16], (DEPTH, D, N_GROUPS), D ** -0.5),
        "moe_b_group": nrm(ks[17], (DEPTH, N_GROUPS), 0.01),
        "moe_w_expert": nrm(ks[18], (DEPTH, D, N_EXPERTS), D ** -0.5),
        "moe_b_expert": nrm(ks[19], (DEPTH, N_EXPERTS), 0.01),
        "moe_w1": nrm(ks[20], (DEPTH, N_EXPERTS, D, D_EXPERT), D ** -0.5),
        "moe_w3": nrm(ks[21], (DEPTH, N_EXPERTS, D, D_EXPERT), D ** -0.5),
        "moe_w2": nrm(ks[22], (DEPTH, N_EXPERTS, D_EXPERT, D), DEEPNORM_BETA * D_EXPERT ** -0.5),
    }


def reference(x, c, ctx, c_ctx, ada_w, ada_b, ln_g, ln_b,
              gla_w_in, gla_w_gate, gla_b_gate, gla_norm_g, gla_w_out,
              nat_w_in, nat_rpb, nat_w_out,
              moe_w_group, moe_b_group, moe_w_expert, moe_b_expert, moe_w1, moe_w3, moe_w2):
    bsz, n, d = x.shape
    h_lat, h_ctx = x, ctx
    for i in range(DEPTH):
        last = i == DEPTH - 1
        j = i // N_MIXERS
        mod_l = (jax.nn.silu(c) @ ada_w[i] + ada_b[i])[:, None, :]
        mod_c = (jax.nn.silu(c_ctx) @ ada_w[i] + ada_b[i])[None, None, :]
        sh1_l, sc1_l, gt1_l, sh2_l, sc2_l, gt2_l = jnp.split(mod_l, 6, axis=-1)
        sh1_c, sc1_c, gt1_c, sh2_c, sc2_c, gt2_c = jnp.split(mod_c, 6, axis=-1)
        a_l = h_lat * (1.0 + sc1_l) + sh1_l
        a_c = h_ctx * (1.0 + sc1_c) + sh1_c
        if i % N_MIXERS == 0:
            y_l, y_c = _gla_mixer(a_l, a_c, gla_w_in[j], gla_w_gate[j], gla_b_gate[j], gla_norm_g[j], gla_w_out[j])
        else:
            y_l, y_c = _natten_mixer(a_l, a_c, nat_w_in[j], nat_rpb[j], nat_w_out[j], not last)
        h_lat = _layer_norm(DEEPNORM_ALPHA * h_lat + gt1_l * y_l, ln_g[i, 0], ln_b[i, 0])
        f_l = (h_lat * (1.0 + sc2_l) + sh2_l).reshape(-1, d)
        if last:
            y2_l = _hier_moe(f_l, moe_w_group[i], moe_b_group[i], moe_w_expert[i], moe_b_expert[i],
                             moe_w1[i], moe_w3[i], moe_w2[i]).reshape(bsz, n, d)
        else:
            h_ctx = _layer_norm(DEEPNORM_ALPHA * h_ctx + gt1_c * y_c, ln_g[i, 0], ln_b[i, 0])
            f_c = (h_ctx * (1.0 + sc2_c) + sh2_c).reshape(-1, d)
            y2 = _hier_moe(jnp.concatenate([f_l, f_c], axis=0), moe_w_group[i], moe_b_group[i],
                           moe_w_expert[i], moe_b_expert[i], moe_w1[i], moe_w3[i], moe_w2[i])
            y2_l = y2[:bsz * n].reshape(bsz, n, d)
            h_ctx = _layer_norm(DEEPNORM_ALPHA * h_ctx + gt2_c * y2[bsz * n:].reshape(h_ctx.shape),
                                ln_g[i, 1], ln_b[i, 1])
        h_lat = _layer_norm(DEEPNORM_ALPHA * h_lat + gt2_l * y2_l, ln_g[i, 1], ln_b[i, 1])
    return h_lat
```

```python
import functools

import jax
import jax.numpy as jnp
import numpy as np
from jax import lax
from jax.experimental import pallas as pl
from jax.experimental.pallas import tpu as pltpu

F32 = jnp.float32
BF16 = jnp.bfloat16

D = 2048
B = 4
SEQ = 4096
DEPTH = 2
GRID_W = 64
CTX = 256
R_LAT = B * SEQ
R_CTX = B * CTX
R_ALL = R_LAT + R_CTX

GLA_H = 4
GLA_DK = 256
GLA_DV = 512
GLA_CHUNK = 64
GLA_RANK = 16
GLA_QK = GLA_H * GLA_DK
GLA_VD = GLA_H * GLA_DV
GLA_MAIN = 2 * GLA_QK + 2 * GLA_VD
GLA_GATE_NORM = 16.0
ROPE_BASE = 10000.0

NAT_H = 16
NAT_DH = 128
WIN_R = 8
WIN_C = 16
NAT_QR = 8
NAT_BAND = 16

N_GROUPS = 4
EPG = 4
N_EXPERTS = 16
D_EXPERT = 1024
MOE_TM = 256

ALPHA = (2 * DEPTH) ** 0.25
LN_EPS = 1e-5
NEG_INF = -1e30

LANE = 128
SUBLANE = 8
VMEM_LIMIT = 56 * 1024 * 1024


def _cparams(n_axes):
    return pltpu.CompilerParams(dimension_semantics=("arbitrary",) * n_axes,
                                vmem_limit_bytes=VMEM_LIMIT)


def _split_bf16(x):
    hi = x.astype(BF16)
    lo = (x - hi.astype(F32)).astype(BF16)
    return hi, lo


def _dot(a, b):
    return jnp.dot(a, b, preferred_element_type=F32)


def _dot_hi(a, b):
    ah, al = _split_bf16(a)
    bh, bl = _split_bf16(b)
    return _dot(ah, bh) + _dot(al, bh) + _dot(ah, bl)


def _dot_nt(a, b):
    return lax.dot_general(a, b, (((1,), (1,)), ((), ())), preferred_element_type=F32)


def _dot_tn(a, b):
    return lax.dot_general(a, b, (((0,), (0,)), ((), ())), preferred_element_type=F32)


def _mod_row(tile_rows):
    return lambda i: jnp.minimum((i * tile_rows) // SEQ, B)


ADA_TN = 1024


def _ada_kernel(c_ref, w_ref, b_ref, o_ref):
    x = c_ref[...]
    s = x * jax.nn.sigmoid(x)
    o_ref[0] = _dot_hi(s, w_ref[0]) + b_ref[0]


def _ada_table(cc, ada_w, ada_b):
    n = 6 * D
    return pl.pallas_call(
        _ada_kernel,
        out_shape=jax.ShapeDtypeStruct((DEPTH, SUBLANE, n), F32),
        grid=(DEPTH, n // ADA_TN),
        in_specs=[
            pl.BlockSpec((SUBLANE, D), lambda l, j: (0, 0)),
            pl.BlockSpec((1, D, ADA_TN), lambda l, j: (l, 0, j)),
            pl.BlockSpec((1, 1, ADA_TN), lambda l, j: (l, 0, j)),
        ],
        out_specs=pl.BlockSpec((1, SUBLANE, ADA_TN), lambda l, j: (l, 0, j)),
        compiler_params=_cparams(2),
        name="ada_table",
    )(cc, ada_w, ada_b.reshape(DEPTH, 1, n))


PROJ_TM = 1024
PROJ_TN = 1024


def _proj_kernel(h_ref, mod_ref, w_ref, *rest, with_extra):
    if with_extra:
        wx_ref, o_ref, ox_ref, a_scr = rest
    else:
        o_ref, a_scr = rest
    j = pl.program_id(1)

    @pl.when(j == 0)
    def _():
        sh = mod_ref[0, 0:1, :]
        sc = mod_ref[0, 1:2, :]
        a_scr[...] = (h_ref[...] * (1.0 + sc) + sh).astype(BF16)
        if with_extra:
            ox_ref[...] = _dot(a_scr[...], wx_ref[...])

    o_ref[...] = _dot(a_scr[...], w_ref[...]).astype(o_ref.dtype)


def _proj(h, mod, w, w_extra=None):
    rows = h.shape[0]
    n = w.shape[1]
    with_extra = w_extra is not None
    in_specs = [
        pl.BlockSpec((PROJ_TM, D), lambda i, j: (i, 0)),
        pl.BlockSpec((1, 6, D), lambda i, j: (_mod_row(PROJ_TM)(i), 0, 0)),
        pl.BlockSpec((D, PROJ_TN), lambda i, j: (0, j)),
    ]
    out_shape = [jax.ShapeDtypeStruct((rows, n), BF16)]
    out_specs = [pl.BlockSpec((PROJ_TM, PROJ_TN), lambda i, j: (i, j))]
    args = [h, mod, w]
    if with_extra:
        in_specs.append(pl.BlockSpec((D, LANE), lambda i, j: (0, 0)))
        out_shape.append(jax.ShapeDtypeStruct((rows, LANE), F32))
        out_specs.append(pl.BlockSpec((PROJ_TM, LANE), lambda i, j: (i, 0)))
        args.append(w_extra)
    out = pl.pallas_call(
        functools.partial(_proj_kernel, with_extra=with_extra),
        out_shape=out_shape,
        grid=(rows // PROJ_TM, n // PROJ_TN),
        in_specs=in_specs,
        out_specs=out_specs,
        scratch_shapes=[pltpu.VMEM((PROJ_TM, D), BF16)],
        compiler_params=_cparams(2),
        name="mod_proj",
    )(*args)
    return out if with_extra else out[0]


GLA_TB = 256
GLA_STEPS = 1 + SEQ // GLA_TB


def _rope_tables():
    half = GLA_DK // 4
    freqs = ROPE_BASE ** (-np.arange(half, dtype=np.float64) / half)
    t = np.arange(SEQ)
    ang_r = (t // GRID_W)[:, None] * freqs
    ang_c = (t % GRID_W)[:, None] * freqs
    cos = np.concatenate([np.cos(ang_r)] * 2 + [np.cos(ang_c)] * 2, axis=1)
    sin = np.concatenate([-np.sin(ang_r), np.sin(ang_r), -np.sin(ang_c), np.sin(ang_c)], axis=1)
    cos = np.concatenate([cos, np.ones((GLA_TB, GLA_DK))], axis=0)
    sin = np.concatenate([sin, np.zeros((GLA_TB, GLA_DK))], axis=0)
    return jnp.asarray(cos, F32), jnp.asarray(sin, F32)


def _rope(x, cos, sin):
    half = LANE // 2
    parts = [pltpu.roll(x[:, g * LANE:(g + 1) * LANE], half, 1) for g in range(x.shape[1] // LANE)]
    return x * cos + jnp.concatenate(parts, axis=1) * sin


def _gla_kernel(q_ref, k_ref, v_ref, glr_ref, wg_ref, bg_ref, cos_ref, sin_ref, o_ref, s_scr, *, reverse):
    @pl.when(pl.program_id(2) == 0)
    def _():
        s_scr[...] = jnp.zeros_like(s_scr)

    row = lax.broadcasted_iota(jnp.int32, (GLA_CHUNK, GLA_CHUNK), 0)
    col = lax.broadcasted_iota(jnp.int32, (GLA_CHUNK, GLA_CHUNK), 1)
    keep = (col >= row) if reverse else (col <= row)
    tri = jnp.where(keep, 1.0, 0.0).astype(BF16)
    ones = jnp.ones((GLA_CHUNK, LANE), BF16)
    n_chunks = GLA_TB // GLA_CHUNK
    order = range(n_chunks - 1, -1, -1) if reverse else range(n_chunks)
    for ci in order:
        sl = slice(ci * GLA_CHUNK, (ci + 1) * GLA_CHUNK)
        cos = cos_ref[sl, :]
        sin = sin_ref[sl, :]
        q = _rope(q_ref[sl, :].astype(F32), cos, sin)
        k = _rope(k_ref[sl, :].astype(F32), cos, sin)
        v = v_ref[sl, :]
        z = _dot_hi(glr_ref[sl, :], wg_ref[0]) + bg_ref[0]
        g = (jnp.minimum(z, 0.0) - jnp.log1p(jnp.exp(-jnp.abs(z)))) * (1.0 / GLA_GATE_NORM)
        g_hi, g_lo = _split_bf16(g)
        cum = _dot(tri, g_hi) + _dot(tri, g_lo)
        tot = cum[0:1, :] if reverse else cum[GLA_CHUNK - 1:GLA_CHUNK, :]
        q_dec = (q * jnp.exp(cum)).astype(BF16)
        k_inv = (k * jnp.exp(-cum)).astype(BF16)
        k_end = (k * jnp.exp(tot - cum)).astype(BF16)
        att = jnp.where(keep, _dot_nt(q_dec, k_inv), 0.0).astype(BF16)
        state = s_scr[...]
        o_ref[sl, :] = _dot(att, v) + _dot(q_dec, state.astype(BF16))
        decay = jnp.exp(_dot_tn(g_hi, ones) + _dot_tn(g_lo, ones))
        decay = jnp.concatenate([decay] * (GLA_DV // LANE), axis=1)
        s_scr[...] = state * decay + _dot_tn(k_end, v)


def _gla_scan(p, glr, wg, bg, cos, sin, reverse):
    lat_blocks = SEQ // GLA_TB

    def tblock(s):
        if reverse:
            return jnp.where(s == 0, 0, GLA_STEPS - s)
        return s

    def rows(b, s):
        tb = tblock(s)
        return jnp.where(tb == 0, R_LAT // GLA_TB + b, b * lat_blocks + tb - 1)

    def rope_rows(s):
        tb = tblock(s)
        return jnp.where(tb == 0, lat_blocks, tb - 1)

    d = 1 if reverse else 0
    kq = GLA_QK // GLA_DK
    return pl.pallas_call(
        functools.partial(_gla_kernel, reverse=reverse),
        out_shape=jax.ShapeDtypeStruct((R_ALL, GLA_VD), F32),
        grid=(B, GLA_H, GLA_STEPS),
        in_specs=[
            pl.BlockSpec((GLA_TB, GLA_DK), lambda b, h, s: (rows(b, s), h)),
            pl.BlockSpec((GLA_TB, GLA_DK), lambda b, h, s: (rows(b, s), kq + h)),
            pl.BlockSpec((GLA_TB, GLA_DV), lambda b, h, s: (rows(b, s), 2 * GLA_QK // GLA_DV + h)),
            pl.BlockSpec((GLA_TB, LANE), lambda b, h, s: (rows(b, s), 0)),
            pl.BlockSpec((1, LANE, GLA_DK), lambda b, h, s: (d, 0, h)),
            pl.BlockSpec((1, 1, GLA_DK), lambda b, h, s: (d, 0, h)),
            pl.BlockSpec((GLA_TB, GLA_DK), lambda b, h, s: (rope_rows(s), 0)),
            pl.BlockSpec((GLA_TB, GLA_DK), lambda b, h, s: (rope_rows(s), 0)),
        ],
        out_specs=pl.BlockSpec((GLA_TB, GLA_DV), lambda b, h, s: (rows(b, s), h)),
        scratch_shapes=[pltpu.VMEM((GLA_DK, GLA_DV), F32)],
        compiler_params=_cparams(3),
        name="gla_scan_bwd" if reverse else "gla_scan_fwd",
    )(p, p, p, glr, wg, bg, cos, sin)


def _nat_bias_table(rpb):
    qc = np.arange(GRID_W)[:, None]
    kc = np.arange(GRID_W)[None, :]
    cs = np.clip(qc - WIN_C // 2, 0, GRID_W - WIN_C)
    col_ok = (kc >= cs) & (kc < cs + WIN_C)
    dc = np.clip(kc - qc + WIN_C - 1, 0, 2 * WIN_C - 2)
    n_dr = 2 * WIN_R - 1
    tc = jnp.where(col_ok[None, None], rpb[:, :, dc].astype(F32), NEG_INF)
    tc = jnp.concatenate([tc, jnp.full((NAT_H, 1, GRID_W, GRID_W), NEG_INF, F32)], axis=1)
    rows_n = SEQ // GRID_W
    sel = np.zeros((3, NAT_QR, NAT_BAND, n_dr + 1), np.float32)
    for p_i, (r0, bs) in enumerate(((0, 0), (NAT_QR, NAT_QR - WIN_R // 2), (rows_n - NAT_QR, rows_n - NAT_BAND))):
        for i in range(NAT_QR):
            qr = r0 + i
            rs = min(max(qr - WIN_R // 2, 0), rows_n - WIN_R)
            for j in range(NAT_BAND):
                kr = bs + j
                a = kr - qr + WIN_R - 1 if rs <= kr < rs + WIN_R else n_dr
                sel[p_i, i, j, a] = 1.0
    tab = jnp.einsum('pija,haqk->phiqjk', jnp.asarray(sel), tc, precision=lax.Precision.HIGHEST)
    return tab.reshape(3, NAT_H, NAT_QR * GRID_W, NAT_BAND * GRID_W)


def _nat_kernel(q_ref, k_ref, v_ref, kc_ref, vc_ref, bias_ref, o_ref):
    rb = pl.program_id(2)
    rows_n = SEQ // GRID_W
    band_start = jnp.clip(rb * NAT_QR - WIN_R // 2, 0, rows_n - NAT_BAND)
    start = pl.multiple_of(band_start * GRID_W, GRID_W)
    n_keys = NAT_BAND * GRID_W
    q = q_ref[...]
    kb = k_ref[pl.ds(start, n_keys), :]
    vb = v_ref[pl.ds(start, n_keys), :]
    s_loc = _dot_nt(q, kb) + bias_ref[0, 0]
    s_ctx = _dot_nt(q, kc_ref[...])
    m = jnp.maximum(jnp.max(s_loc, axis=1, keepdims=True), jnp.max(s_ctx, axis=1, keepdims=True))
    p_loc = jnp.exp(s_loc - m)
    p_ctx = jnp.exp(s_ctx - m)
    denom = jnp.sum(p_loc, axis=1, keepdims=True) + jnp.sum(p_ctx, axis=1, keepdims=True)
    o = _dot(p_loc.astype(BF16), vb) + _dot(p_ctx.astype(BF16), vc_ref[...])
    o_ref[...] = (o / denom).astype(o_ref.dtype)


def _nat_attention(p, bias_tab):
    n_rb = SEQ // (NAT_QR * GRID_W)
    tq = NAT_QR * GRID_W
    hq = D // NAT_DH

    def pattern(rb):
        return jnp.where(rb == 0, 0, jnp.where(rb == n_rb - 1, 2, 1))

    return pl.pallas_call(
        _nat_kernel,
        out_shape=jax.ShapeDtypeStruct((R_LAT, D), BF16),
        grid=(B, NAT_H, n_rb),
        in_specs=[
            pl.BlockSpec((tq, NAT_DH), lambda b, h, rb: (b * n_rb + rb, h)),
            pl.BlockSpec((SEQ, NAT_DH), lambda b, h, rb: (b, hq + h)),
            pl.BlockSpec((SEQ, NAT_DH), lambda b, h, rb: (b, 2 * hq + h)),
            pl.BlockSpec((CTX, NAT_DH), lambda b, h, rb: (R_LAT // CTX + b, hq + h)),
            pl.BlockSpec((CTX, NAT_DH), lambda b, h, rb: (R_LAT // CTX + b, 2 * hq + h)),
            pl.BlockSpec((1, 1, tq, NAT_BAND * GRID_W), lambda b, h, rb: (pattern(rb), h, 0, 0)),
        ],
        out_specs=pl.BlockSpec((tq, NAT_DH), lambda b, h, rb: (b * n_rb + rb, h)),
        compiler_params=_cparams(3),
        name="nat_attention",
    )(p, p, p, p, p, bias_tab)


MIX_TM = 256


def _layer_norm(u, g, b):
    mu = jnp.mean(u, axis=-1, keepdims=True)
    var = jnp.mean(jnp.square(u - mu), axis=-1, keepdims=True)
    return (u - mu) * lax.rsqrt(var + LN_EPS) * g + b


def _mix_out_kernel(*refs, gla):
    if gla:
        of_ref, ob_ref, r_ref, ng_ref = refs[:4]
        refs = refs[4:]
    else:
        x_ref = refs[0]
        refs = refs[1:]
    h_ref, mod_ref, w_ref, lng_ref, lnb_ref, wr_ref, br_ref, h1_ref, f_ref, lg_ref = refs
    if gla:
        parts = []
        for hd in range(GLA_H):
            cs = slice(hd * GLA_DV, (hd + 1) * GLA_DV)
            o = of_ref[:, cs] + ob_ref[:, cs]
            o = o * lax.rsqrt(jnp.mean(jnp.square(o), axis=-1, keepdims=True) + LN_EPS) * ng_ref[...]
            r = r_ref[:, cs].astype(F32)
            parts.append((o * (r * jax.nn.sigmoid(r))).astype(BF16))
        x = jnp.concatenate(parts, axis=1)
    else:
        x = x_ref[...]
    y = _dot(x, w_ref[...])
    gate1 = mod_ref[0, 2:3, :]
    h1 = _layer_norm(ALPHA * h_ref[...] + gate1 * y, lng_ref[...], lnb_ref[...])
    h1_ref[...] = h1
    f = h1 * (1.0 + mod_ref[0, 4:5, :]) + mod_ref[0, 3:4, :]
    f_ref[...] = f
    lg_ref[...] = _dot_hi(f, wr_ref[...]) + br_ref[...]


def _mix_out(mix_inputs, h, mod, w_out, ln_g, ln_b, w_router, b_router, rows, gla):
    row_spec = lambda width: pl.BlockSpec((MIX_TM, width), lambda i: (i, 0))
    full = lambda a: pl.BlockSpec(a.shape, lambda i: (0,) * a.ndim)
    if gla:
        o_f, o_b, p, norm_g = mix_inputs
        in_specs = [row_spec(GLA_VD), row_spec(GLA_VD),
                    pl.BlockSpec((MIX_TM, GLA_VD), lambda i: (i, (2 * GLA_QK + GLA_VD) // GLA_VD)),
                    full(norm_g)]
        args = [o_f, o_b, p, norm_g]
    else:
        (x,) = mix_inputs
        in_specs = [row_spec(D)]
        args = [x]
    in_specs += [row_spec(D), pl.BlockSpec((1, 6, D), lambda i: (_mod_row(MIX_TM)(i), 0, 0)),
                 full(w_out), full(ln_g), full(ln_b), full(w_router), full(b_router)]
    args += [h, mod, w_out, ln_g, ln_b, w_router, b_router]
    return pl.pallas_call(
        functools.partial(_mix_out_kernel, gla=gla),
        out_shape=[jax.ShapeDtypeStruct((rows, D), F32), jax.ShapeDtypeStruct((rows, D), F32),
                   jax.ShapeDtypeStruct((rows, LANE), F32)],
        grid=(rows // MIX_TM,),
        in_specs=in_specs,
        out_specs=[row_spec(D), row_spec(D), row_spec(LANE)],
        compiler_params=_cparams(1),
        name="gla_out_ln" if gla else "nat_out_ln",
    )(*args)


def _route(logits, b_dummy=None):
    t = logits.shape[0]
    g_logits = logits[:, :N_GROUPS]
    g_p, g_idx = lax.top_k(jax.nn.softmax(g_logits, axis=-1), 1)
    e_logits = logits[:, N_GROUPS:N_GROUPS + N_EXPERTS].reshape(t, N_GROUPS, EPG)
    e_in = jnp.take_along_axis(e_logits, g_idx[:, :, None], axis=1)[:, 0]
    e_p, e_idx = lax.top_k(jax.nn.softmax(e_in, axis=-1), 2)
    e_p = e_p / jnp.sum(e_p, axis=-1, keepdims=True)
    weight = (g_p * e_p).reshape(-1)
    expert = (g_idx * EPG + e_idx).reshape(-1).astype(jnp.int32)
    n_assign = 2 * t
    n_tiles = n_assign // MOE_TM + N_EXPERTS
    cap = n_tiles * MOE_TM
    order = jnp.argsort(expert, stable=True).astype(jnp.int32)
    counts = jnp.sum((expert[:, None] == jnp.arange(N_EXPERTS)[None, :]).astype(jnp.int32), axis=0)
    starts = jnp.cumsum(counts) - counts
    tiles_per = (counts + MOE_TM - 1) // MOE_TM
    tile_end = jnp.cumsum(tiles_per)
    tile_start = tile_end - tiles_per
    used = tile_end[-1]
    tile_ids = jnp.arange(n_tiles, dtype=jnp.int32)
    tile_valid = (tile_ids < used).astype(jnp.int32)
    lookup = jnp.minimum(tile_ids, used - 1)
    tile_expert = jnp.minimum(jnp.searchsorted(tile_end, lookup, side='right'), N_EXPERTS - 1).astype(jnp.int32)
    slot = jnp.arange(cap, dtype=jnp.int32)
    s_exp = tile_expert[slot // MOE_TM]
    local = slot - tile_start[s_exp] * MOE_TM
    s_valid = (tile_valid[slot // MOE_TM] == 1) & (local < counts[s_exp])
    assign = order[jnp.clip(starts[s_exp] + local, 0, n_assign - 1)]
    pad_rank = jnp.cumsum((~s_valid).astype(jnp.int32)) - 1
    src = jnp.where(s_valid, assign // 2, 0).astype(jnp.int32)
    dst = jnp.where(s_valid, assign, n_assign + pad_rank).astype(jnp.int32)
    s_w = jnp.where(s_valid, weight[assign], 0.0)
    s_w = jnp.broadcast_to(s_w[:, None], (cap, LANE))
    return tile_expert, tile_valid, src, dst, s_w, n_tiles


def _moe_kernel(te_ref, tv_ref, src_ref, dst_ref, f_hbm, w1_ref, w3_ref, w2_ref, sw_ref, out_hbm,
                xbuf, ybuf, gsem, ssem):
    i = pl.program_id(0)
    n = pl.num_programs(0)
    slot = i % 2

    def gather(tile, sl, start):
        if not start:
            pltpu.make_async_copy(f_hbm.at[pl.ds(0, MOE_TM), :], xbuf.at[sl], gsem.at[sl]).wait()
            return

        def body(r, carry):
            tok = src_ref[tile * MOE_TM + r]
            pltpu.make_async_copy(f_hbm.at[pl.ds(tok, 1), :], xbuf.at[sl, pl.ds(r, 1), :], gsem.at[sl]).start()
            return carry

        lax.fori_loop(0, MOE_TM, body, 0, unroll=8)

    def scatter(tile, sl, start):
        if not start:
            pltpu.make_async_copy(ybuf.at[sl], out_hbm.at[pl.ds(0, MOE_TM), :], ssem.at[sl]).wait()
            return

        def body(r, carry):
            row = dst_ref[tile * MOE_TM + r]
            pltpu.make_async_copy(ybuf.at[sl, pl.ds(r, 1), :], out_hbm.at[pl.ds(row, 1), :], ssem.at[sl]).start()
            return carry

        lax.fori_loop(0, MOE_TM, body, 0, unroll=8)

    @pl.when(i == 0)
    def _():
        gather(0, 0, True)

    @pl.when(i + 1 < n)
    def _():
        gather(i + 1, 1 - slot, True)

    gather(i, slot, False)

    @pl.when(i >= 2)
    def _():
        scatter(i - 2, slot, False)

    @pl.when(tv_ref[i] != 0)
    def _():
        x = xbuf[slot].astype(BF16)
        h1 = _dot(x, w1_ref[0])
        h3 = _dot(x, w3_ref[0])
        a = (h1 * jax.nn.sigmoid(h1) * h3).astype(BF16)
        y = _dot(a, w2_ref[0])
        sw = jnp.concatenate([sw_ref[...]] * (D // LANE), axis=1)
        ybuf[slot] = y * sw

    @pl.when(tv_ref[i] == 0)
    def _():
        ybuf[slot] = jnp.zeros((MOE_TM, D), F32)

    scatter(i, slot, True)

    @pl.when(i == n - 1)
    def _():
        scatter(i, slot, False)

        @pl.when(n >= 2)
        def _():
            scatter(i - 1, 1 - slot, False)


def _moe(f, logits, w1, w3, w2):
    t = f.shape[0]
    tile_expert, tile_valid, src, dst, s_w, n_tiles = _route(logits)
    cap = n_tiles * MOE_TM
    grid_spec = pltpu.PrefetchScalarGridSpec(
        num_scalar_prefetch=4,
        grid=(n_tiles,),
        in_specs=[
            pl.BlockSpec(memory_space=pl.ANY),
            pl.BlockSpec((1, D, D_EXPERT), lambda i, te, tv, s, d: (te[i], 0, 0)),
            pl.BlockSpec((1, D, D_EXPERT), lambda i, te, tv, s, d: (te[i], 0, 0)),
            pl.BlockSpec((1, D_EXPERT, D), lambda i, te, tv, s, d: (te[i], 0, 0)),
            pl.BlockSpec((MOE_TM, LANE), lambda i, te, tv, s, d: (i, 0)),
        ],
        out_specs=pl.BlockSpec(memory_space=pl.ANY),
        scratch_shapes=[
            pltpu.VMEM((2, MOE_TM, D), F32),
            pltpu.VMEM((2, MOE_TM, D), F32),
            pltpu.SemaphoreType.DMA((2,)),
            pltpu.SemaphoreType.DMA((2,)),
        ],
    )
    out = pl.pallas_call(
        _moe_kernel,
        out_shape=jax.ShapeDtypeStruct((cap, D), F32),
        grid_spec=grid_spec,
        compiler_params=_cparams(1),
        name="moe_experts",
    )(tile_expert, tile_valid, src, dst, f, w1, w3, w2, s_w)
    return out.reshape(cap // 2, 2 * D)


FIN_TM = 512


def _final_kernel(h_ref, y_ref, mod_ref, lng_ref, lnb_ref, o_ref):
    y = y_ref[:, :D] + y_ref[:, D:]
    gate2 = mod_ref[0, 5:6, :]
    o_ref[...] = _layer_norm(ALPHA * h_ref[...] + gate2 * y, lng_ref[...], lnb_ref[...])


def _ffn_residual(h1, y_pair, mod, ln_g, ln_b):
    rows = h1.shape[0]
    return pl.pallas_call(
        _final_kernel,
        out_shape=jax.ShapeDtypeStruct((rows, D), F32),
        grid=(rows // FIN_TM,),
        in_specs=[
            pl.BlockSpec((FIN_TM, D), lambda i: (i, 0)),
            pl.BlockSpec((FIN_TM, 2 * D), lambda i: (i, 0)),
            pl.BlockSpec((1, 6, D), lambda i: (_mod_row(FIN_TM)(i), 0, 0)),
            pl.BlockSpec((1, D), lambda i: (0, 0)),
            pl.BlockSpec((1, D), lambda i: (0, 0)),
        ],
        out_specs=pl.BlockSpec((FIN_TM, D), lambda i: (i, 0)),
        compiler_params=_cparams(1),
        name="ffn_residual_ln",
    )(h1, y_pair, mod, ln_g, ln_b)


def _router_weights(w_group, b_group, w_expert, b_expert):
    pad = LANE - N_GROUPS - N_EXPERTS
    w = jnp.concatenate([w_group, w_expert, jnp.zeros((D, pad), F32)], axis=1)
    b = jnp.concatenate([b_group, b_expert, jnp.zeros((pad,), F32)])[None, :]
    return w, b


def kernel(x, c, ctx, c_ctx, ada_w, ada_b, ln_g, ln_b, gla_w_in, gla_w_gate, gla_b_gate, gla_norm_g, gla_w_out,
           nat_w_in, nat_rpb, nat_w_out, moe_w_group, moe_b_group, moe_w_expert, moe_b_expert,
           moe_w1, moe_w3, moe_w2):
    h = jnp.concatenate([x.reshape(R_LAT, D), ctx.reshape(R_CTX, D)], axis=0)
    cc = jnp.concatenate([c, c_ctx[None, :], jnp.zeros((SUBLANE - B - 1, D), F32)], axis=0)
    mod = _ada_table(cc, ada_w, ada_b).reshape(DEPTH, SUBLANE, 6, D)

    w_in = gla_w_in[0]
    w_main = jnp.concatenate([w_in[:, :GLA_QK] * (GLA_DK ** -0.5), w_in[:, GLA_QK:GLA_MAIN]], axis=1).astype(BF16)
    w_glr = jnp.concatenate([w_in[:, GLA_MAIN:], jnp.zeros((D, LANE - 2 * GLA_RANK), F32)], axis=1).astype(BF16)
    p, glr = _proj(h, mod[0], w_main, w_glr)
    wg = jnp.zeros((2, LANE, GLA_QK), F32)
    wg = wg.at[0, :GLA_RANK].set(gla_w_gate[0, 0]).at[1, GLA_RANK:2 * GLA_RANK].set(gla_w_gate[0, 1])
    bg = gla_b_gate[0][:, None, :]
    cos, sin = _rope_tables()
    o_f = _gla_scan(p, glr, wg, bg, cos, sin, reverse=False)
    o_b = _gla_scan(p, glr, wg, bg, cos, sin, reverse=True)
    w_r, b_r = _router_weights(moe_w_group[0], moe_b_group[0], moe_w_expert[0], moe_b_expert[0])
    h1, f, logits = _mix_out((o_f, o_b, p, gla_norm_g[0][None, :]), h, mod[0], gla_w_out[0].astype(BF16),
                             ln_g[0, 0][None, :], ln_b[0, 0][None, :], w_r, b_r, R_ALL, gla=True)
    y_pair = _moe(f, logits, moe_w1[0].astype(BF16), moe_w3[0].astype(BF16), moe_w2[0].astype(BF16))
    h = _ffn_residual(h1, y_pair, mod[0], ln_g[0, 1][None, :], ln_b[0, 1][None, :])

    w_in = nat_w_in[0]
    w_nat = jnp.concatenate([w_in[:, :D] * (NAT_DH ** -0.5), w_in[:, D:]], axis=1).astype(BF16)
    p = _proj(h, mod[1], w_nat)
    o = _nat_attention(p, _nat_bias_table(nat_rpb[0]))
    w_r, b_r = _router_weights(moe_w_group[1], moe_b_group[1], moe_w_expert[1], moe_b_expert[1])
    h1, f, logits = _mix_out((o,), h, mod[1], nat_w_out[0].astype(BF16),
                             ln_g[1, 0][None, :], ln_b[1, 0][None, :], w_r, b_r, R_LAT, gla=False)
    y_pair = _moe(f, logits, moe_w1[1].astype(BF16), moe_w3[1].astype(BF16), moe_w2[1].astype(BF16))
    out = _ffn_residual(h1, y_pair, mod[1], ln_g[1, 1][None, :], ln_b[1, 1][None, :])
    return out.reshape(B, SEQ, D)
```

```python
import functools

import jax
import jax.numpy as jnp
import numpy as np
from jax import lax
from jax.experimental import pallas as pl
from jax.experimental.pallas import tpu as pltpu

F32 = jnp.float32
BF16 = jnp.bfloat16

D = 2048
B = 4
SEQ = 4096
DEPTH = 2
GRID_W = 64
CTX = 256
R_LAT = B * SEQ
R_CTX = B * CTX
R_ALL = R_LAT + R_CTX

GLA_H = 4
GLA_DK = 256
GLA_DV = 512
GLA_CHUNK = 64
GLA_RANK = 16
GLA_QK = GLA_H * GLA_DK
GLA_VD = GLA_H * GLA_DV
GLA_MAIN = 2 * GLA_QK + 2 * GLA_VD
GLA_GATE_NORM = 16.0
ROPE_BASE = 10000.0

NAT_H = 16
NAT_DH = 128
WIN_R = 8
WIN_C = 16
NAT_QR = 8
NAT_BAND = 16

N_GROUPS = 4
EPG = 4
N_EXPERTS = 16
D_EXPERT = 1024
MOE_TM = 256

ALPHA = (2 * DEPTH) ** 0.25
LN_EPS = 1e-5
NEG_INF = -1e30

LANE = 128
SUBLANE = 8
VMEM_LIMIT = 56 * 1024 * 1024


def _cparams(n_axes):
    return pltpu.CompilerParams(dimension_semantics=("arbitrary",) * n_axes,
                                vmem_limit_bytes=VMEM_LIMIT)


def _split_bf16(x):
    hi = x.astype(BF16)
    lo = (x - hi.astype(F32)).astype(BF16)
    return hi, lo


def _dot(a, b):
    return jnp.dot(a, b, preferred_element_type=F32)


def _dot_hi(a, b):
    ah, al = _split_bf16(a)
    bh, bl = _split_bf16(b)
    return _dot(ah, bh) + _dot(al, bh) + _dot(ah, bl)


def _dot_nt(a, b):
    return lax.dot_general(a, b, (((1,), (1,)), ((), ())), preferred_element_type=F32)


def _dot_tn(a, b):
    return lax.dot_general(a, b, (((0,), (0,)), ((), ())), preferred_element_type=F32)


def _mod_row(tile_rows):
    return lambda i: jnp.minimum((i * tile_rows) // SEQ, B)


ADA_TN = 1024


def _ada_kernel(c_ref, w_ref, b_ref, o_ref):
    x = c_ref[...]
    s = x * jax.nn.sigmoid(x)
    o_ref[0] = _dot_hi(s, w_ref[0]) + b_ref[0]


def _ada_table(cc, ada_w, ada_b):
    n = 6 * D
    return pl.pallas_call(
        _ada_kernel,
        out_shape=jax.ShapeDtypeStruct((DEPTH, SUBLANE, n), F32),
        grid=(DEPTH, n // ADA_TN),
        in_specs=[
            pl.BlockSpec((SUBLANE, D), lambda l, j: (0, 0)),
            pl.BlockSpec((1, D, ADA_TN), lambda l, j: (l, 0, j)),
            pl.BlockSpec((1, 1, ADA_TN), lambda l, j: (l, 0, j)),
        ],
        out_specs=pl.BlockSpec((1, SUBLANE, ADA_TN), lambda l, j: (l, 0, j)),
        compiler_params=_cparams(2),
        name="ada_table",
    )(cc, ada_w, ada_b.reshape(DEPTH, 1, n))


PROJ_TM = 1024
PROJ_TN = 1024


def _proj_kernel(h_ref, mod_ref, w_ref, *rest, with_extra):
    if with_extra:
        wx_ref, o_ref, ox_ref, a_scr = rest
    else:
        o_ref, a_scr = rest
    j = pl.program_id(1)

    @pl.when(j == 0)
    def _():
        sh = mod_ref[0, 0:1, :]
        sc = mod_ref[0, 1:2, :]
        a_scr[...] = (h_ref[...] * (1.0 + sc) + sh).astype(BF16)
        if with_extra:
            ox_ref[...] = _dot(a_scr[...], wx_ref[...])

    o_ref[...] = _dot(a_scr[...], w_ref[...]).astype(o_ref.dtype)


def _proj(h, mod, w, w_extra=None):
    rows = h.shape[0]
    n = w.shape[1]
    with_extra = w_extra is not None
    in_specs = [
        pl.BlockSpec((PROJ_TM, D), lambda i, j: (i, 0)),
        pl.BlockSpec((1, 6, D), lambda i, j: (_mod_row(PROJ_TM)(i), 0, 0)),
        pl.BlockSpec((D, PROJ_TN), lambda i, j: (0, j)),
    ]
    out_shape = [jax.ShapeDtypeStruct((rows, n), BF16)]
    out_specs = [pl.BlockSpec((PROJ_TM, PROJ_TN), lambda i, j: (i, j))]
    args = [h, mod, w]
    if with_extra:
        in_specs.append(pl.BlockSpec((D, LANE), lambda i, j: (0, 0)))
        out_shape.append(jax.ShapeDtypeStruct((rows, LANE), F32))
        out_specs.append(pl.BlockSpec((PROJ_TM, LANE), lambda i, j: (i, 0)))
        args.append(w_extra)
    out = pl.pallas_call(
        functools.partial(_proj_kernel, with_extra=with_extra),
        out_shape=out_shape,
        grid=(rows // PROJ_TM, n // PROJ_TN),
        in_specs=in_specs,
        out_specs=out_specs,
        scratch_shapes=[pltpu.VMEM((PROJ_TM, D), BF16)],
        compiler_params=_cparams(2),
        name="mod_proj",
    )(*args)
    return out if with_extra else out[0]


GLA_TB = 256
GLA_STEPS = 1 + SEQ // GLA_TB


def _rope_tables():
    half = GLA_DK // 4
    freqs = ROPE_BASE ** (-np.arange(half, dtype=np.float64) / half)
    t = np.arange(SEQ)
    ang_r = (t // GRID_W)[:, None] * freqs
    ang_c = (t % GRID_W)[:, None] * freqs
    cos = np.concatenate([np.cos(ang_r)] * 2 + [np.cos(ang_c)] * 2, axis=1)
    sin = np.concatenate([-np.sin(ang_r), np.sin(ang_r), -np.sin(ang_c), np.sin(ang_c)], axis=1)
    cos = np.concatenate([cos, np.ones((GLA_TB, GLA_DK))], axis=0)
    sin = np.concatenate([sin, np.zeros((GLA_TB, GLA_DK))], axis=0)
    return jnp.asarray(cos, F32), jnp.asarray(sin, F32)


def _rope(x, cos, sin):
    half = LANE // 2
    parts = [pltpu.roll(x[:, g * LANE:(g + 1) * LANE], half, 1) for g in range(x.shape[1] // LANE)]
    return x * cos + jnp.concatenate(parts, axis=1) * sin


def _gla_kernel(q_ref, k_ref, v_ref, glr_ref, wg_ref, bg_ref, cos_ref, sin_ref, o_ref, s_scr, *, reverse):
    @pl.when(pl.program_id(2) == 0)
    def _():
        s_scr[...] = jnp.zeros_like(s_scr)

    row = lax.broadcasted_iota(jnp.int32, (GLA_CHUNK, GLA_CHUNK), 0)
    col = lax.broadcasted_iota(jnp.int32, (GLA_CHUNK, GLA_CHUNK), 1)
    keep = (col >= row) if reverse else (col <= row)
    tri = jnp.where(keep, 1.0, 0.0).astype(BF16)
    ones = jnp.ones((GLA_CHUNK, LANE), BF16)
    n_chunks = GLA_TB // GLA_CHUNK
    order = range(n_chunks - 1, -1, -1) if reverse else range(n_chunks)
    for ci in order:
        sl = slice(ci * GLA_CHUNK, (ci + 1) * GLA_CHUNK)
        cos = cos_ref[sl, :]
        sin = sin_ref[sl, :]
        q = _rope(q_ref[sl, :].astype(F32), cos, sin)
        k = _rope(k_ref[sl, :].astype(F32), cos, sin)
        v = v_ref[sl, :]
        z = _dot_hi(glr_ref[sl, :], wg_ref[0]) + bg_ref[0]
        g = (jnp.minimum(z, 0.0) - jnp.log1p(jnp.exp(-jnp.abs(z)))) * (1.0 / GLA_GATE_NORM)
        g_hi, g_lo = _split_bf16(g)
        cum = _dot(tri, g_hi) + _dot(tri, g_lo)
        tot = cum[0:1, :] if reverse else cum[GLA_CHUNK - 1:GLA_CHUNK, :]
        q_dec = (q * jnp.exp(cum)).astype(BF16)
        k_inv = (k * jnp.exp(-cum)).astype(BF16)
        k_end = (k * jnp.exp(tot - cum)).astype(BF16)
        att = jnp.where(keep, _dot_nt(q_dec, k_inv), 0.0).astype(BF16)
        state = s_scr[...]
        o_ref[sl, :] = _dot(att, v) + _dot(q_dec, state.astype(BF16))
        decay = jnp.exp(_dot_tn(g_hi, ones) + _dot_tn(g_lo, ones))
        decay = jnp.concatenate([decay] * (GLA_DV // LANE), axis=1)
        s_scr[...] = state * decay + _dot_tn(k_end, v)


def _gla_scan(p, glr, wg, bg, cos, sin, reverse):
    lat_blocks = SEQ // GLA_TB

    def tblock(s):
        if reverse:
            return jnp.where(s == 0, 0, GLA_STEPS - s)
        return s

    def rows(b, s):
        tb = tblock(s)
        return jnp.where(tb == 0, R_LAT // GLA_TB + b, b * lat_blocks + tb - 1)

    def rope_rows(s):
        tb = tblock(s)
        return jnp.where(tb == 0, lat_blocks, tb - 1)

    d = 1 if reverse else 0
    kq = GLA_QK // GLA_DK
    return pl.pallas_call(
        functools.partial(_gla_kernel, reverse=reverse),
        out_shape=jax.ShapeDtypeStruct((R_ALL, GLA_VD), F32),
        grid=(B, GLA_H, GLA_STEPS),
        in_specs=[
            pl.BlockSpec((GLA_TB, GLA_DK), lambda b, h, s: (rows(b, s), h)),
            pl.BlockSpec((GLA_TB, GLA_DK), lambda b, h, s: (rows(b, s), kq + h)),
            pl.BlockSpec((GLA_TB, GLA_DV), lambda b, h, s: (rows(b, s), 2 * GLA_QK // GLA_DV + h)),
            pl.BlockSpec((GLA_TB, LANE), lambda b, h, s: (rows(b, s), 0)),
            pl.BlockSpec((1, LANE, GLA_DK), lambda b, h, s: (d, 0, h)),
            pl.BlockSpec((1, 1, GLA_DK), lambda b, h, s: (d, 0, h)),
            pl.BlockSpec((GLA_TB, GLA_DK), lambda b, h, s: (rope_rows(s), 0)),
            pl.BlockSpec((GLA_TB, GLA_DK), lambda b, h, s: (rope_rows(s), 0)),
        ],
        out_specs=pl.BlockSpec((GLA_TB, GLA_DV), lambda b, h, s: (rows(b, s), h)),
        scratch_shapes=[pltpu.VMEM((GLA_DK, GLA_DV), F32)],
        compiler_params=_cparams(3),
        name="gla_scan_bwd" if reverse else "gla_scan_fwd",
    )(p, p, p, glr, wg, bg, cos, sin)


def _nat_bias_table(rpb):
    qc = np.arange(GRID_W)[:, None]
    kc = np.arange(GRID_W)[None, :]
    cs = np.clip(qc - WIN_C // 2, 0, GRID_W - WIN_C)
    col_ok = (kc >= cs) & (kc < cs + WIN_C)
    dc = np.clip(kc - qc + WIN_C - 1, 0, 2 * WIN_C - 2)
    n_dr = 2 * WIN_R - 1
    tc = jnp.where(col_ok[None, None], rpb[:, :, dc].astype(F32), NEG_INF)
    tc = jnp.concatenate([tc, jnp.full((NAT_H, 1, GRID_W, GRID_W), NEG_INF, F32)], axis=1)
    rows_n = SEQ // GRID_W
    sel = np.zeros((3, NAT_QR, NAT_BAND, n_dr + 1), np.float32)
    for p_i, (r0, bs) in enumerate(((0, 0), (NAT_QR, NAT_QR - WIN_R // 2), (rows_n - NAT_QR, rows_n - NAT_BAND))):
        for i in range(NAT_QR):
            qr = r0 + i
            rs = min(max(qr - WIN_R // 2, 0), rows_n - WIN_R)
            for j in range(NAT_BAND):
                kr = bs + j
                a = kr - qr + WIN_R - 1 if rs <= kr < rs + WIN_R else n_dr
                sel[p_i, i, j, a] = 1.0
    tab = jnp.einsum('pija,haqk->phiqjk', jnp.asarray(sel), tc, precision=lax.Precision.HIGHEST)
    return tab.reshape(3, NAT_H, NAT_QR * GRID_W, NAT_BAND * GRID_W)


def _nat_kernel(q_ref, k_ref, v_ref, kc_ref, vc_ref, bias_ref, o_ref):
    rb = pl.program_id(2)
    rows_n = SEQ // GRID_W
    band_start = jnp.clip(rb * NAT_QR - WIN_R // 2, 0, rows_n - NAT_BAND)
    start = pl.multiple_of(band_start * GRID_W, GRID_W)
    n_keys = NAT_BAND * GRID_W
    q = q_ref[...]
    kb = k_ref[pl.ds(start, n_keys), :]
    vb = v_ref[pl.ds(start, n_keys), :]
    s_loc = _dot_nt(q, kb) + bias_ref[0, 0]
    s_ctx = _dot_nt(q, kc_ref[...])
    m = jnp.maximum(jnp.max(s_loc, axis=1, keepdims=True), jnp.max(s_ctx, axis=1, keepdims=True))
    p_loc = jnp.exp(s_loc - m)
    p_ctx = jnp.exp(s_ctx - m)
    denom = jnp.sum(p_loc, axis=1, keepdims=True) + jnp.sum(p_ctx, axis=1, keepdims=True)
    o = _dot(p_loc.astype(BF16), vb) + _dot(p_ctx.astype(BF16), vc_ref[...])
    o_ref[...] = (o / denom).astype(o_ref.dtype)


def _nat_attention(p, bias_tab):
    n_rb = SEQ // (NAT_QR * GRID_W)
    tq = NAT_QR * GRID_W
    hq = D // NAT_DH

    def pattern(rb):
        return jnp.where(rb == 0, 0, jnp.where(rb == n_rb - 1, 2, 1))

    return pl.pallas_call(
        _nat_kernel,
        out_shape=jax.ShapeDtypeStruct((R_LAT, D), BF16),
        grid=(B, NAT_H, n_rb),
        in_specs=[
            pl.BlockSpec((tq, NAT_DH), lambda b, h, rb: (b * n_rb + rb, h)),
            pl.BlockSpec((SEQ, NAT_DH), lambda b, h, rb: (b, hq + h)),
            pl.BlockSpec((SEQ, NAT_DH), lambda b, h, rb: (b, 2 * hq + h)),
            pl.BlockSpec((CTX, NAT_DH), lambda b, h, rb: (R_LAT // CTX + b, hq + h)),
            pl.BlockSpec((CTX, NAT_DH), lambda b, h, rb: (R_LAT // CTX + b, 2 * hq + h)),
            pl.BlockSpec((1, 1, tq, NAT_BAND * GRID_W), lambda b, h, rb: (pattern(rb), h, 0, 0)),
        ],
        out_specs=pl.BlockSpec((tq, NAT_DH), lambda b, h, rb: (b * n_rb + rb, h)),
        compiler_params=_cparams(3),
        name="nat_attention",
    )(p, p, p, p, p, bias_tab)


MIX_TM = 256


def _layer_norm(u, g, b):
    mu = jnp.mean(u, axis=-1, keepdims=True)
    var = jnp.mean(jnp.square(u - mu), axis=-1, keepdims=True)
    return (u - mu) * lax.rsqrt(var + LN_EPS) * g + b


def _mix_out_kernel(*refs, gla):
    if gla:
        of_ref, ob_ref, r_ref, ng_ref = refs[:4]
        refs = refs[4:]
    else:
        x_ref = refs[0]
        refs = refs[1:]
    h_ref, mod_ref, w_ref, lng_ref, lnb_ref, wr_ref, br_ref, h1_ref, f_ref, rt_ref, cnt_ref, cnt_scr = refs

    @pl.when(pl.program_id(0) == 0)
    def _():
        cnt_scr[...] = jnp.zeros_like(cnt_scr)

    if gla:
        parts = []
        for hd in range(GLA_H):
            cs = slice(hd * GLA_DV, (hd + 1) * GLA_DV)
            o = of_ref[:, cs] + ob_ref[:, cs]
            o = o * lax.rsqrt(jnp.mean(jnp.square(o), axis=-1, keepdims=True) + LN_EPS) * ng_ref[...]
            r = r_ref[:, cs].astype(F32)
            parts.append((o * (r * jax.nn.sigmoid(r))).astype(BF16))
        x = jnp.concatenate(parts, axis=1)
    else:
        x = x_ref[...]
    y = _dot(x, w_ref[...])
    gate1 = mod_ref[0, 2:3, :]
    h1 = _layer_norm(ALPHA * h_ref[...] + gate1 * y, lng_ref[...], lnb_ref[...])
    h1_ref[...] = h1
    f = h1 * (1.0 + mod_ref[0, 4:5, :]) + mod_ref[0, 3:4, :]
    f_ref[...] = f
    route, counts = _route_tile(_dot_hi(f, wr_ref[...]) + br_ref[...], cnt_scr[...])
    rt_ref[...] = route
    cnt_scr[...] = counts
    cnt_ref[...] = jnp.broadcast_to(counts, cnt_ref.shape)


RT_E, RT_W, RT_RANK = 0, 2, 4


def _route_tile(logits, run):
    tm = logits.shape[0]
    lane = lax.broadcasted_iota(jnp.int32, (tm, LANE), 1).astype(F32)

    def first_max(vals):
        top = jnp.max(vals, axis=1, keepdims=True)
        return top, jnp.min(jnp.where(vals == top, lane, float(LANE)), axis=1, keepdims=True)

    gl = jnp.where(lane < N_GROUPS, logits, NEG_INF)
    g_max, g_idx = first_max(gl)
    g_p = 1.0 / jnp.sum(jnp.exp(gl - g_max), axis=1, keepdims=True)
    lo = N_GROUPS + EPG * g_idx
    el = jnp.where((lane >= lo) & (lane < lo + EPG), logits, NEG_INF)
    e1, i1 = first_max(el)
    e2, i2 = first_max(jnp.where(lane == i1, NEG_INF, el))
    t = jnp.exp(e2 - e1)
    p1 = 1.0 / (1.0 + t)
    x1 = i1 - N_GROUPS
    x2 = i2 - N_GROUPS
    onehot = jnp.where((lane == x1) | (lane == x2), 1.0, 0.0)
    row = lax.broadcasted_iota(jnp.int32, (tm, tm), 0)
    col = lax.broadcasted_iota(jnp.int32, (tm, tm), 1)
    before = jnp.where(col < row, 1.0, 0.0).astype(BF16)
    seen = _dot(before, onehot.astype(BF16)) + run
    rank1 = jnp.sum(jnp.where(lane == x1, seen, 0.0), axis=1, keepdims=True)
    rank2 = jnp.sum(jnp.where(lane == x2, seen, 0.0), axis=1, keepdims=True)
    route = jnp.zeros((tm, LANE), F32)
    for k, val in ((RT_E, x1), (RT_E + 1, x2), (RT_W, g_p * p1), (RT_W + 1, g_p * p1 * t),
                   (RT_RANK, rank1), (RT_RANK + 1, rank2)):
        route = jnp.where(lane == k, val, route)
    return route, run + jnp.sum(onehot, axis=0, keepdims=True)


def _mix_out(mix_inputs, h, mod, w_out, ln_g, ln_b, w_router, b_router, rows, gla):
    row_spec = lambda width: pl.BlockSpec((MIX_TM, width), lambda i: (i, 0))
    full = lambda a: pl.BlockSpec(a.shape, lambda i: (0,) * a.ndim)
    if gla:
        o_f, o_b, p, norm_g = mix_inputs
        in_specs = [row_spec(GLA_VD), row_spec(GLA_VD),
                    pl.BlockSpec((MIX_TM, GLA_VD), lambda i: (i, (2 * GLA_QK + GLA_VD) // GLA_VD)),
                    full(norm_g)]
        args = [o_f, o_b, p, norm_g]
    else:
        (x,) = mix_inputs
        in_specs = [row_spec(D)]
        args = [x]
    in_specs += [row_spec(D), pl.BlockSpec((1, 6, D), lambda i: (_mod_row(MIX_TM)(i), 0, 0)),
                 full(w_out), full(ln_g), full(ln_b), full(w_router), full(b_router)]
    args += [h, mod, w_out, ln_g, ln_b, w_router, b_router]
    return pl.pallas_call(
        functools.partial(_mix_out_kernel, gla=gla),
        out_shape=[jax.ShapeDtypeStruct((rows, D), F32), jax.ShapeDtypeStruct((rows, D), F32),
                   jax.ShapeDtypeStruct((rows, LANE), F32), jax.ShapeDtypeStruct((SUBLANE, LANE), F32)],
        grid=(rows // MIX_TM,),
        in_specs=in_specs,
        out_specs=[row_spec(D), row_spec(D), row_spec(LANE), pl.BlockSpec((SUBLANE, LANE), lambda i: (0, 0))],
        scratch_shapes=[pltpu.VMEM((1, LANE), F32)],
        compiler_params=_cparams(1),
        name="gla_out_ln" if gla else "nat_out_ln",
    )(*args)


def _plan(route, counts_row):
    t = route.shape[0]
    e = route[:, RT_E:RT_E + 2].astype(jnp.int32)
    rank = route[:, RT_RANK:RT_RANK + 2].astype(jnp.int32)
    counts = counts_row[0, :N_EXPERTS].astype(jnp.int32)
    tiles_per = (counts + MOE_TM - 1) // MOE_TM
    tile_end = jnp.cumsum(tiles_per)
    tile_start = tile_end - tiles_per
    n_tiles = 2 * t // MOE_TM + N_EXPERTS
    ids = jnp.arange(n_tiles, dtype=jnp.int32)
    used = tile_end[-1]
    tile_valid = (ids < used).astype(jnp.int32)
    lookup = jnp.minimum(ids, used - 1)
    tile_expert = jnp.sum((tile_end[None, :] <= lookup[:, None]).astype(jnp.int32), axis=1)
    tile_expert = jnp.minimum(tile_expert, N_EXPERTS - 1).astype(jnp.int32)
    base = tile_start * MOE_TM
    experts = jnp.arange(N_EXPERTS, dtype=jnp.int32)
    pos = jnp.sum(jnp.where(e[..., None] == experts, base, 0), axis=-1) + rank
    trailing = n_tiles - 1 - experts
    fill_tile = jnp.concatenate([jnp.maximum(tile_end - 1, 0), trailing]).astype(jnp.int32)
    fill_on = jnp.concatenate([tiles_per > 0, trailing >= used]).astype(jnp.int32)
    return tile_expert, tile_valid, pos.reshape(-1).astype(jnp.int32), fill_tile, fill_on, n_tiles


def _start_row_copies(pos_ref, tile, rows, make):
    def body(r, carry):
        a = 2 * (tile * rows + r)
        make(r, pos_ref[a], False).start()
        make(r, pos_ref[a + 1], True).start()
        return carry

    lax.fori_loop(0, rows, body, 0, unroll=8)


DISP_TM = 256


def _dispatch_kernel(pos_ref, lt_ref, ht_ref, f_ref, xs_hbm, sbuf, zbuf, sem, zsem):
    i = pl.program_id(0)
    n = pl.num_programs(0)
    slot = i % 2

    def fill(e):
        return pltpu.make_async_copy(zbuf, xs_hbm.at[pl.ds(lt_ref[e] * MOE_TM, MOE_TM), :], zsem)

    @pl.when(i == 0)
    def _():
        zbuf[...] = jnp.zeros_like(zbuf)
        for e in range(2 * N_EXPERTS):
            @pl.when(ht_ref[e] != 0)
            def _():
                fill(e).start()
        for e in range(2 * N_EXPERTS):
            @pl.when(ht_ref[e] != 0)
            def _():
                fill(e).wait()

    def wait_scatter(sl):
        for _ in range(2):
            pltpu.make_async_copy(sbuf.at[sl], xs_hbm.at[pl.ds(0, DISP_TM), :], sem.at[sl]).wait()

    @pl.when(i >= 2)
    def _():
        wait_scatter(slot)

    sbuf[slot] = f_ref[...]

    def make(r, dst, second):
        return pltpu.make_async_copy(sbuf.at[slot, pl.ds(r, 1), :], xs_hbm.at[pl.ds(dst, 1), :], sem.at[slot])

    _start_row_copies(pos_ref, i, DISP_TM, make)

    @pl.when(i == n - 1)
    def _():
        wait_scatter(slot)

        @pl.when(n >= 2)
        def _():
            wait_scatter(1 - slot)


def _dispatch(f, pos, last_tile, has_tile, cap):
    t = f.shape[0]
    grid_spec = pltpu.PrefetchScalarGridSpec(
        num_scalar_prefetch=3,
        grid=(t // DISP_TM,),
        in_specs=[pl.BlockSpec((DISP_TM, D), lambda i, *_: (i, 0))],
        out_specs=pl.BlockSpec(memory_space=pl.ANY),
        scratch_shapes=[
            pltpu.VMEM((2, DISP_TM, D), F32),
            pltpu.VMEM((MOE_TM, D), F32),
            pltpu.SemaphoreType.DMA((2,)),
            pltpu.SemaphoreType.DMA,
        ],
    )
    return pl.pallas_call(
        _dispatch_kernel,
        out_shape=jax.ShapeDtypeStruct((cap, D), F32),
        grid_spec=grid_spec,
        compiler_params=_cparams(1),
        name="moe_dispatch",
    )(pos, last_tile, has_tile, f)


def _moe_kernel(te_ref, tv_ref, x_ref, w1_ref, w3_ref, w2_ref, y_ref):
    i = pl.program_id(0)

    @pl.when(tv_ref[i] != 0)
    def _():
        x = x_ref[...].astype(BF16)
        h1 = _dot(x, w1_ref[0])
        h3 = _dot(x, w3_ref[0])
        a = (h1 * jax.nn.sigmoid(h1) * h3).astype(BF16)
        y_ref[...] = _dot(a, w2_ref[0])

    @pl.when(tv_ref[i] == 0)
    def _():
        y_ref[...] = jnp.zeros_like(y_ref)


def _moe_experts(xs, tile_expert, tile_valid, w1, w3, w2):
    cap = xs.shape[0]
    grid_spec = pltpu.PrefetchScalarGridSpec(
        num_scalar_prefetch=2,
        grid=(cap // MOE_TM,),
        in_specs=[
            pl.BlockSpec((MOE_TM, D), lambda i, te, tv: (i * tv[i], 0)),
            pl.BlockSpec((1, D, D_EXPERT), lambda i, te, tv: (te[i], 0, 0)),
            pl.BlockSpec((1, D, D_EXPERT), lambda i, te, tv: (te[i], 0, 0)),
            pl.BlockSpec((1, D_EXPERT, D), lambda i, te, tv: (te[i], 0, 0)),
        ],
        out_specs=pl.BlockSpec((MOE_TM, D), lambda i, te, tv: (i, 0)),
    )
    return pl.pallas_call(
        _moe_kernel,
        out_shape=jax.ShapeDtypeStruct((cap, D), F32),
        grid_spec=grid_spec,
        compiler_params=_cparams(1),
        name="moe_experts",
    )(tile_expert, tile_valid, xs, w1, w3, w2)


FIN_TM = 256


def _final_kernel(pos_ref, h_ref, rt_ref, mod_ref, lng_ref, lnb_ref, ys_hbm, o_ref, abuf, bbuf, sem):
    i = pl.program_id(0)
    n = pl.num_programs(0)
    slot = i % 2

    def start_gather(tile, sl):
        def make(r, src, second):
            buf = bbuf if second else abuf
            return pltpu.make_async_copy(ys_hbm.at[pl.ds(src, 1), :], buf.at[sl, pl.ds(r, 1), :], sem.at[sl])

        _start_row_copies(pos_ref, tile, FIN_TM, make)

    @pl.when(i == 0)
    def _():
        start_gather(0, 0)

    @pl.when(i + 1 < n)
    def _():
        start_gather(i + 1, 1 - slot)

    for buf in (abuf, bbuf):
        pltpu.make_async_copy(ys_hbm.at[pl.ds(0, FIN_TM), :], buf.at[slot], sem.at[slot]).wait()

    y = abuf[slot] * rt_ref[:, RT_W:RT_W + 1] + bbuf[slot] * rt_ref[:, RT_W + 1:RT_W + 2]
    gate2 = mod_ref[0, 5:6, :]
    o_ref[...] = _layer_norm(ALPHA * h_ref[...] + gate2 * y, lng_ref[...], lnb_ref[...])


def _ffn_residual(h1, route, pos, ys, mod, ln_g, ln_b):
    rows = h1.shape[0]
    grid_spec = pltpu.PrefetchScalarGridSpec(
        num_scalar_prefetch=1,
        grid=(rows // FIN_TM,),
        in_specs=[
            pl.BlockSpec((FIN_TM, D), lambda i, *_: (i, 0)),
            pl.BlockSpec((FIN_TM, LANE), lambda i, *_: (i, 0)),
            pl.BlockSpec((1, 6, D), lambda i, *_: (_mod_row(FIN_TM)(i), 0, 0)),
            pl.BlockSpec((1, D), lambda i, *_: (0, 0)),
            pl.BlockSpec((1, D), lambda i, *_: (0, 0)),
            pl.BlockSpec(memory_space=pl.ANY),
        ],
        out_specs=pl.BlockSpec((FIN_TM, D), lambda i, *_: (i, 0)),
        scratch_shapes=[
            pltpu.VMEM((2, FIN_TM, D), F32),
            pltpu.VMEM((2, FIN_TM, D), F32),
            pltpu.SemaphoreType.DMA((2,)),
        ],
    )
    return pl.pallas_call(
        _final_kernel,
        out_shape=jax.ShapeDtypeStruct((rows, D), F32),
        grid_spec=grid_spec,
        compiler_params=_cparams(1),
        name="ffn_residual_ln",
    )(pos, h1, route, mod, ln_g, ln_b, ys)


def _moe_ffn(h1, f, route, counts, mod, w1, w3, w2, ln_g, ln_b):
    tile_expert, tile_valid, pos, last_tile, has_tile, n_tiles = _plan(route, counts)
    xs = _dispatch(f, pos, last_tile, has_tile, n_tiles * MOE_TM)
    ys = _moe_experts(xs, tile_expert, tile_valid, w1, w3, w2)
    return _ffn_residual(h1, route, pos, ys, mod, ln_g, ln_b)


def _router_weights(w_group, b_group, w_expert, b_expert):
    pad = LANE - N_GROUPS - N_EXPERTS
    w = jnp.concatenate([w_group, w_expert, jnp.zeros((D, pad), F32)], axis=1)
    b = jnp.concatenate([b_group, b_expert, jnp.zeros((pad,), F32)])[None, :]
    return w, b


def kernel(x, c, ctx, c_ctx, ada_w, ada_b, ln_g, ln_b, gla_w_in, gla_w_gate, gla_b_gate, gla_norm_g, gla_w_out,
           nat_w_in, nat_rpb, nat_w_out, moe_w_group, moe_b_group, moe_w_expert, moe_b_expert,
           moe_w1, moe_w3, moe_w2):
    h = jnp.concatenate([x.reshape(R_LAT, D), ctx.reshape(R_CTX, D)], axis=0)
    cc = jnp.concatenate([c, c_ctx[None, :], jnp.zeros((SUBLANE - B - 1, D), F32)], axis=0)
    mod = _ada_table(cc, ada_w, ada_b).reshape(DEPTH, SUBLANE, 6, D)

    w_in = gla_w_in[0]
    w_main = jnp.concatenate([w_in[:, :GLA_QK] * (GLA_DK ** -0.5), w_in[:, GLA_QK:GLA_MAIN]], axis=1).astype(BF16)
    w_glr = jnp.concatenate([w_in[:, GLA_MAIN:], jnp.zeros((D, LANE - 2 * GLA_RANK), F32)], axis=1).astype(BF16)
    p, glr = _proj(h, mod[0], w_main, w_glr)
    wg = jnp.zeros((2, LANE, GLA_QK), F32)
    wg = wg.at[0, :GLA_RANK].set(gla_w_gate[0, 0]).at[1, GLA_RANK:2 * GLA_RANK].set(gla_w_gate[0, 1])
    bg = gla_b_gate[0][:, None, :]
    cos, sin = _rope_tables()
    o_f = _gla_scan(p, glr, wg, bg, cos, sin, reverse=False)
    o_b = _gla_scan(p, glr, wg, bg, cos, sin, reverse=True)
    w_r, b_r = _router_weights(moe_w_group[0], moe_b_group[0], moe_w_expert[0], moe_b_expert[0])
    h1, f, route, counts = _mix_out((o_f, o_b, p, gla_norm_g[0][None, :]), h, mod[0], gla_w_out[0].astype(BF16),
                                    ln_g[0, 0][None, :], ln_b[0, 0][None, :], w_r, b_r, R_ALL, gla=True)
    h = _moe_ffn(h1, f, route, counts, mod[0], moe_w1[0].astype(BF16), moe_w3[0].astype(BF16),
                 moe_w2[0].astype(BF16), ln_g[0, 1][None, :], ln_b[0, 1][None, :])

    w_in = nat_w_in[0]
    w_nat = jnp.concatenate([w_in[:, :D] * (NAT_DH ** -0.5), w_in[:, D:]], axis=1).astype(BF16)
    p = _proj(h, mod[1], w_nat)
    o = _nat_attention(p, _nat_bias_table(nat_rpb[0]))
    w_r, b_r = _router_weights(moe_w_group[1], moe_b_group[1], moe_w_expert[1], moe_b_expert[1])
    h1, f, route, counts = _mix_out((o,), h, mod[1], nat_w_out[0].astype(BF16),
                                    ln_g[1, 0][None, :], ln_b[1, 0][None, :], w_r, b_r, R_LAT, gla=False)
    out = _moe_ffn(h1, f, route, counts, mod[1], moe_w1[1].astype(BF16), moe_w3[1].astype(BF16),
                   moe_w2[1].astype(BF16), ln_g[1, 1][None, :], ln_b[1, 1][None, :])
    return out.reshape(B, SEQ, D)
```

```python
import functools

import jax
import jax.numpy as jnp
import numpy as np
from jax import lax
from jax.experimental import pallas as pl
from jax.experimental.pallas import tpu as pltpu

F32 = jnp.float32
BF16 = jnp.bfloat16

D = 2048
B = 4
SEQ = 4096
DEPTH = 2
GRID_W = 64
CTX = 256
R_LAT = B * SEQ
R_CTX = B * CTX
R_ALL = R_LAT + R_CTX

GLA_H = 4
GLA_DK = 256
GLA_DV = 512
GLA_CHUNK = 64
GLA_RANK = 16
GLA_QK = GLA_H * GLA_DK
GLA_VD = GLA_H * GLA_DV
GLA_MAIN = 2 * GLA_QK + 2 * GLA_VD
GLA_GATE_NORM = 16.0
ROPE_BASE = 10000.0

NAT_H = 16
NAT_DH = 128
WIN_R = 8
WIN_C = 16
NAT_QR = 16

N_GROUPS = 4
EPG = 4
N_EXPERTS = 16
D_EXPERT = 1024
MOE_TM = 256

ALPHA = (2 * DEPTH) ** 0.25
LN_EPS = 1e-5
NEG_INF = -1e30

LANE = 128
SUBLANE = 8
VMEM_LIMIT = 56 * 1024 * 1024


def _cparams(n_axes):
    return pltpu.CompilerParams(dimension_semantics=("arbitrary",) * n_axes,
                                vmem_limit_bytes=VMEM_LIMIT)


def _split_bf16(x):
    hi = x.astype(BF16)
    lo = (x - hi.astype(F32)).astype(BF16)
    return hi, lo


def _dot(a, b):
    return jnp.dot(a, b, preferred_element_type=F32)


def _dot_hi(a, b):
    ah, al = _split_bf16(a)
    bh, bl = _split_bf16(b)
    return _dot(ah, bh) + _dot(al, bh) + _dot(ah, bl)


def _dot_nt(a, b):
    return lax.dot_general(a, b, (((1,), (1,)), ((), ())), preferred_element_type=F32)


def _dot_tn(a, b):
    return lax.dot_general(a, b, (((0,), (0,)), ((), ())), preferred_element_type=F32)


def _mod_row(tile_rows):
    return lambda i: jnp.minimum((i * tile_rows) // SEQ, B)


ADA_TN = 1024


def _ada_kernel(c_ref, w_ref, b_ref, o_ref):
    x = c_ref[...]
    s = x * jax.nn.sigmoid(x)
    o_ref[0] = _dot_hi(s, w_ref[0]) + b_ref[0]


def _ada_table(cc, ada_w, ada_b):
    n = 6 * D
    return pl.pallas_call(
        _ada_kernel,
        out_shape=jax.ShapeDtypeStruct((DEPTH, SUBLANE, n), F32),
        grid=(DEPTH, n // ADA_TN),
        in_specs=[
            pl.BlockSpec((SUBLANE, D), lambda l, j: (0, 0)),
            pl.BlockSpec((1, D, ADA_TN), lambda l, j: (l, 0, j)),
            pl.BlockSpec((1, 1, ADA_TN), lambda l, j: (l, 0, j)),
        ],
        out_specs=pl.BlockSpec((1, SUBLANE, ADA_TN), lambda l, j: (l, 0, j)),
        compiler_params=_cparams(2),
        name="ada_table",
    )(cc, ada_w, ada_b.reshape(DEPTH, 1, n))


PROJ_TM = 1024
PROJ_TN = 1024


def _proj_kernel(h_ref, mod_ref, w_ref, *rest, with_extra):
    if with_extra:
        wx_ref, o_ref, ox_ref, a_scr = rest
    else:
        o_ref, a_scr = rest
    j = pl.program_id(1)

    @pl.when(j == 0)
    def _():
        sh = mod_ref[0, 0:1, :]
        sc = mod_ref[0, 1:2, :]
        a_scr[...] = (h_ref[...] * (1.0 + sc) + sh).astype(BF16)
        if with_extra:
            ox_ref[...] = _dot(a_scr[...], wx_ref[...])

    o_ref[...] = _dot(a_scr[...], w_ref[...]).astype(o_ref.dtype)


def _proj(h, mod, w, w_extra=None):
    rows = h.shape[0]
    n = w.shape[1]
    with_extra = w_extra is not None
    in_specs = [
        pl.BlockSpec((PROJ_TM, D), lambda i, j: (i, 0)),
        pl.BlockSpec((1, 6, D), lambda i, j: (_mod_row(PROJ_TM)(i), 0, 0)),
        pl.BlockSpec((D, PROJ_TN), lambda i, j: (0, j)),
    ]
    out_shape = [jax.ShapeDtypeStruct((rows, n), BF16)]
    out_specs = [pl.BlockSpec((PROJ_TM, PROJ_TN), lambda i, j: (i, j))]
    args = [h, mod, w]
    if with_extra:
        in_specs.append(pl.BlockSpec((D, LANE), lambda i, j: (0, 0)))
        out_shape.append(jax.ShapeDtypeStruct((rows, LANE), F32))
        out_specs.append(pl.BlockSpec((PROJ_TM, LANE), lambda i, j: (i, 0)))
        args.append(w_extra)
    out = pl.pallas_call(
        functools.partial(_proj_kernel, with_extra=with_extra),
        out_shape=out_shape,
        grid=(rows // PROJ_TM, n // PROJ_TN),
        in_specs=in_specs,
        out_specs=out_specs,
        scratch_shapes=[pltpu.VMEM((PROJ_TM, D), BF16)],
        compiler_params=_cparams(2),
        name="mod_proj",
    )(*args)
    return out if with_extra else out[0]


GLA_TB = 256
GLA_STEPS = 1 + SEQ // GLA_TB


def _rope_tables():
    half = GLA_DK // 4
    freqs = ROPE_BASE ** (-np.arange(half, dtype=np.float64) / half)
    t = np.arange(SEQ)
    ang_r = (t // GRID_W)[:, None] * freqs
    ang_c = (t % GRID_W)[:, None] * freqs
    cos = np.concatenate([np.cos(ang_r)] * 2 + [np.cos(ang_c)] * 2, axis=1)
    sin = np.concatenate([-np.sin(ang_r), np.sin(ang_r), -np.sin(ang_c), np.sin(ang_c)], axis=1)
    cos = np.concatenate([cos, np.ones((GLA_TB, GLA_DK))], axis=0)
    sin = np.concatenate([sin, np.zeros((GLA_TB, GLA_DK))], axis=0)
    return jnp.asarray(cos, F32), jnp.asarray(sin, F32)


def _rope(x, cos, sin):
    half = LANE // 2
    parts = [pltpu.roll(x[:, g * LANE:(g + 1) * LANE], half, 1) for g in range(x.shape[1] // LANE)]
    return x * cos + jnp.concatenate(parts, axis=1) * sin


def _gla_kernel(q_ref, k_ref, v_ref, glr_ref, wg_ref, bg_ref, cos_ref, sin_ref, o_ref, s_scr, *, reverse):
    @pl.when(pl.program_id(1) == 0)
    def _():
        s_scr[...] = jnp.zeros_like(s_scr)

    row = lax.broadcasted_iota(jnp.int32, (GLA_CHUNK, GLA_CHUNK), 0)
    col = lax.broadcasted_iota(jnp.int32, (GLA_CHUNK, GLA_CHUNK), 1)
    keep = (col >= row) if reverse else (col <= row)
    tri = jnp.where(keep, 1.0, 0.0).astype(BF16)
    ones = jnp.ones((GLA_CHUNK, LANE), BF16)
    n_chunks = GLA_TB // GLA_CHUNK
    order = range(n_chunks - 1, -1, -1) if reverse else range(n_chunks)
    for ci in order:
        sl = slice(ci * GLA_CHUNK, (ci + 1) * GLA_CHUNK)
        cos = cos_ref[sl, :]
        sin = sin_ref[sl, :]
        z = _dot_hi(glr_ref[sl, :], wg_ref[0]) + bg_ref[0]
        g = (jnp.minimum(z, 0.0) - jnp.log1p(jnp.exp(-jnp.abs(z)))) * (1.0 / GLA_GATE_NORM)
        g_hi, g_lo = _split_bf16(g)
        cum_all = _dot(tri, g_hi) + _dot(tri, g_lo)
        decay_all = jnp.exp(_dot_tn(g_hi, ones) + _dot_tn(g_lo, ones))
        for hd in range(GLA_H):
            ks = slice(hd * GLA_DK, (hd + 1) * GLA_DK)
            vs = slice(hd * GLA_DV, (hd + 1) * GLA_DV)
            q = _rope(q_ref[sl, ks].astype(F32), cos, sin)
            k = _rope(k_ref[sl, ks].astype(F32), cos, sin)
            v = v_ref[sl, vs]
            cum = cum_all[:, ks]
            tot = cum[0:1, :] if reverse else cum[GLA_CHUNK - 1:GLA_CHUNK, :]
            q_dec = (q * jnp.exp(cum)).astype(BF16)
            k_inv = (k * jnp.exp(-cum)).astype(BF16)
            k_end = (k * jnp.exp(tot - cum)).astype(BF16)
            att = jnp.where(keep, _dot_nt(q_dec, k_inv), 0.0).astype(BF16)
            state = s_scr[hd]
            o_ref[sl, vs] = (_dot(att, v) + _dot(q_dec, state.astype(BF16))).astype(o_ref.dtype)
            decay = jnp.concatenate([decay_all[ks, :]] * (GLA_DV // LANE), axis=1)
            s_scr[hd] = state * decay + _dot_tn(k_end, v)


def _gla_scan(p, glr, wg, bg, cos, sin, reverse):
    lat_blocks = SEQ // GLA_TB

    def tblock(s):
        if reverse:
            return jnp.where(s == 0, 0, GLA_STEPS - s)
        return s

    def rows(b, s):
        tb = tblock(s)
        return jnp.where(tb == 0, R_LAT // GLA_TB + b, b * lat_blocks + tb - 1)

    def rope_rows(s):
        tb = tblock(s)
        return jnp.where(tb == 0, lat_blocks, tb - 1)

    d = 1 if reverse else 0
    return pl.pallas_call(
        functools.partial(_gla_kernel, reverse=reverse),
        out_shape=jax.ShapeDtypeStruct((R_ALL, GLA_VD), BF16),
        grid=(B, GLA_STEPS),
        in_specs=[
            pl.BlockSpec((GLA_TB, GLA_QK), lambda b, s: (rows(b, s), 0)),
            pl.BlockSpec((GLA_TB, GLA_QK), lambda b, s: (rows(b, s), 1)),
            pl.BlockSpec((GLA_TB, GLA_VD), lambda b, s: (rows(b, s), 2 * GLA_QK // GLA_VD)),
            pl.BlockSpec((GLA_TB, LANE), lambda b, s: (rows(b, s), 0)),
            pl.BlockSpec((1, LANE, GLA_QK), lambda b, s: (d, 0, 0)),
            pl.BlockSpec((1, 1, GLA_QK), lambda b, s: (d, 0, 0)),
            pl.BlockSpec((GLA_TB, GLA_DK), lambda b, s: (rope_rows(s), 0)),
            pl.BlockSpec((GLA_TB, GLA_DK), lambda b, s: (rope_rows(s), 0)),
        ],
        out_specs=pl.BlockSpec((GLA_TB, GLA_VD), lambda b, s: (rows(b, s), 0)),
        scratch_shapes=[pltpu.VMEM((GLA_H, GLA_DK, GLA_DV), F32)],
        compiler_params=_cparams(2),
        name="gla_scan_bwd" if reverse else "gla_scan_fwd",
    )(p, p, p, glr, wg, bg, cos, sin)


def _nat_bias_table(rpb):
    qc = np.arange(GRID_W)[:, None]
    kc = np.arange(GRID_W)[None, :]
    cs = np.clip(qc - WIN_C // 2, 0, GRID_W - WIN_C)
    col_ok = (kc >= cs) & (kc < cs + WIN_C)
    dc = np.clip(kc - qc + WIN_C - 1, 0, 2 * WIN_C - 2)
    pick = (dc[:, :, None] == np.arange(2 * WIN_C - 1)).astype(np.float32)
    tc = jnp.einsum('had,qkd->haqk', rpb.astype(F32), jnp.asarray(pick), precision=lax.Precision.HIGHEST)
    tc = jnp.where(col_ok[None, None], tc, NEG_INF)
    tab = jnp.stack([tc[:, a0:a0 + WIN_R] for a0 in range(WIN_R)], axis=1)
    return tab.transpose(0, 1, 3, 2, 4).reshape(NAT_H, WIN_R, GRID_W, WIN_R * GRID_W)


def _nat_kernel(q_ref, k_ref, v_ref, kc_ref, vc_ref, bias_ref, o_ref):
    rb = pl.program_id(2)
    rows_n = SEQ // GRID_W
    n_keys = WIN_R * GRID_W
    q = q_ref[...]
    s_ctx = _dot_nt(q, kc_ref[...])
    m_ctx = jnp.max(s_ctx, axis=1, keepdims=True)
    o_loc, p_ctx, denom = [], [], []
    for i in range(NAT_QR):
        qr = rb * NAT_QR + i
        rs = jnp.clip(qr - WIN_R // 2, 0, rows_n - WIN_R)
        a0 = rs - qr + WIN_R - 1
        start = pl.multiple_of(rs * GRID_W, GRID_W)
        sl = slice(i * GRID_W, (i + 1) * GRID_W)
        s = _dot_nt(q[sl], k_ref[pl.ds(start, n_keys), :]) + bias_ref[0, a0]
        m = jnp.maximum(jnp.max(s, axis=1, keepdims=True), m_ctx[sl])
        p = jnp.exp(s - m)
        pc = jnp.exp(s_ctx[sl] - m)
        denom.append(jnp.sum(p, axis=1, keepdims=True) + jnp.sum(pc, axis=1, keepdims=True))
        p_ctx.append(pc.astype(BF16))
        o_loc.append(_dot(p.astype(BF16), v_ref[pl.ds(start, n_keys), :]))
    o = jnp.concatenate(o_loc, axis=0) + _dot(jnp.concatenate(p_ctx, axis=0), vc_ref[...])
    o_ref[...] = (o / jnp.concatenate(denom, axis=0)).astype(o_ref.dtype)


def _nat_attention(p, bias_tab):
    n_rb = SEQ // (NAT_QR * GRID_W)
    tq = NAT_QR * GRID_W
    hq = D // NAT_DH
    return pl.pallas_call(
        _nat_kernel,
        out_shape=jax.ShapeDtypeStruct((R_LAT, D), BF16),
        grid=(B, NAT_H, n_rb),
        in_specs=[
            pl.BlockSpec((tq, NAT_DH), lambda b, h, rb: (b * n_rb + rb, h)),
            pl.BlockSpec((SEQ, NAT_DH), lambda b, h, rb: (b, hq + h)),
            pl.BlockSpec((SEQ, NAT_DH), lambda b, h, rb: (b, 2 * hq + h)),
            pl.BlockSpec((CTX, NAT_DH), lambda b, h, rb: (R_LAT // CTX + b, hq + h)),
            pl.BlockSpec((CTX, NAT_DH), lambda b, h, rb: (R_LAT // CTX + b, 2 * hq + h)),
            pl.BlockSpec((1, WIN_R, GRID_W, WIN_R * GRID_W), lambda b, h, rb: (h, 0, 0, 0)),
        ],
        out_specs=pl.BlockSpec((tq, NAT_DH), lambda b, h, rb: (b * n_rb + rb, h)),
        compiler_params=_cparams(3),
        name="nat_attention",
    )(p, p, p, p, p, bias_tab)


MIX_TM = 256
MIX_PARTS = 2


def _layer_norm(u, g, b):
    mu = jnp.mean(u, axis=-1, keepdims=True)
    var = jnp.mean(jnp.square(u - mu), axis=-1, keepdims=True)
    return (u - mu) * lax.rsqrt(var + LN_EPS) * g + b


def _mix_out_kernel(*refs, gla):
    if gla:
        of_ref, ob_ref, r_ref, ng_ref = refs[:4]
        refs = refs[4:]
    else:
        x_ref = refs[0]
        refs = refs[1:]
    h_ref, mod_ref, w_ref, lng_ref, lnb_ref, wr_ref, br_ref, h1_ref, f_ref, rt_ref, cnt_ref, cnt_scr = refs

    @pl.when(pl.program_id(0) == 0)
    def _():
        cnt_scr[...] = jnp.zeros_like(cnt_scr)

    logits = []
    for part in range(MIX_PARTS):
        rs = slice(part * (MIX_TM // MIX_PARTS), (part + 1) * (MIX_TM // MIX_PARTS))
        if gla:
            parts = []
            for hd in range(GLA_H):
                cs = slice(hd * GLA_DV, (hd + 1) * GLA_DV)
                o = of_ref[rs, cs].astype(F32) + ob_ref[rs, cs].astype(F32)
                o = o * lax.rsqrt(jnp.mean(jnp.square(o), axis=-1, keepdims=True) + LN_EPS) * ng_ref[...]
                r = r_ref[rs, cs].astype(F32)
                parts.append((o * (r * jax.nn.sigmoid(r))).astype(BF16))
            x = jnp.concatenate(parts, axis=1)
        else:
            x = x_ref[rs, :]
        y = _dot(x, w_ref[...])
        gate1 = mod_ref[0, 2:3, :]
        h1 = _layer_norm(ALPHA * h_ref[rs, :] + gate1 * y, lng_ref[...], lnb_ref[...])
        h1_ref[rs, :] = h1
        f = h1 * (1.0 + mod_ref[0, 4:5, :]) + mod_ref[0, 3:4, :]
        f_ref[rs, :] = f
        logits.append(_dot_hi(f, wr_ref[...]) + br_ref[...])
    route, counts = _route_tile(jnp.concatenate(logits, axis=0), cnt_scr[...])
    rt_ref[...] = route
    cnt_scr[...] = counts
    cnt_ref[...] = jnp.broadcast_to(counts, cnt_ref.shape)


RT_E, RT_W, RT_RANK = 0, 2, 4


def _route_tile(logits, run):
    tm = logits.shape[0]
    lane = lax.broadcasted_iota(jnp.int32, (tm, LANE), 1).astype(F32)

    def first_max(vals):
        top = jnp.max(vals, axis=1, keepdims=True)
        return top, jnp.min(jnp.where(vals == top, lane, float(LANE)), axis=1, keepdims=True)

    gl = jnp.where(lane < N_GROUPS, logits, NEG_INF)
    g_max, g_idx = first_max(gl)
    g_p = 1.0 / jnp.sum(jnp.exp(gl - g_max), axis=1, keepdims=True)
    lo = N_GROUPS + EPG * g_idx
    el = jnp.where((lane >= lo) & (lane < lo + EPG), logits, NEG_INF)
    e1, i1 = first_max(el)
    e2, i2 = first_max(jnp.where(lane == i1, NEG_INF, el))
    t = jnp.exp(e2 - e1)
    p1 = 1.0 / (1.0 + t)
    x1 = i1 - N_GROUPS
    x2 = i2 - N_GROUPS
    onehot = jnp.where((lane == x1) | (lane == x2), 1.0, 0.0)
    row = lax.broadcasted_iota(jnp.int32, (tm, tm), 0)
    col = lax.broadcasted_iota(jnp.int32, (tm, tm), 1)
    before = jnp.where(col < row, 1.0, 0.0).astype(BF16)
    seen = _dot(before, onehot.astype(BF16)) + run
    rank1 = jnp.sum(jnp.where(lane == x1, seen, 0.0), axis=1, keepdims=True)
    rank2 = jnp.sum(jnp.where(lane == x2, seen, 0.0), axis=1, keepdims=True)
    route = jnp.zeros((tm, LANE), F32)
    for k, val in ((RT_E, x1), (RT_E + 1, x2), (RT_W, g_p * p1), (RT_W + 1, g_p * p1 * t),
                   (RT_RANK, rank1), (RT_RANK + 1, rank2)):
        route = jnp.where(lane == k, val, route)
    return route, run + jnp.sum(onehot, axis=0, keepdims=True)


def _mix_out(mix_inputs, h, mod, w_out, ln_g, ln_b, w_router, b_router, rows, gla):
    row_spec = lambda width: pl.BlockSpec((MIX_TM, width), lambda i: (i, 0))
    full = lambda a: pl.BlockSpec(a.shape, lambda i: (0,) * a.ndim)
    if gla:
        o_f, o_b, p, norm_g = mix_inputs
        in_specs = [row_spec(GLA_VD), row_spec(GLA_VD),
                    pl.BlockSpec((MIX_TM, GLA_VD), lambda i: (i, (2 * GLA_QK + GLA_VD) // GLA_VD)),
                    full(norm_g)]
        args = [o_f, o_b, p, norm_g]
    else:
        (x,) = mix_inputs
        in_specs = [row_spec(D)]
        args = [x]
    in_specs += [row_spec(D), pl.BlockSpec((1, 6, D), lambda i: (_mod_row(MIX_TM)(i), 0, 0)),
                 full(w_out), full(ln_g), full(ln_b), full(w_router), full(b_router)]
    args += [h, mod, w_out, ln_g, ln_b, w_router, b_router]
    return pl.pallas_call(
        functools.partial(_mix_out_kernel, gla=gla),
        out_shape=[jax.ShapeDtypeStruct((rows, D), F32), jax.ShapeDtypeStruct((rows, D), F32),
                   jax.ShapeDtypeStruct((rows, LANE), F32), jax.ShapeDtypeStruct((SUBLANE, LANE), F32)],
        grid=(rows // MIX_TM,),
        in_specs=in_specs,
        out_specs=[row_spec(D), row_spec(D), row_spec(LANE), pl.BlockSpec((SUBLANE, LANE), lambda i: (0, 0))],
        scratch_shapes=[pltpu.VMEM((1, LANE), F32)],
        compiler_params=_cparams(1),
        name="gla_out_ln" if gla else "nat_out_ln",
    )(*args)


def _plan(route, counts_row):
    t = route.shape[0]
    e = route[:, RT_E:RT_E + 2].astype(jnp.int32)
    rank = route[:, RT_RANK:RT_RANK + 2].astype(jnp.int32)
    counts = counts_row[0, :N_EXPERTS].astype(jnp.int32)
    tiles_per = (counts + MOE_TM - 1) // MOE_TM
    tile_end = jnp.cumsum(tiles_per)
    tile_start = tile_end - tiles_per
    n_tiles = 2 * t // MOE_TM + N_EXPERTS
    ids = jnp.arange(n_tiles, dtype=jnp.int32)
    used = tile_end[-1]
    tile_valid = (ids < used).astype(jnp.int32)
    lookup = jnp.minimum(ids, used - 1)
    tile_expert = jnp.sum((tile_end[None, :] <= lookup[:, None]).astype(jnp.int32), axis=1)
    tile_expert = jnp.minimum(tile_expert, N_EXPERTS - 1).astype(jnp.int32)
    base = tile_start * MOE_TM
    experts = jnp.arange(N_EXPERTS, dtype=jnp.int32)
    pos = jnp.sum(jnp.where(e[..., None] == experts, base, 0), axis=-1) + rank
    trailing = n_tiles - 1 - experts
    fill_tile = jnp.concatenate([jnp.maximum(tile_end - 1, 0), trailing]).astype(jnp.int32)
    fill_on = jnp.concatenate([tiles_per > 0, trailing >= used]).astype(jnp.int32)
    return tile_expert, tile_valid, pos.reshape(-1).astype(jnp.int32), fill_tile, fill_on, n_tiles


def _start_row_copies(pos_ref, tile, rows, make):
    def body(r, carry):
        a = 2 * (tile * rows + r)
        make(r, pos_ref[a], False).start()
        make(r, pos_ref[a + 1], True).start()
        return carry

    lax.fori_loop(0, rows, body, 0, unroll=8)


DISP_TM = 256


def _dispatch_kernel(pos_ref, lt_ref, ht_ref, f_ref, xs_hbm, sbuf, zbuf, sem, zsem):
    i = pl.program_id(0)
    n = pl.num_programs(0)
    slot = i % 2

    def fill(e):
        return pltpu.make_async_copy(zbuf, xs_hbm.at[pl.ds(lt_ref[e] * MOE_TM, MOE_TM), :], zsem)

    @pl.when(i == 0)
    def _():
        zbuf[...] = jnp.zeros_like(zbuf)
        for e in range(2 * N_EXPERTS):
            @pl.when(ht_ref[e] != 0)
            def _():
                fill(e).start()
        for e in range(2 * N_EXPERTS):
            @pl.when(ht_ref[e] != 0)
            def _():
                fill(e).wait()

    def wait_scatter(sl):
        for _ in range(2):
            pltpu.make_async_copy(sbuf.at[sl], xs_hbm.at[pl.ds(0, DISP_TM), :], sem.at[sl]).wait()

    @pl.when(i >= 2)
    def _():
        wait_scatter(slot)

    sbuf[slot] = f_ref[...]

    def make(r, dst, second):
        return pltpu.make_async_copy(sbuf.at[slot, pl.ds(r, 1), :], xs_hbm.at[pl.ds(dst, 1), :], sem.at[slot])

    _start_row_copies(pos_ref, i, DISP_TM, make)

    @pl.when(i == n - 1)
    def _():
        wait_scatter(slot)

        @pl.when(n >= 2)
        def _():
            wait_scatter(1 - slot)


def _dispatch(f, pos, last_tile, has_tile, cap):
    t = f.shape[0]
    grid_spec = pltpu.PrefetchScalarGridSpec(
        num_scalar_prefetch=3,
        grid=(t // DISP_TM,),
        in_specs=[pl.BlockSpec((DISP_TM, D), lambda i, *_: (i, 0))],
        out_specs=pl.BlockSpec(memory_space=pl.ANY),
        scratch_shapes=[
            pltpu.VMEM((2, DISP_TM, D), F32),
            pltpu.VMEM((MOE_TM, D), F32),
            pltpu.SemaphoreType.DMA((2,)),
            pltpu.SemaphoreType.DMA,
        ],
    )
    return pl.pallas_call(
        _dispatch_kernel,
        out_shape=jax.ShapeDtypeStruct((cap, D), F32),
        grid_spec=grid_spec,
        compiler_params=_cparams(1),
        name="moe_dispatch",
    )(pos, last_tile, has_tile, f)


def _moe_kernel(te_ref, tv_ref, x_ref, w1_ref, w3_ref, w2_ref, y_ref):
    i = pl.program_id(0)

    @pl.when(tv_ref[i] != 0)
    def _():
        x = x_ref[...].astype(BF16)
        h1 = _dot(x, w1_ref[0, 0])
        h3 = _dot(x, w3_ref[0, 0])
        a = (h1 * jax.nn.sigmoid(h1) * h3).astype(BF16)
        y_ref[...] = _dot(a, w2_ref[0, 0])

    @pl.when(tv_ref[i] == 0)
    def _():
        y_ref[...] = jnp.zeros_like(y_ref)


def _moe_experts(xs, tile_expert, tile_valid, layer, w1, w3, w2):
    cap = xs.shape[0]
    grid_spec = pltpu.PrefetchScalarGridSpec(
        num_scalar_prefetch=2,
        grid=(cap // MOE_TM,),
        in_specs=[
            pl.BlockSpec((MOE_TM, D), lambda i, te, tv: (i * tv[i], 0)),
            pl.BlockSpec((1, 1, D, D_EXPERT), lambda i, te, tv: (layer, te[i], 0, 0)),
            pl.BlockSpec((1, 1, D, D_EXPERT), lambda i, te, tv: (layer, te[i], 0, 0)),
            pl.BlockSpec((1, 1, D_EXPERT, D), lambda i, te, tv: (layer, te[i], 0, 0)),
        ],
        out_specs=pl.BlockSpec((MOE_TM, D), lambda i, te, tv: (i, 0)),
    )
    return pl.pallas_call(
        _moe_kernel,
        out_shape=jax.ShapeDtypeStruct((cap, D), F32),
        grid_spec=grid_spec,
        compiler_params=_cparams(1),
        name="moe_experts",
    )(tile_expert, tile_valid, xs, w1, w3, w2)


FIN_TM = 256


def _final_kernel(pos_ref, h_ref, rt_ref, mod_ref, lng_ref, lnb_ref, ys_hbm, o_ref, abuf, bbuf, sem):
    i = pl.program_id(0)
    n = pl.num_programs(0)
    slot = i % 2

    def start_gather(tile, sl):
        def make(r, src, second):
            buf = bbuf if second else abuf
            return pltpu.make_async_copy(ys_hbm.at[pl.ds(src, 1), :], buf.at[sl, pl.ds(r, 1), :], sem.at[sl])

        _start_row_copies(pos_ref, tile, FIN_TM, make)

    @pl.when(i == 0)
    def _():
        start_gather(0, 0)

    @pl.when(i + 1 < n)
    def _():
        start_gather(i + 1, 1 - slot)

    for buf in (abuf, bbuf):
        pltpu.make_async_copy(ys_hbm.at[pl.ds(0, FIN_TM), :], buf.at[slot], sem.at[slot]).wait()

    y = abuf[slot] * rt_ref[:, RT_W:RT_W + 1] + bbuf[slot] * rt_ref[:, RT_W + 1:RT_W + 2]
    gate2 = mod_ref[0, 5:6, :]
    o_ref[...] = _layer_norm(ALPHA * h_ref[...] + gate2 * y, lng_ref[...], lnb_ref[...])


def _ffn_residual(h1, route, pos, ys, mod, ln_g, ln_b):
    rows = h1.shape[0]
    grid_spec = pltpu.PrefetchScalarGridSpec(
        num_scalar_prefetch=1,
        grid=(rows // FIN_TM,),
        in_specs=[
            pl.BlockSpec((FIN_TM, D), lambda i, *_: (i, 0)),
            pl.BlockSpec((FIN_TM, LANE), lambda i, *_: (i, 0)),
            pl.BlockSpec((1, 6, D), lambda i, *_: (_mod_row(FIN_TM)(i), 0, 0)),
            pl.BlockSpec((1, D), lambda i, *_: (0, 0)),
            pl.BlockSpec((1, D), lambda i, *_: (0, 0)),
            pl.BlockSpec(memory_space=pl.ANY),
        ],
        out_specs=pl.BlockSpec((FIN_TM, D), lambda i, *_: (i, 0)),
        scratch_shapes=[
            pltpu.VMEM((2, FIN_TM, D), F32),
            pltpu.VMEM((2, FIN_TM, D), F32),
            pltpu.SemaphoreType.DMA((2,)),
        ],
    )
    return pl.pallas_call(
        _final_kernel,
        out_shape=jax.ShapeDtypeStruct((rows, D), F32),
        grid_spec=grid_spec,
        compiler_params=_cparams(1),
        name="ffn_residual_ln",
    )(pos, h1, route, mod, ln_g, ln_b, ys)


def _moe_ffn(h1, f, route, counts, mod, layer, w1, w3, w2, ln_g, ln_b):
    tile_expert, tile_valid, pos, fill_tile, fill_on, n_tiles = _plan(route, counts)
    xs = _dispatch(f, pos, fill_tile, fill_on, n_tiles * MOE_TM)
    ys = _moe_experts(xs, tile_expert, tile_valid, layer, w1, w3, w2)
    return _ffn_residual(h1, route, pos, ys, mod, ln_g, ln_b)


def _router_weights(w_group, b_group, w_expert, b_expert):
    pad = LANE - N_GROUPS - N_EXPERTS
    w = jnp.concatenate([w_group, w_expert, jnp.zeros((D, pad), F32)], axis=1)
    b = jnp.concatenate([b_group, b_expert, jnp.zeros((pad,), F32)])[None, :]
    return w, b


def kernel(x, c, ctx, c_ctx, ada_w, ada_b, ln_g, ln_b, gla_w_in, gla_w_gate, gla_b_gate, gla_norm_g, gla_w_out,
           nat_w_in, nat_rpb, nat_w_out, moe_w_group, moe_b_group, moe_w_expert, moe_b_expert,
           moe_w1, moe_w3, moe_w2):
    h = jnp.concatenate([x.reshape(R_LAT, D), ctx.reshape(R_CTX, D)], axis=0)
    cc = jnp.concatenate([c, c_ctx[None, :], jnp.zeros((SUBLANE - B - 1, D), F32)], axis=0)
    mod = _ada_table(cc, ada_w, ada_b).reshape(DEPTH, SUBLANE, 6, D)
    w1, w3, w2 = moe_w1.astype(BF16), moe_w3.astype(BF16), moe_w2.astype(BF16)

    w_in = gla_w_in[0]
    w_main = jnp.concatenate([w_in[:, :GLA_QK] * (GLA_DK ** -0.5), w_in[:, GLA_QK:GLA_MAIN]], axis=1).astype(BF16)
    w_glr = jnp.concatenate([w_in[:, GLA_MAIN:], jnp.zeros((D, LANE - 2 * GLA_RANK), F32)], axis=1).astype(BF16)
    p, glr = _proj(h, mod[0], w_main, w_glr)
    wg = jnp.zeros((2, LANE, GLA_QK), F32)
    wg = wg.at[0, :GLA_RANK].set(gla_w_gate[0, 0]).at[1, GLA_RANK:2 * GLA_RANK].set(gla_w_gate[0, 1])
    bg = gla_b_gate[0][:, None, :]
    cos, sin = _rope_tables()
    o_f = _gla_scan(p, glr, wg, bg, cos, sin, reverse=False)
    o_b = _gla_scan(p, glr, wg, bg, cos, sin, reverse=True)
    w_r, b_r = _router_weights(moe_w_group[0], moe_b_group[0], moe_w_expert[0], moe_b_expert[0])
    h1, f, route, counts = _mix_out((o_f, o_b, p, gla_norm_g[0][None, :]), h, mod[0], gla_w_out[0].astype(BF16),
                                    ln_g[0, 0][None, :], ln_b[0, 0][None, :], w_r, b_r, R_ALL, gla=True)
    h = _moe_ffn(h1, f, route, counts, mod[0], 0, w1, w3, w2, ln_g[0, 1][None, :], ln_b[0, 1][None, :])

    w_in = nat_w_in[0]
    w_nat = jnp.concatenate([w_in[:, :D] * (NAT_DH ** -0.5), w_in[:, D:]], axis=1).astype(BF16)
    p = _proj(h, mod[1], w_nat)
    o = _nat_attention(p, _nat_bias_table(nat_rpb[0]))
    w_r, b_r = _router_weights(moe_w_group[1], moe_b_group[1], moe_w_expert[1], moe_b_expert[1])
    h1, f, route, counts = _mix_out((o,), h, mod[1], nat_w_out[0].astype(BF16),
                                    ln_g[1, 0][None, :], ln_b[1, 0][None, :], w_r, b_r, R_LAT, gla=False)
    out = _moe_ffn(h1, f, route, counts, mod[1], 1, w1, w3, w2, ln_g[1, 1][None, :], ln_b[1, 1][None, :])
    return out.reshape(B, SEQ, D)
```

```python
import functools

import jax
import jax.numpy as jnp
import numpy as np
from jax import lax
from jax.experimental import pallas as pl
from jax.experimental.pallas import tpu as pltpu

F32 = jnp.float32
BF16 = jnp.bfloat16

D = 2048
B = 4
SEQ = 4096
DEPTH = 2
GRID_W = 64
CTX = 256
R_LAT = B * SEQ
R_CTX = B * CTX
R_ALL = R_LAT + R_CTX

GLA_H = 4
GLA_DK = 256
GLA_DV = 512
GLA_CHUNK = 64
GLA_RANK = 16
GLA_QK = GLA_H * GLA_DK
GLA_VD = GLA_H * GLA_DV
GLA_MAIN = 2 * GLA_QK + 2 * GLA_VD
GLA_GATE_NORM = 16.0
ROPE_BASE = 10000.0

NAT_H = 16
NAT_DH = 128
WIN_R = 8
WIN_C = 16
NAT_QR = 4
NAT_BAND = 12
NAT_G = 4

N_GROUPS = 4
EPG = 4
N_EXPERTS = 16
D_EXPERT = 1024
MOE_TM = 256

ALPHA = (2 * DEPTH) ** 0.25
LN_EPS = 1e-5
NEG_INF = -1e30

LANE = 128
SUBLANE = 8
VMEM_LIMIT = 56 * 1024 * 1024


def _cparams(n_axes):
    return pltpu.CompilerParams(dimension_semantics=("arbitrary",) * n_axes,
                                vmem_limit_bytes=VMEM_LIMIT)


def _split_bf16(x):
    hi = x.astype(BF16)
    lo = (x - hi.astype(F32)).astype(BF16)
    return hi, lo


def _dot(a, b):
    return jnp.dot(a, b, preferred_element_type=F32)


def _dot_hi(a, b):
    ah, al = _split_bf16(a)
    bh, bl = _split_bf16(b)
    return _dot(ah, bh) + _dot(al, bh) + _dot(ah, bl)


def _dot_nt(a, b):
    return lax.dot_general(a, b, (((1,), (1,)), ((), ())), preferred_element_type=F32)


def _dot_tn(a, b):
    return lax.dot_general(a, b, (((0,), (0,)), ((), ())), preferred_element_type=F32)


def _mod_row(tile_rows):
    return lambda i: jnp.minimum((i * tile_rows) // SEQ, B)


def _stream_specs(stream, tile_rows):
    n_lat = R_LAT // tile_rows
    base = stream[2] // tile_rows
    return (pl.BlockSpec((tile_rows, D), lambda i, *_: (jnp.minimum(i, n_lat - 1), 0)),
            pl.BlockSpec((tile_rows, D), lambda i, *_: (base + jnp.maximum(i - n_lat, 0), 0)))


ADA_TN = 1024


def _ada_kernel(c_ref, w_ref, b_ref, o_ref):
    x = c_ref[...]
    s = x * jax.nn.sigmoid(x)
    o_ref[0] = _dot_hi(s, w_ref[0]) + b_ref[0]


def _ada_table(cc, ada_w, ada_b):
    n = 6 * D
    return pl.pallas_call(
        _ada_kernel,
        out_shape=jax.ShapeDtypeStruct((DEPTH, SUBLANE, n), F32),
        grid=(DEPTH, n // ADA_TN),
        in_specs=[
            pl.BlockSpec((SUBLANE, D), lambda l, j: (0, 0)),
            pl.BlockSpec((1, D, ADA_TN), lambda l, j: (l, 0, j)),
            pl.BlockSpec((1, 1, ADA_TN), lambda l, j: (l, 0, j)),
        ],
        out_specs=pl.BlockSpec((1, SUBLANE, ADA_TN), lambda l, j: (l, 0, j)),
        compiler_params=_cparams(2),
        name="ada_table",
    )(cc, ada_w, ada_b.reshape(DEPTH, 1, n))


PROJ_TM = 1024
PROJ_TN = 1024


def _proj_kernel(h_ref, hc_ref, mod_ref, w_ref, *rest, with_extra):
    if with_extra:
        wx_ref, o_ref, ox_ref, a_scr = rest
    else:
        o_ref, a_scr = rest
    j = pl.program_id(1)

    @pl.when(j == 0)
    def _():
        sh = mod_ref[0, 0:1, :]
        sc = mod_ref[0, 1:2, :]
        h = jnp.where(pl.program_id(0) < R_LAT // PROJ_TM, h_ref[...], hc_ref[...])
        a_scr[...] = (h * (1.0 + sc) + sh).astype(BF16)
        if with_extra:
            ox_ref[...] = _dot(a_scr[...], wx_ref[...])

    o_ref[...] = _dot(a_scr[...], w_ref[...]).astype(o_ref.dtype)


def _proj(stream, mod, w, w_extra=None):
    rows = R_ALL
    n = w.shape[1]
    with_extra = w_extra is not None
    in_specs = [
        *_stream_specs(stream, PROJ_TM),
        pl.BlockSpec((1, 6, D), lambda i, j: (_mod_row(PROJ_TM)(i), 0, 0)),
        pl.BlockSpec((D, PROJ_TN), lambda i, j: (0, j)),
    ]
    out_shape = [jax.ShapeDtypeStruct((rows, n), BF16)]
    out_specs = [pl.BlockSpec((PROJ_TM, PROJ_TN), lambda i, j: (i, j))]
    args = [stream[0], stream[1], mod, w]
    if with_extra:
        in_specs.append(pl.BlockSpec((D, LANE), lambda i, j: (0, 0)))
        out_shape.append(jax.ShapeDtypeStruct((rows, LANE), F32))
        out_specs.append(pl.BlockSpec((PROJ_TM, LANE), lambda i, j: (i, 0)))
        args.append(w_extra)
    out = pl.pallas_call(
        functools.partial(_proj_kernel, with_extra=with_extra),
        out_shape=out_shape,
        grid=(rows // PROJ_TM, n // PROJ_TN),
        in_specs=in_specs,
        out_specs=out_specs,
        scratch_shapes=[pltpu.VMEM((PROJ_TM, D), BF16)],
        compiler_params=_cparams(2),
        name="mod_proj",
    )(*args)
    return out if with_extra else out[0]


GLA_TB = 256
GLA_STEPS = 1 + SEQ // GLA_TB


def _rope_tables():
    half = GLA_DK // 4
    freqs = ROPE_BASE ** (-np.arange(half, dtype=np.float64) / half)
    t = np.arange(SEQ)
    ang_r = (t // GRID_W)[:, None] * freqs
    ang_c = (t % GRID_W)[:, None] * freqs
    cos = np.concatenate([np.cos(ang_r)] * 2 + [np.cos(ang_c)] * 2, axis=1)
    sin = np.concatenate([-np.sin(ang_r), np.sin(ang_r), -np.sin(ang_c), np.sin(ang_c)], axis=1)
    cos = np.concatenate([cos, np.ones((GLA_TB, GLA_DK))], axis=0)
    sin = np.concatenate([sin, np.zeros((GLA_TB, GLA_DK))], axis=0)
    return jnp.asarray(cos, F32), jnp.asarray(sin, F32)


def _rope(x, cos, sin):
    half = LANE // 2
    parts = [pltpu.roll(x[:, g * LANE:(g + 1) * LANE], half, 1) for g in range(x.shape[1] // LANE)]
    return x * cos + jnp.concatenate(parts, axis=1) * sin


def _gla_kernel(q_ref, k_ref, v_ref, glr_ref, wg_ref, bg_ref, cos_ref, sin_ref, o_ref, s_scr, *, reverse):
    @pl.when(pl.program_id(1) == 0)
    def _():
        s_scr[...] = jnp.zeros_like(s_scr)

    row = lax.broadcasted_iota(jnp.int32, (GLA_CHUNK, GLA_CHUNK), 0)
    col = lax.broadcasted_iota(jnp.int32, (GLA_CHUNK, GLA_CHUNK), 1)
    keep = (col >= row) if reverse else (col <= row)
    tri = jnp.where(keep, 1.0, 0.0).astype(BF16)
    stacked = (GLA_H * GLA_CHUNK, GLA_H * LANE)
    head_sel = jnp.where(lax.broadcasted_iota(jnp.int32, stacked, 0) // GLA_CHUNK
                         == lax.broadcasted_iota(jnp.int32, stacked, 1) // LANE, 1.0, 0.0).astype(BF16)
    n_chunks = GLA_TB // GLA_CHUNK
    order = range(n_chunks - 1, -1, -1) if reverse else range(n_chunks)
    for ci in order:
        sl = slice(ci * GLA_CHUNK, (ci + 1) * GLA_CHUNK)
        cos = cos_ref[sl, :]
        sin = sin_ref[sl, :]
        z = _dot_hi(glr_ref[sl, :], wg_ref[0]) + bg_ref[0]
        g = (jnp.minimum(z, 0.0) - jnp.log1p(jnp.exp(-jnp.abs(z)))) * (1.0 / GLA_GATE_NORM)
        g_hi, g_lo = _split_bf16(g)
        cum_all = _dot(tri, g_hi) + _dot(tri, g_lo)
        by_head = lambda t: jnp.concatenate([t[:, hd * GLA_DK:(hd + 1) * GLA_DK] for hd in range(GLA_H)], axis=0)
        decay_all = jnp.exp(_dot_tn(by_head(g_hi), head_sel) + _dot_tn(by_head(g_lo), head_sel))
        for hd in range(GLA_H):
            ks = slice(hd * GLA_DK, (hd + 1) * GLA_DK)
            vs = slice(hd * GLA_DV, (hd + 1) * GLA_DV)
            q = _rope(q_ref[sl, ks].astype(F32), cos, sin)
            k = _rope(k_ref[sl, ks].astype(F32), cos, sin)
            v = v_ref[sl, vs]
            cum = cum_all[:, ks]
            tot = cum[0:1, :] if reverse else cum[GLA_CHUNK - 1:GLA_CHUNK, :]
            q_dec = (q * jnp.exp(cum)).astype(BF16)
            k_inv = (k * jnp.exp(-cum)).astype(BF16)
            k_end = (k * jnp.exp(tot - cum)).astype(BF16)
            att = jnp.where(keep, _dot_nt(q_dec, k_inv), 0.0).astype(BF16)
            state = s_scr[hd]
            o_ref[sl, vs] = (_dot(att, v) + _dot(q_dec, state.astype(BF16))).astype(o_ref.dtype)
            decay = jnp.concatenate([decay_all[:, hd * LANE:(hd + 1) * LANE]] * (GLA_DV // LANE), axis=1)
            s_scr[hd] = state * decay + _dot_tn(k_end, v)


def _gla_scan(p, glr, wg, bg, cos, sin, reverse):
    lat_blocks = SEQ // GLA_TB

    def tblock(s):
        if reverse:
            return jnp.where(s == 0, 0, GLA_STEPS - s)
        return s

    def rows(b, s):
        tb = tblock(s)
        return jnp.where(tb == 0, R_LAT // GLA_TB + b, b * lat_blocks + tb - 1)

    def rope_rows(s):
        tb = tblock(s)
        return jnp.where(tb == 0, lat_blocks, tb - 1)

    d = 1 if reverse else 0
    return pl.pallas_call(
        functools.partial(_gla_kernel, reverse=reverse),
        out_shape=jax.ShapeDtypeStruct((R_ALL, GLA_VD), BF16),
        grid=(B, GLA_STEPS),
        in_specs=[
            pl.BlockSpec((GLA_TB, GLA_QK), lambda b, s: (rows(b, s), 0)),
            pl.BlockSpec((GLA_TB, GLA_QK), lambda b, s: (rows(b, s), 1)),
            pl.BlockSpec((GLA_TB, GLA_VD), lambda b, s: (rows(b, s), 2 * GLA_QK // GLA_VD)),
            pl.BlockSpec((GLA_TB, LANE), lambda b, s: (rows(b, s), 0)),
            pl.BlockSpec((1, LANE, GLA_QK), lambda b, s: (d, 0, 0)),
            pl.BlockSpec((1, 1, GLA_QK), lambda b, s: (d, 0, 0)),
            pl.BlockSpec((GLA_TB, GLA_DK), lambda b, s: (rope_rows(s), 0)),
            pl.BlockSpec((GLA_TB, GLA_DK), lambda b, s: (rope_rows(s), 0)),
        ],
        out_specs=pl.BlockSpec((GLA_TB, GLA_VD), lambda b, s: (rows(b, s), 0)),
        scratch_shapes=[pltpu.VMEM((GLA_H, GLA_DK, GLA_DV), F32)],
        compiler_params=_cparams(2),
        name="gla_scan_bwd" if reverse else "gla_scan_fwd",
    )(p, p, p, glr, wg, bg, cos, sin)


def _nat_bias_table(rpb):
    qc = np.arange(GRID_W)[:, None]
    kc = np.arange(GRID_W)[None, :]
    cs = np.clip(qc - WIN_C // 2, 0, GRID_W - WIN_C)
    col_ok = (kc >= cs) & (kc < cs + WIN_C)
    dc = np.clip(kc - qc + WIN_C - 1, 0, 2 * WIN_C - 2)
    pick = (dc[:, :, None] == np.arange(2 * WIN_C - 1)).astype(np.float32)
    tc = jnp.einsum('had,qkd->haqk', rpb.astype(F32), jnp.asarray(pick), precision=lax.Precision.HIGHEST)
    tc = jnp.where(col_ok[None, None], tc, NEG_INF)
    neg = jnp.full((NAT_H, GRID_W, GRID_W), NEG_INF, F32)
    rows_n = SEQ // GRID_W
    half = WIN_R // 2
    patterns = []
    for r0, bs in ((0, 0), (NAT_QR, NAT_QR - half), (rows_n - NAT_QR, rows_n - NAT_BAND)):
        q_rows = []
        for qr in range(r0, r0 + NAT_QR):
            rs = min(max(qr - half, 0), rows_n - WIN_R)
            blocks = [tc[:, kr - qr + WIN_R - 1] if rs <= kr < rs + WIN_R else neg for kr in range(bs, bs + NAT_BAND)]
            q_rows.append(jnp.concatenate(blocks, axis=-1))
        patterns.append(jnp.concatenate(q_rows, axis=1))
    return jnp.stack(patterns)


def _nat_kernel(q_ref, k_ref, v_ref, kc_ref, vc_ref, bias_ref, o_ref):
    rb = pl.program_id(2)
    rows_n = SEQ // GRID_W
    band_start = jnp.clip(rb * NAT_QR - WIN_R // 2, 0, rows_n - NAT_BAND)
    start = pl.multiple_of(band_start * GRID_W, GRID_W)
    n_keys = NAT_BAND * GRID_W
    for g in range(NAT_G):
        hs = slice(g * NAT_DH, (g + 1) * NAT_DH)
        q = q_ref[:, hs]
        s_loc = _dot_nt(q, k_ref[pl.ds(start, n_keys), hs]) + bias_ref[0, g]
        s_ctx = _dot_nt(q, kc_ref[:, hs])
        m = jnp.maximum(jnp.max(s_loc, axis=1, keepdims=True), jnp.max(s_ctx, axis=1, keepdims=True))
        p_loc = jnp.exp(s_loc - m)
        p_ctx = jnp.exp(s_ctx - m)
        denom = jnp.sum(p_loc, axis=1, keepdims=True) + jnp.sum(p_ctx, axis=1, keepdims=True)
        o = _dot(p_loc.astype(BF16), v_ref[pl.ds(start, n_keys), hs]) + _dot(p_ctx.astype(BF16), vc_ref[:, hs])
        o_ref[:, hs] = (o / denom).astype(o_ref.dtype)


def _nat_attention(p, bias_tab):
    n_rb = SEQ // (NAT_QR * GRID_W)
    tq = NAT_QR * GRID_W
    width = NAT_G * NAT_DH
    hq = D // width

    def pattern(rb):
        return jnp.where(rb == 0, 0, jnp.where(rb == n_rb - 1, 2, 1))

    return pl.pallas_call(
        _nat_kernel,
        out_shape=jax.ShapeDtypeStruct((R_LAT, D), BF16),
        grid=(B, NAT_H // NAT_G, n_rb),
        in_specs=[
            pl.BlockSpec((tq, width), lambda b, h, rb: (b * n_rb + rb, h)),
            pl.BlockSpec((SEQ, width), lambda b, h, rb: (b, hq + h)),
            pl.BlockSpec((SEQ, width), lambda b, h, rb: (b, 2 * hq + h)),
            pl.BlockSpec((CTX, width), lambda b, h, rb: (R_LAT // CTX + b, hq + h)),
            pl.BlockSpec((CTX, width), lambda b, h, rb: (R_LAT // CTX + b, 2 * hq + h)),
            pl.BlockSpec((1, NAT_G, tq, NAT_BAND * GRID_W), lambda b, h, rb: (pattern(rb), h, 0, 0)),
        ],
        out_specs=pl.BlockSpec((tq, width), lambda b, h, rb: (b * n_rb + rb, h)),
        compiler_params=_cparams(3),
        name="nat_attention",
    )(p, p, p, p, p, bias_tab)


MIX_TM = 256


def _layer_norm(u, g, b):
    mu = jnp.mean(u, axis=-1, keepdims=True)
    var = jnp.mean(jnp.square(u - mu), axis=-1, keepdims=True)
    return (u - mu) * lax.rsqrt(var + LN_EPS) * g + b


def _mix_out_kernel(*refs, gla, lat_tiles):
    if gla:
        of_ref, ob_ref, r_ref, ng_ref = refs[:4]
        refs = refs[4:]
    else:
        x_ref = refs[0]
        refs = refs[1:]
    (h_ref, hc_ref, mod_ref, w_ref, lng_ref, lnb_ref, wr_ref, br_ref,
     h1_ref, f_ref, rt_ref, cnt_ref, cnt_scr) = refs

    @pl.when(pl.program_id(0) == 0)
    def _():
        cnt_scr[...] = jnp.zeros_like(cnt_scr)

    if gla:
        parts = []
        for hd in range(GLA_H):
            cs = slice(hd * GLA_DV, (hd + 1) * GLA_DV)
            o = of_ref[:, cs].astype(F32) + ob_ref[:, cs].astype(F32)
            o = o * lax.rsqrt(jnp.mean(jnp.square(o), axis=-1, keepdims=True) + LN_EPS) * ng_ref[...]
            r = r_ref[:, cs].astype(F32)
            parts.append((o * (r * jax.nn.sigmoid(r))).astype(BF16))
        x = jnp.concatenate(parts, axis=1)
    else:
        x = x_ref[...]
    y = _dot(x, w_ref[...])
    gate1 = mod_ref[0, 2:3, :]
    h_in = jnp.where(pl.program_id(0) < lat_tiles, h_ref[...], hc_ref[...])
    h1 = _layer_norm(ALPHA * h_in + gate1 * y, lng_ref[...], lnb_ref[...])
    h1_ref[...] = h1
    f = h1 * (1.0 + mod_ref[0, 4:5, :]) + mod_ref[0, 3:4, :]
    f_ref[...] = f
    route, counts = _route_tile(_dot_hi(f, wr_ref[...]) + br_ref[...], cnt_scr[...])
    rt_ref[...] = route
    cnt_scr[...] = counts
    cnt_ref[...] = jnp.broadcast_to(counts, cnt_ref.shape)


RT_E, RT_W, RT_RANK = 0, 2, 4


def _route_tile(logits, run):
    tm = logits.shape[0]
    lane = lax.broadcasted_iota(jnp.int32, (tm, LANE), 1).astype(F32)

    def first_max(vals):
        top = jnp.max(vals, axis=1, keepdims=True)
        return top, jnp.min(jnp.where(vals == top, lane, float(LANE)), axis=1, keepdims=True)

    gl = jnp.where(lane < N_GROUPS, logits, NEG_INF)
    g_max, g_idx = first_max(gl)
    g_p = 1.0 / jnp.sum(jnp.exp(gl - g_max), axis=1, keepdims=True)
    lo = N_GROUPS + EPG * g_idx
    el = jnp.where((lane >= lo) & (lane < lo + EPG), logits, NEG_INF)
    e1, i1 = first_max(el)
    e2, i2 = first_max(jnp.where(lane == i1, NEG_INF, el))
    t = jnp.exp(e2 - e1)
    p1 = 1.0 / (1.0 + t)
    x1 = i1 - N_GROUPS
    x2 = i2 - N_GROUPS
    onehot = jnp.where((lane == x1) | (lane == x2), 1.0, 0.0)
    row = lax.broadcasted_iota(jnp.int32, (tm, tm), 0)
    col = lax.broadcasted_iota(jnp.int32, (tm, tm), 1)
    before = jnp.where(col < row, 1.0, 0.0).astype(BF16)
    seen = _dot(before, onehot.astype(BF16)) + run
    rank1 = jnp.sum(jnp.where(lane == x1, seen, 0.0), axis=1, keepdims=True)
    rank2 = jnp.sum(jnp.where(lane == x2, seen, 0.0), axis=1, keepdims=True)
    route = jnp.zeros((tm, LANE), F32)
    for k, val in ((RT_E, x1), (RT_E + 1, x2), (RT_W, g_p * p1), (RT_W + 1, g_p * p1 * t),
                   (RT_RANK, rank1), (RT_RANK + 1, rank2)):
        route = jnp.where(lane == k, val, route)
    return route, run + jnp.sum(onehot, axis=0, keepdims=True)


def _mix_out(mix_inputs, stream, mod, w_out, ln_g, ln_b, w_router, b_router, rows, gla):
    row_spec = lambda width: pl.BlockSpec((MIX_TM, width), lambda i: (i, 0))
    full = lambda a: pl.BlockSpec(a.shape, lambda i: (0,) * a.ndim)
    if gla:
        o_f, o_b, p, norm_g = mix_inputs
        in_specs = [row_spec(GLA_VD), row_spec(GLA_VD),
                    pl.BlockSpec((MIX_TM, GLA_VD), lambda i: (i, (2 * GLA_QK + GLA_VD) // GLA_VD)),
                    full(norm_g)]
        args = [o_f, o_b, p, norm_g]
    else:
        (x,) = mix_inputs
        in_specs = [row_spec(D)]
        args = [x]
    in_specs += [*_stream_specs(stream, MIX_TM), pl.BlockSpec((1, 6, D), lambda i: (_mod_row(MIX_TM)(i), 0, 0)),
                 full(w_out), full(ln_g), full(ln_b), full(w_router), full(b_router)]
    args += [stream[0], stream[1], mod, w_out, ln_g, ln_b, w_router, b_router]
    return pl.pallas_call(
        functools.partial(_mix_out_kernel, gla=gla, lat_tiles=R_LAT // MIX_TM),
        out_shape=[jax.ShapeDtypeStruct((rows, D), F32), jax.ShapeDtypeStruct((rows, D), F32),
                   jax.ShapeDtypeStruct((rows, LANE), F32), jax.ShapeDtypeStruct((SUBLANE, LANE), F32)],
        grid=(rows // MIX_TM,),
        in_specs=in_specs,
        out_specs=[row_spec(D), row_spec(D), row_spec(LANE), pl.BlockSpec((SUBLANE, LANE), lambda i: (0, 0))],
        scratch_shapes=[pltpu.VMEM((1, LANE), F32)],
        compiler_params=_cparams(1),
        name="gla_out_ln" if gla else "nat_out_ln",
    )(*args)


def _plan(route, counts_row):
    t = route.shape[0]
    e = route[:, RT_E:RT_E + 2].astype(jnp.int32)
    rank = route[:, RT_RANK:RT_RANK + 2].astype(jnp.int32)
    counts = counts_row[0, :N_EXPERTS].astype(jnp.int32)
    tiles_per = (counts + MOE_TM - 1) // MOE_TM
    tile_end = jnp.cumsum(tiles_per)
    tile_start = tile_end - tiles_per
    n_tiles = 2 * t // MOE_TM + N_EXPERTS
    ids = jnp.arange(n_tiles, dtype=jnp.int32)
    used = tile_end[-1]
    tile_valid = (ids < used).astype(jnp.int32)
    lookup = jnp.minimum(ids, used - 1)
    tile_expert = jnp.sum((tile_end[None, :] <= lookup[:, None]).astype(jnp.int32), axis=1)
    tile_expert = jnp.minimum(tile_expert, N_EXPERTS - 1).astype(jnp.int32)
    base = tile_start * MOE_TM
    experts = jnp.arange(N_EXPERTS, dtype=jnp.int32)
    pos = jnp.sum(jnp.where(e[..., None] == experts, base, 0), axis=-1) + rank
    trailing = n_tiles - 1 - experts
    fill_tile = jnp.concatenate([jnp.maximum(tile_end - 1, 0), trailing]).astype(jnp.int32)
    fill_on = jnp.concatenate([tiles_per > 0, trailing >= used]).astype(jnp.int32)
    return tile_expert, tile_valid, pos.reshape(-1).astype(jnp.int32), fill_tile, fill_on, n_tiles


def _start_row_copies(pos_ref, tile, rows, make):
    base = 2 * tile * rows
    for r in range(rows):
        make(r, pos_ref[base + 2 * r], False).start()
        make(r, pos_ref[base + 2 * r + 1], True).start()


DISP_TM = 256


def _dispatch_kernel(pos_ref, lt_ref, ht_ref, f_ref, xs_hbm, sbuf, zbuf, sem, zsem):
    i = pl.program_id(0)
    n = pl.num_programs(0)
    slot = i % 2

    def fill(e):
        return pltpu.make_async_copy(zbuf, xs_hbm.at[pl.ds(lt_ref[e] * MOE_TM, MOE_TM), :], zsem)

    @pl.when(i == 0)
    def _():
        zbuf[...] = jnp.zeros_like(zbuf)
        for e in range(2 * N_EXPERTS):
            @pl.when(ht_ref[e] != 0)
            def _():
                fill(e).start()
        for e in range(2 * N_EXPERTS):
            @pl.when(ht_ref[e] != 0)
            def _():
                fill(e).wait()

    def wait_scatter(sl):
        for _ in range(2):
            pltpu.make_async_copy(sbuf.at[sl], xs_hbm.at[pl.ds(0, DISP_TM), :], sem.at[sl]).wait()

    @pl.when(i >= 2)
    def _():
        wait_scatter(slot)

    sbuf[slot] = f_ref[...]

    def make(r, dst, second):
        return pltpu.make_async_copy(sbuf.at[slot, pl.ds(r, 1), :], xs_hbm.at[pl.ds(dst, 1), :], sem.at[slot])

    _start_row_copies(pos_ref, i, DISP_TM, make)

    @pl.when(i == n - 1)
    def _():
        wait_scatter(slot)

        @pl.when(n >= 2)
        def _():
            wait_scatter(1 - slot)


def _dispatch(f, pos, last_tile, has_tile, cap):
    t = f.shape[0]
    grid_spec = pltpu.PrefetchScalarGridSpec(
        num_scalar_prefetch=3,
        grid=(t // DISP_TM,),
        in_specs=[pl.BlockSpec((DISP_TM, D), lambda i, *_: (i, 0))],
        out_specs=pl.BlockSpec(memory_space=pl.ANY),
        scratch_shapes=[
            pltpu.VMEM((2, DISP_TM, D), F32),
            pltpu.VMEM((MOE_TM, D), F32),
            pltpu.SemaphoreType.DMA((2,)),
            pltpu.SemaphoreType.DMA,
        ],
    )
    return pl.pallas_call(
        _dispatch_kernel,
        out_shape=jax.ShapeDtypeStruct((cap, D), F32),
        grid_spec=grid_spec,
        compiler_params=_cparams(1),
        name="moe_dispatch",
    )(pos, last_tile, has_tile, f)


def _moe_kernel(te_ref, tv_ref, x_ref, w1_ref, w3_ref, w2_ref, y_ref):
    i = pl.program_id(0)

    @pl.when(tv_ref[i] != 0)
    def _():
        x = x_ref[...].astype(BF16)
        h1 = _dot(x, w1_ref[0, 0])
        h3 = _dot(x, w3_ref[0, 0])
        a = (h1 * jax.nn.sigmoid(h1) * h3).astype(BF16)
        y_ref[...] = _dot(a, w2_ref[0, 0])

    @pl.when(tv_ref[i] == 0)
    def _():
        y_ref[...] = jnp.zeros_like(y_ref)


def _moe_experts(xs, tile_expert, tile_valid, layer, w1, w3, w2):
    cap = xs.shape[0]
    grid_spec = pltpu.PrefetchScalarGridSpec(
        num_scalar_prefetch=2,
        grid=(cap // MOE_TM,),
        in_specs=[
            pl.BlockSpec((MOE_TM, D), lambda i, te, tv: (i * tv[i], 0)),
            pl.BlockSpec((1, 1, D, D_EXPERT), lambda i, te, tv: (layer, te[i], 0, 0)),
            pl.BlockSpec((1, 1, D, D_EXPERT), lambda i, te, tv: (layer, te[i], 0, 0)),
            pl.BlockSpec((1, 1, D_EXPERT, D), lambda i, te, tv: (layer, te[i], 0, 0)),
        ],
        out_specs=pl.BlockSpec((MOE_TM, D), lambda i, te, tv: (i, 0)),
    )
    return pl.pallas_call(
        _moe_kernel,
        out_shape=jax.ShapeDtypeStruct((cap, D), F32),
        grid_spec=grid_spec,
        compiler_params=_cparams(1),
        name="moe_experts",
    )(tile_expert, tile_valid, xs, w1, w3, w2)


FIN_TM = 256


def _final_kernel(pos_ref, h_ref, rt_ref, mod_ref, lng_ref, lnb_ref, ys_hbm, o_ref, abuf, bbuf, sem):
    i = pl.program_id(0)
    n = pl.num_programs(0)
    slot = i % 2

    def start_gather(tile, sl):
        def make(r, src, second):
            buf = bbuf if second else abuf
            return pltpu.make_async_copy(ys_hbm.at[pl.ds(src, 1), :], buf.at[sl, pl.ds(r, 1), :], sem.at[sl])

        _start_row_copies(pos_ref, tile, FIN_TM, make)

    @pl.when(i == 0)
    def _():
        start_gather(0, 0)

    @pl.when(i + 1 < n)
    def _():
        start_gather(i + 1, 1 - slot)

    for buf in (abuf, bbuf):
        pltpu.make_async_copy(ys_hbm.at[pl.ds(0, FIN_TM), :], buf.at[slot], sem.at[slot]).wait()

    y = abuf[slot] * rt_ref[:, RT_W:RT_W + 1] + bbuf[slot] * rt_ref[:, RT_W + 1:RT_W + 2]
    gate2 = mod_ref[0, 5:6, :]
    o_ref[...] = _layer_norm(ALPHA * h_ref[...] + gate2 * y, lng_ref[...], lnb_ref[...])


def _ffn_residual(h1, route, pos, ys, mod, ln_g, ln_b):
    rows = h1.shape[0]
    grid_spec = pltpu.PrefetchScalarGridSpec(
        num_scalar_prefetch=1,
        grid=(rows // FIN_TM,),
        in_specs=[
            pl.BlockSpec((FIN_TM, D), lambda i, *_: (i, 0)),
            pl.BlockSpec((FIN_TM, LANE), lambda i, *_: (i, 0)),
            pl.BlockSpec((1, 6, D), lambda i, *_: (_mod_row(FIN_TM)(i), 0, 0)),
            pl.BlockSpec((1, D), lambda i, *_: (0, 0)),
            pl.BlockSpec((1, D), lambda i, *_: (0, 0)),
            pl.BlockSpec(memory_space=pl.ANY),
        ],
        out_specs=pl.BlockSpec((FIN_TM, D), lambda i, *_: (i, 0)),
        scratch_shapes=[
            pltpu.VMEM((2, FIN_TM, D), F32),
            pltpu.VMEM((2, FIN_TM, D), F32),
            pltpu.SemaphoreType.DMA((2,)),
        ],
    )
    return pl.pallas_call(
        _final_kernel,
        out_shape=jax.ShapeDtypeStruct((rows, D), F32),
        grid_spec=grid_spec,
        compiler_params=_cparams(1),
        name="ffn_residual_ln",
    )(pos, h1, route, mod, ln_g, ln_b, ys)


def _moe_ffn(h1, f, route, counts, mod, layer, w1, w3, w2, ln_g, ln_b):
    tile_expert, tile_valid, pos, fill_tile, fill_on, n_tiles = _plan(route, counts)
    xs = _dispatch(f, pos, fill_tile, fill_on, n_tiles * MOE_TM)
    ys = _moe_experts(xs, tile_expert, tile_valid, layer, w1, w3, w2)
    return _ffn_residual(h1, route, pos, ys, mod, ln_g, ln_b)


def _router_weights(w_group, b_group, w_expert, b_expert):
    pad = LANE - N_GROUPS - N_EXPERTS
    w = jnp.concatenate([w_group, w_expert, jnp.zeros((D, pad), F32)], axis=1)
    b = jnp.concatenate([b_group, b_expert, jnp.zeros((pad,), F32)])[None, :]
    return w, b


def kernel(x, c, ctx, c_ctx, ada_w, ada_b, ln_g, ln_b, gla_w_in, gla_w_gate, gla_b_gate, gla_norm_g, gla_w_out,
           nat_w_in, nat_rpb, nat_w_out, moe_w_group, moe_b_group, moe_w_expert, moe_b_expert,
           moe_w1, moe_w3, moe_w2):
    stream = (x.reshape(R_LAT, D), ctx.reshape(R_CTX, D), 0)
    cc = jnp.concatenate([c, c_ctx[None, :], jnp.zeros((SUBLANE - B - 1, D), F32)], axis=0)
    mod = _ada_table(cc, ada_w, ada_b).reshape(DEPTH, SUBLANE, 6, D)
    w1, w3, w2 = moe_w1.astype(BF16), moe_w3.astype(BF16), moe_w2.astype(BF16)

    w_in = gla_w_in[0]
    w_main = jnp.concatenate([w_in[:, :GLA_QK] * (GLA_DK ** -0.5), w_in[:, GLA_QK:GLA_MAIN]], axis=1).astype(BF16)
    w_glr = jnp.concatenate([w_in[:, GLA_MAIN:], jnp.zeros((D, LANE - 2 * GLA_RANK), F32)], axis=1).astype(BF16)
    p, glr = _proj(stream, mod[0], w_main, w_glr)
    wg = jnp.zeros((2, LANE, GLA_QK), F32)
    wg = wg.at[0, :GLA_RANK].set(gla_w_gate[0, 0]).at[1, GLA_RANK:2 * GLA_RANK].set(gla_w_gate[0, 1])
    bg = gla_b_gate[0][:, None, :]
    cos, sin = _rope_tables()
    o_f = _gla_scan(p, glr, wg, bg, cos, sin, reverse=False)
    o_b = _gla_scan(p, glr, wg, bg, cos, sin, reverse=True)
    w_r, b_r = _router_weights(moe_w_group[0], moe_b_group[0], moe_w_expert[0], moe_b_expert[0])
    h1, f, route, counts = _mix_out((o_f, o_b, p, gla_norm_g[0][None, :]), stream, mod[0], gla_w_out[0].astype(BF16),
                                    ln_g[0, 0][None, :], ln_b[0, 0][None, :], w_r, b_r, R_ALL, gla=True)
    h = _moe_ffn(h1, f, route, counts, mod[0], 0, w1, w3, w2, ln_g[0, 1][None, :], ln_b[0, 1][None, :])

    w_in = nat_w_in[0]
    w_nat = jnp.concatenate([w_in[:, :D] * (NAT_DH ** -0.5), w_in[:, D:]], axis=1).astype(BF16)
    stream = (h, h, R_LAT)
    p = _proj(stream, mod[1], w_nat)
    o = _nat_attention(p, _nat_bias_table(nat_rpb[0]))
    w_r, b_r = _router_weights(moe_w_group[1], moe_b_group[1], moe_w_expert[1], moe_b_expert[1])
    h1, f, route, counts = _mix_out((o,), stream, mod[1], nat_w_out[0].astype(BF16),
                                    ln_g[1, 0][None, :], ln_b[1, 0][None, :], w_r, b_r, R_LAT, gla=False)
    out = _moe_ffn(h1, f, route, counts, mod[1], 1, w1, w3, w2, ln_g[1, 1][None, :], ln_b[1, 1][None, :])
    return out.reshape(B, SEQ, D)
```

```python
import functools

import jax
import jax.numpy as jnp
import numpy as np
from jax import lax
from jax.experimental import pallas as pl
from jax.experimental.pallas import tpu as pltpu

F32 = jnp.float32
BF16 = jnp.bfloat16

D = 2048
B = 4
SEQ = 4096
DEPTH = 2
GRID_W = 64
CTX = 256
R_LAT = B * SEQ
R_CTX = B * CTX
R_ALL = R_LAT + R_CTX

GLA_H = 4
GLA_DK = 256
GLA_DV = 512
GLA_CHUNK = 64
GLA_RANK = 16
GLA_QK = GLA_H * GLA_DK
GLA_VD = GLA_H * GLA_DV
GLA_MAIN = 2 * GLA_QK + 2 * GLA_VD
GLA_GATE_NORM = 16.0
ROPE_BASE = 10000.0

NAT_H = 16
NAT_DH = 128
WIN_R = 8
WIN_C = 16
NAT_QR = 4
NAT_BAND = 12
NAT_G = 4

N_GROUPS = 4
EPG = 4
N_EXPERTS = 16
D_EXPERT = 1024
MOE_TM = 256

ALPHA = (2 * DEPTH) ** 0.25
LN_EPS = 1e-5
NEG_INF = -1e30

LANE = 128
SUBLANE = 8
VMEM_LIMIT = 56 * 1024 * 1024


def _cparams(n_axes):
    return pltpu.CompilerParams(dimension_semantics=("arbitrary",) * n_axes,
                                vmem_limit_bytes=VMEM_LIMIT)


def _split_bf16(x):
    hi = x.astype(BF16)
    lo = (x - hi.astype(F32)).astype(BF16)
    return hi, lo


def _dot(a, b):
    return jnp.dot(a, b, preferred_element_type=F32)


def _dot_hi(a, b):
    ah, al = _split_bf16(a)
    bh, bl = _split_bf16(b)
    return _dot(ah, bh) + _dot(al, bh) + _dot(ah, bl)


def _dot_nt(a, b):
    return lax.dot_general(a, b, (((1,), (1,)), ((), ())), preferred_element_type=F32)


def _dot_tn(a, b):
    return lax.dot_general(a, b, (((0,), (0,)), ((), ())), preferred_element_type=F32)


D_PACK = D // 2


def _pack_rows(x):
    lo = pltpu.bitcast(x[:, :D_PACK].astype(BF16).astype(F32), jnp.uint32)
    hi = pltpu.bitcast(x[:, D_PACK:].astype(BF16).astype(F32), jnp.uint32)
    return (lo >> 16) | hi


def _unpack_rows(u):
    lo = pltpu.bitcast(u << 16, F32)
    hi = pltpu.bitcast(u & jnp.uint32(0xFFFF0000), F32)
    return jnp.concatenate([lo, hi], axis=1)


CAST_STEPS = 64


def _cast_specs(casts, step_of):
    in_specs, out_shape, out_specs, args = [], [], [], []
    for arr, n_rows, first_row in casts:
        c_rows = n_rows // CAST_STEPS
        first = first_row // c_rows
        chunk = lambda *idx: jnp.minimum(step_of(*idx), CAST_STEPS - 1)
        in_specs.append(pl.BlockSpec((c_rows, arr.shape[1]), lambda *idx, first=first: (first + chunk(*idx), 0)))
        out_shape.append(jax.ShapeDtypeStruct((n_rows, arr.shape[1]), BF16))
        out_specs.append(pl.BlockSpec((c_rows, arr.shape[1]), lambda *idx: (chunk(*idx), 0)))
        args.append(arr)
    return in_specs, out_shape, out_specs, args


def _mod_row(tile_rows):
    return lambda i: jnp.minimum((i * tile_rows) // SEQ, B)


def _stream_specs(stream, tile_rows, lag=0):
    n_lat = R_LAT // tile_rows
    base = stream[2] // tile_rows
    return (pl.BlockSpec((tile_rows, D), lambda i, *_: (jnp.clip(i - lag, 0, n_lat - 1), 0)),
            pl.BlockSpec((tile_rows, D), lambda i, *_: (base + jnp.maximum(i - lag - n_lat, 0), 0)))


ADA_TN = 1024


def _ada_kernel(c_ref, w_ref, b_ref, o_ref):
    x = c_ref[...]
    s = x * jax.nn.sigmoid(x)
    o_ref[0] = _dot_hi(s, w_ref[0]) + b_ref[0]


def _ada_table(cc, ada_w, ada_b):
    n = 6 * D
    return pl.pallas_call(
        _ada_kernel,
        out_shape=jax.ShapeDtypeStruct((DEPTH, SUBLANE, n), F32),
        grid=(DEPTH, n // ADA_TN),
        in_specs=[
            pl.BlockSpec((SUBLANE, D), lambda l, j: (0, 0)),
            pl.BlockSpec((1, D, ADA_TN), lambda l, j: (l, 0, j)),
            pl.BlockSpec((1, 1, ADA_TN), lambda l, j: (l, 0, j)),
        ],
        out_specs=pl.BlockSpec((1, SUBLANE, ADA_TN), lambda l, j: (l, 0, j)),
        compiler_params=_cparams(2),
        name="ada_table",
    )(cc, ada_w, ada_b.reshape(DEPTH, 1, n))


PROJ_TM = 1024
PROJ_TN = 1024


def _proj_kernel(h_ref, hc_ref, mod_ref, w_ref, *rest, with_extra):
    if with_extra:
        wx_ref, o_ref, ox_ref, a_scr = rest
    else:
        o_ref, a_scr = rest
    j = pl.program_id(1)

    @pl.when(j == 0)
    def _():
        sh = mod_ref[0, 0:1, :]
        sc = mod_ref[0, 1:2, :]
        h = jnp.where(pl.program_id(0) < R_LAT // PROJ_TM, h_ref[...], hc_ref[...])
        a_scr[...] = (h * (1.0 + sc) + sh).astype(BF16)
        if with_extra:
            ox_ref[...] = _dot(a_scr[...], wx_ref[...])

    o_ref[...] = _dot(a_scr[...], w_ref[...]).astype(o_ref.dtype)


def _proj(stream, mod, w, w_extra=None):
    rows = R_ALL
    n = w.shape[1]
    with_extra = w_extra is not None
    in_specs = [
        *_stream_specs(stream, PROJ_TM),
        pl.BlockSpec((1, 6, D), lambda i, j: (_mod_row(PROJ_TM)(i), 0, 0)),
        pl.BlockSpec((D, PROJ_TN), lambda i, j: (0, j)),
    ]
    out_shape = [jax.ShapeDtypeStruct((rows, n), BF16)]
    out_specs = [pl.BlockSpec((PROJ_TM, PROJ_TN), lambda i, j: (i, j))]
    args = [stream[0], stream[1], mod, w]
    if with_extra:
        in_specs.append(pl.BlockSpec((D, LANE), lambda i, j: (0, 0)))
        out_shape.append(jax.ShapeDtypeStruct((rows, LANE), F32))
        out_specs.append(pl.BlockSpec((PROJ_TM, LANE), lambda i, j: (i, 0)))
        args.append(w_extra)
    out = pl.pallas_call(
        functools.partial(_proj_kernel, with_extra=with_extra),
        out_shape=out_shape,
        grid=(rows // PROJ_TM, n // PROJ_TN),
        in_specs=in_specs,
        out_specs=out_specs,
        scratch_shapes=[pltpu.VMEM((PROJ_TM, D), BF16)],
        compiler_params=_cparams(2),
        name="mod_proj",
    )(*args)
    return out if with_extra else out[0]


GLA_TB = 256
GLA_STEPS = 1 + SEQ // GLA_TB


def _rope_tables():
    half = GLA_DK // 4
    freqs = ROPE_BASE ** (-np.arange(half, dtype=np.float64) / half)
    t = np.arange(SEQ)
    ang_r = (t // GRID_W)[:, None] * freqs
    ang_c = (t % GRID_W)[:, None] * freqs
    cos = np.concatenate([np.cos(ang_r)] * 2 + [np.cos(ang_c)] * 2, axis=1)
    sin = np.concatenate([-np.sin(ang_r), np.sin(ang_r), -np.sin(ang_c), np.sin(ang_c)], axis=1)
    cos = np.concatenate([cos, np.ones((GLA_TB, GLA_DK))], axis=0)
    sin = np.concatenate([sin, np.zeros((GLA_TB, GLA_DK))], axis=0)
    return jnp.asarray(cos, F32), jnp.asarray(sin, F32)


def _rope(x, cos, sin):
    half = LANE // 2
    parts = [pltpu.roll(x[:, g * LANE:(g + 1) * LANE], half, 1) for g in range(x.shape[1] // LANE)]
    return x * cos + jnp.concatenate(parts, axis=1) * sin


def _gla_kernel(q_ref, k_ref, v_ref, glr_ref, wg_ref, bg_ref, cos_ref, sin_ref, *rest, reverse, n_cast):
    cast_in, o_ref, cast_out, s_scr = rest[:n_cast], rest[n_cast], rest[n_cast + 1:-1], rest[-1]
    for src, dst in zip(cast_in, cast_out):
        dst[...] = src[...].astype(BF16)

    @pl.when(pl.program_id(1) == 0)
    def _():
        s_scr[...] = jnp.zeros_like(s_scr)

    row = lax.broadcasted_iota(jnp.int32, (GLA_CHUNK, GLA_CHUNK), 0)
    col = lax.broadcasted_iota(jnp.int32, (GLA_CHUNK, GLA_CHUNK), 1)
    keep = (col >= row) if reverse else (col <= row)
    tri = jnp.where(keep, 1.0, 0.0).astype(BF16)
    ones = jnp.ones((GLA_CHUNK, LANE), BF16)
    n_chunks = GLA_TB // GLA_CHUNK
    order = range(n_chunks - 1, -1, -1) if reverse else range(n_chunks)
    for ci in order:
        sl = slice(ci * GLA_CHUNK, (ci + 1) * GLA_CHUNK)
        cos = cos_ref[sl, :]
        sin = sin_ref[sl, :]
        z = _dot_hi(glr_ref[sl, :], wg_ref[0]) + bg_ref[0]
        g = (jnp.minimum(z, 0.0) - jnp.log1p(jnp.exp(-jnp.abs(z)))) * (1.0 / GLA_GATE_NORM)
        g_hi, g_lo = _split_bf16(g)
        cum_all = _dot(tri, g_hi) + _dot(tri, g_lo)
        decay_all = jnp.exp(_dot_tn(g_hi, ones) + _dot_tn(g_lo, ones))
        for hd in range(GLA_H):
            ks = slice(hd * GLA_DK, (hd + 1) * GLA_DK)
            vs = slice(hd * GLA_DV, (hd + 1) * GLA_DV)
            q = _rope(q_ref[sl, ks].astype(F32), cos, sin)
            k = _rope(k_ref[sl, ks].astype(F32), cos, sin)
            v = v_ref[sl, vs]
            cum = cum_all[:, ks]
            tot = cum[0:1, :] if reverse else cum[GLA_CHUNK - 1:GLA_CHUNK, :]
            q_dec = (q * jnp.exp(cum)).astype(BF16)
            k_inv = (k * jnp.exp(-cum)).astype(BF16)
            k_end = (k * jnp.exp(tot - cum)).astype(BF16)
            att = jnp.where(keep, _dot_nt(q_dec, k_inv), 0.0).astype(BF16)
            state = s_scr[hd]
            o_ref[sl, vs] = (_dot(att, v) + _dot(q_dec, state.astype(BF16))).astype(o_ref.dtype)
            decay = jnp.concatenate([decay_all[ks, :]] * (GLA_DV // LANE), axis=1)
            s_scr[hd] = state * decay + _dot_tn(k_end, v)


def _gla_scan(p, glr, wg, bg, cos, sin, reverse, casts):
    lat_blocks = SEQ // GLA_TB

    def tblock(s):
        if reverse:
            return jnp.where(s == 0, 0, GLA_STEPS - s)
        return s

    def rows(b, s):
        tb = tblock(s)
        return jnp.where(tb == 0, R_LAT // GLA_TB + b, b * lat_blocks + tb - 1)

    def rope_rows(s):
        tb = tblock(s)
        return jnp.where(tb == 0, lat_blocks, tb - 1)

    d = 1 if reverse else 0
    c_in, c_shape, c_out, c_args = _cast_specs(casts, lambda b, s: b * GLA_STEPS + s)
    return pl.pallas_call(
        functools.partial(_gla_kernel, reverse=reverse, n_cast=len(casts)),
        out_shape=[jax.ShapeDtypeStruct((R_ALL, GLA_VD), BF16), *c_shape],
        grid=(B, GLA_STEPS),
        in_specs=[
            pl.BlockSpec((GLA_TB, GLA_QK), lambda b, s: (rows(b, s), 0)),
            pl.BlockSpec((GLA_TB, GLA_QK), lambda b, s: (rows(b, s), 1)),
            pl.BlockSpec((GLA_TB, GLA_VD), lambda b, s: (rows(b, s), 2 * GLA_QK // GLA_VD)),
            pl.BlockSpec((GLA_TB, LANE), lambda b, s: (rows(b, s), 0)),
            pl.BlockSpec((1, LANE, GLA_QK), lambda b, s: (d, 0, 0)),
            pl.BlockSpec((1, 1, GLA_QK), lambda b, s: (d, 0, 0)),
            pl.BlockSpec((GLA_TB, GLA_DK), lambda b, s: (rope_rows(s), 0)),
            pl.BlockSpec((GLA_TB, GLA_DK), lambda b, s: (rope_rows(s), 0)),
            *c_in,
        ],
        out_specs=[pl.BlockSpec((GLA_TB, GLA_VD), lambda b, s: (rows(b, s), 0)), *c_out],
        scratch_shapes=[pltpu.VMEM((GLA_H, GLA_DK, GLA_DV), F32)],
        compiler_params=_cparams(2),
        name="gla_scan_bwd" if reverse else "gla_scan_fwd",
    )(p, p, p, glr, wg, bg, cos, sin, *c_args)


def _nat_bias_table(rpb):
    qc = np.arange(GRID_W)[:, None]
    kc = np.arange(GRID_W)[None, :]
    cs = np.clip(qc - WIN_C // 2, 0, GRID_W - WIN_C)
    col_ok = (kc >= cs) & (kc < cs + WIN_C)
    dc = np.clip(kc - qc + WIN_C - 1, 0, 2 * WIN_C - 2)
    pick = (dc[:, :, None] == np.arange(2 * WIN_C - 1)).astype(np.float32)
    tc = jnp.einsum('had,qkd->haqk', rpb.astype(F32), jnp.asarray(pick), precision=lax.Precision.HIGHEST)
    tc = jnp.where(col_ok[None, None], tc, NEG_INF)
    neg = jnp.full((NAT_H, GRID_W, GRID_W), NEG_INF, F32)
    rows_n = SEQ // GRID_W
    half = WIN_R // 2
    patterns = []
    for r0, bs in ((0, 0), (NAT_QR, NAT_QR - half), (rows_n - NAT_QR, rows_n - NAT_BAND)):
        q_rows = []
        for qr in range(r0, r0 + NAT_QR):
            rs = min(max(qr - half, 0), rows_n - WIN_R)
            blocks = [tc[:, kr - qr + WIN_R - 1] if rs <= kr < rs + WIN_R else neg for kr in range(bs, bs + NAT_BAND)]
            q_rows.append(jnp.concatenate(blocks, axis=-1))
        patterns.append(jnp.concatenate(q_rows, axis=1))
    return jnp.stack(patterns)


def _nat_kernel(q_ref, k_ref, v_ref, kc_ref, vc_ref, bias_ref, o_ref):
    rb = pl.program_id(2)
    rows_n = SEQ // GRID_W
    band_start = jnp.clip(rb * NAT_QR - WIN_R // 2, 0, rows_n - NAT_BAND)
    start = pl.multiple_of(band_start * GRID_W, GRID_W)
    n_keys = NAT_BAND * GRID_W
    for g in range(NAT_G):
        hs = slice(g * NAT_DH, (g + 1) * NAT_DH)
        q = q_ref[:, hs]
        s_loc = _dot_nt(q, k_ref[pl.ds(start, n_keys), hs]) + bias_ref[0, g]
        s_ctx = _dot_nt(q, kc_ref[:, hs])
        m = jnp.maximum(jnp.max(s_loc, axis=1, keepdims=True), jnp.max(s_ctx, axis=1, keepdims=True))
        p_loc = jnp.exp(s_loc - m)
        p_ctx = jnp.exp(s_ctx - m)
        denom = jnp.sum(p_loc, axis=1, keepdims=True) + jnp.sum(p_ctx, axis=1, keepdims=True)
        o = _dot(p_loc.astype(BF16), v_ref[pl.ds(start, n_keys), hs]) + _dot(p_ctx.astype(BF16), vc_ref[:, hs])
        o_ref[:, hs] = (o / denom).astype(o_ref.dtype)


def _nat_attention(p, bias_tab):
    n_rb = SEQ // (NAT_QR * GRID_W)
    tq = NAT_QR * GRID_W
    width = NAT_G * NAT_DH
    hq = D // width

    def pattern(rb):
        return jnp.where(rb == 0, 0, jnp.where(rb == n_rb - 1, 2, 1))

    return pl.pallas_call(
        _nat_kernel,
        out_shape=jax.ShapeDtypeStruct((R_LAT, D), BF16),
        grid=(B, NAT_H // NAT_G, n_rb),
        in_specs=[
            pl.BlockSpec((tq, width), lambda b, h, rb: (b * n_rb + rb, h)),
            pl.BlockSpec((SEQ, width), lambda b, h, rb: (b, hq + h)),
            pl.BlockSpec((SEQ, width), lambda b, h, rb: (b, 2 * hq + h)),
            pl.BlockSpec((CTX, width), lambda b, h, rb: (R_LAT // CTX + b, hq + h)),
            pl.BlockSpec((CTX, width), lambda b, h, rb: (R_LAT // CTX + b, 2 * hq + h)),
            pl.BlockSpec((1, NAT_G, tq, NAT_BAND * GRID_W), lambda b, h, rb: (pattern(rb), h, 0, 0)),
        ],
        out_specs=pl.BlockSpec((tq, width), lambda b, h, rb: (b * n_rb + rb, h)),
        compiler_params=_cparams(3),
        name="nat_attention",
    )(p, p, p, p, p, bias_tab)


MIX_TM = 256


def _layer_norm(u, g, b):
    mu = jnp.mean(u, axis=-1, keepdims=True)
    var = jnp.mean(jnp.square(u - mu), axis=-1, keepdims=True)
    return (u - mu) * lax.rsqrt(var + LN_EPS) * g + b


def _mix_out_kernel(*refs, gla, lat_tiles):
    if gla:
        of_ref, ob_ref, r_ref, ng_ref = refs[:4]
        refs = refs[4:]
    else:
        x_ref = refs[0]
        refs = refs[1:]
    (h_ref, hc_ref, mod_ref, w_ref, lng_ref, lnb_ref, wr_ref, br_ref,
     h1_ref, f_ref, rt_ref, cnt_ref, y_even, y_odd, cnt_scr) = refs
    s = pl.program_id(0)

    @pl.when(s == 0)
    def _():
        cnt_scr[...] = jnp.zeros_like(cnt_scr)
        y_odd[...] = jnp.zeros_like(y_odd)

    def step(y_new, y_prev):
        if gla:
            parts = []
            for hd in range(GLA_H):
                cs = slice(hd * GLA_DV, (hd + 1) * GLA_DV)
                o = of_ref[:, cs].astype(F32) + ob_ref[:, cs].astype(F32)
                o = o * lax.rsqrt(jnp.mean(jnp.square(o), axis=-1, keepdims=True) + LN_EPS) * ng_ref[...]
                r = r_ref[:, cs].astype(F32)
                parts.append((o * (r * jax.nn.sigmoid(r))).astype(BF16))
            x = jnp.concatenate(parts, axis=1)
        else:
            x = x_ref[...]
        y_new[...] = _dot(x, w_ref[...])

        gate1 = mod_ref[0, 2:3, :]
        h_in = jnp.where(s - 1 < lat_tiles, h_ref[...], hc_ref[...])
        h1 = _layer_norm(ALPHA * h_in + gate1 * y_prev[...], lng_ref[...], lnb_ref[...])
        h1_ref[...] = h1
        f = h1 * (1.0 + mod_ref[0, 4:5, :]) + mod_ref[0, 3:4, :]
        f_ref[...] = _pack_rows(f)
        live = jnp.where(s > 0, 1.0, 0.0)
        w_hi, w_lo = _split_bf16(wr_ref[...])
        f_bf = f.astype(BF16)
        logits = _dot(f_bf, w_hi) + _dot(f_bf, w_lo) + br_ref[...]
        route, counts = _route_tile(logits, cnt_scr[...], live)
        rt_ref[...] = route
        cnt_scr[...] = counts
        cnt_ref[...] = jnp.broadcast_to(counts, cnt_ref.shape)

    @pl.when(s % 2 == 0)
    def _():
        step(y_even, y_odd)

    @pl.when(s % 2 == 1)
    def _():
        step(y_odd, y_even)


RT_E, RT_W, RT_RANK = 0, 2, 4


def _route_tile(logits, run, live):
    tm = logits.shape[0]
    lane = lax.broadcasted_iota(jnp.int32, (tm, LANE), 1).astype(F32)

    def first_max(vals):
        top = jnp.max(vals, axis=1, keepdims=True)
        return top, jnp.min(jnp.where(vals == top, lane, float(LANE)), axis=1, keepdims=True)

    gl = jnp.where(lane < N_GROUPS, logits, NEG_INF)
    g_max, g_idx = first_max(gl)
    g_p = 1.0 / jnp.sum(jnp.exp(gl - g_max), axis=1, keepdims=True)
    lo = N_GROUPS + EPG * g_idx
    el = jnp.where((lane >= lo) & (lane < lo + EPG), logits, NEG_INF)
    e1, i1 = first_max(el)
    e2, i2 = first_max(jnp.where(lane == i1, NEG_INF, el))
    t = jnp.exp(e2 - e1)
    p1 = 1.0 / (1.0 + t)
    x1 = i1 - N_GROUPS
    x2 = i2 - N_GROUPS
    onehot = jnp.where((lane == x1) | (lane == x2), 1.0, 0.0)
    row = lax.broadcasted_iota(jnp.int32, (tm, tm), 0)
    col = lax.broadcasted_iota(jnp.int32, (tm, tm), 1)
    before = jnp.where(col < row, 1.0, 0.0).astype(BF16)
    seen = _dot(before, onehot.astype(BF16)) + run
    rank1 = jnp.sum(jnp.where(lane == x1, seen, 0.0), axis=1, keepdims=True)
    rank2 = jnp.sum(jnp.where(lane == x2, seen, 0.0), axis=1, keepdims=True)
    route = jnp.zeros((tm, LANE), F32)
    for k, val in ((RT_E, x1), (RT_E + 1, x2), (RT_W, g_p * p1), (RT_W + 1, g_p * p1 * t),
                   (RT_RANK, rank1), (RT_RANK + 1, rank2)):
        route = jnp.where(lane == k, val, route)
    return route, run + live * jnp.sum(onehot, axis=0, keepdims=True)


def _mix_out(mix_inputs, stream, mod, w_out, ln_g, ln_b, w_router, b_router, rows, gla):
    n_tiles = rows // MIX_TM
    row_spec = lambda width: pl.BlockSpec((MIX_TM, width), lambda i: (jnp.minimum(i, n_tiles - 1), 0))
    done_spec = lambda width: pl.BlockSpec((MIX_TM, width), lambda i: (jnp.maximum(i - 1, 0), 0))
    full = lambda a: pl.BlockSpec(a.shape, lambda i: (0,) * a.ndim)
    if gla:
        o_f, o_b, p, norm_g = mix_inputs
        in_specs = [row_spec(GLA_VD), row_spec(GLA_VD),
                    pl.BlockSpec((MIX_TM, GLA_VD),
                                 lambda i: (jnp.minimum(i, n_tiles - 1), (2 * GLA_QK + GLA_VD) // GLA_VD)),
                    full(norm_g)]
        args = [o_f, o_b, p, norm_g]
    else:
        (x,) = mix_inputs
        in_specs = [row_spec(D)]
        args = [x]
    in_specs += [*_stream_specs(stream, MIX_TM, lag=1),
                 pl.BlockSpec((1, 6, D), lambda i: (_mod_row(MIX_TM)(jnp.maximum(i - 1, 0)), 0, 0)),
                 full(w_out), full(ln_g), full(ln_b), full(w_router), full(b_router)]
    args += [stream[0], stream[1], mod, w_out, ln_g, ln_b, w_router, b_router]
    return pl.pallas_call(
        functools.partial(_mix_out_kernel, gla=gla, lat_tiles=R_LAT // MIX_TM),
        out_shape=[jax.ShapeDtypeStruct((rows, D), F32), jax.ShapeDtypeStruct((rows, D_PACK), jnp.uint32),
                   jax.ShapeDtypeStruct((rows, LANE), F32), jax.ShapeDtypeStruct((SUBLANE, LANE), F32)],
        grid=(n_tiles + 1,),
        in_specs=in_specs,
        out_specs=[done_spec(D), done_spec(D_PACK), done_spec(LANE), pl.BlockSpec((SUBLANE, LANE), lambda i: (0, 0))],
        scratch_shapes=[pltpu.VMEM((MIX_TM, D), F32), pltpu.VMEM((MIX_TM, D), F32), pltpu.VMEM((1, LANE), F32)],
        compiler_params=_cparams(1),
        name="gla_out_ln" if gla else "nat_out_ln",
    )(*args)


def _plan(route, counts_row):
    t = route.shape[0]
    e = route[:, RT_E:RT_E + 2].astype(jnp.int32)
    rank = route[:, RT_RANK:RT_RANK + 2].astype(jnp.int32)
    counts = counts_row[0, :N_EXPERTS].astype(jnp.int32)
    tiles_per = (counts + MOE_TM - 1) // MOE_TM
    tile_end = jnp.cumsum(tiles_per)
    tile_start = tile_end - tiles_per
    n_tiles = 2 * t // MOE_TM + N_EXPERTS
    ids = jnp.arange(n_tiles, dtype=jnp.int32)
    used = tile_end[-1]
    tile_valid = (ids < used).astype(jnp.int32)
    lookup = jnp.minimum(ids, used - 1)
    tile_expert = jnp.sum((tile_end[None, :] <= lookup[:, None]).astype(jnp.int32), axis=1)
    tile_expert = jnp.minimum(tile_expert, N_EXPERTS - 1).astype(jnp.int32)
    base = tile_start * MOE_TM
    experts = jnp.arange(N_EXPERTS, dtype=jnp.int32)
    pos = jnp.sum(jnp.where(e[..., None] == experts, base, 0), axis=-1) + rank
    trailing = n_tiles - 1 - experts
    fill_tile = jnp.concatenate([jnp.maximum(tile_end - 1, 0), trailing]).astype(jnp.int32)
    fill_on = jnp.concatenate([tiles_per > 0, trailing >= used]).astype(jnp.int32)
    return tile_expert, tile_valid, pos.reshape(-1).astype(jnp.int32), fill_tile, fill_on, n_tiles


def _start_row_copies(pos_ref, tile, rows, make):
    base = 2 * tile * rows
    for r in range(rows):
        make(r, pos_ref[base + 2 * r], False).start()
        make(r, pos_ref[base + 2 * r + 1], True).start()


DISP_TM = 256


def _dispatch_kernel(pos_ref, lt_ref, ht_ref, f_ref, xs_hbm, sbuf, zbuf, sem, zsem):
    i = pl.program_id(0)
    n = pl.num_programs(0)
    slot = i % 2

    def fill(e):
        return pltpu.make_async_copy(zbuf, xs_hbm.at[pl.ds(lt_ref[e] * MOE_TM, MOE_TM), :], zsem)

    @pl.when(i == 0)
    def _():
        zbuf[...] = jnp.zeros_like(zbuf)
        for e in range(2 * N_EXPERTS):
            @pl.when(ht_ref[e] != 0)
            def _():
                fill(e).start()
        for e in range(2 * N_EXPERTS):
            @pl.when(ht_ref[e] != 0)
            def _():
                fill(e).wait()

    def wait_scatter(sl):
        for _ in range(2):
            pltpu.make_async_copy(sbuf.at[sl], xs_hbm.at[pl.ds(0, DISP_TM), :], sem.at[sl]).wait()

    @pl.when(i >= 2)
    def _():
        wait_scatter(slot)

    sbuf[slot] = f_ref[...]

    def make(r, dst, second):
        return pltpu.make_async_copy(sbuf.at[slot, pl.ds(r, 1), :], xs_hbm.at[pl.ds(dst, 1), :], sem.at[slot])

    _start_row_copies(pos_ref, i, DISP_TM, make)

    @pl.when(i == n - 1)
    def _():
        wait_scatter(slot)

        @pl.when(n >= 2)
        def _():
            wait_scatter(1 - slot)


def _dispatch(f, pos, last_tile, has_tile, cap):
    t = f.shape[0]
    grid_spec = pltpu.PrefetchScalarGridSpec(
        num_scalar_prefetch=3,
        grid=(t // DISP_TM,),
        in_specs=[pl.BlockSpec((DISP_TM, D_PACK), lambda i, *_: (i, 0))],
        out_specs=pl.BlockSpec(memory_space=pl.ANY),
        scratch_shapes=[
            pltpu.VMEM((2, DISP_TM, D_PACK), jnp.uint32),
            pltpu.VMEM((MOE_TM, D_PACK), jnp.uint32),
            pltpu.SemaphoreType.DMA((2,)),
            pltpu.SemaphoreType.DMA,
        ],
    )
    return pl.pallas_call(
        _dispatch_kernel,
        out_shape=jax.ShapeDtypeStruct((cap, D_PACK), jnp.uint32),
        grid_spec=grid_spec,
        compiler_params=_cparams(1),
        name="moe_dispatch",
    )(pos, last_tile, has_tile, f)


def _moe_kernel(te_ref, tv_ref, x_ref, w1_ref, w3_ref, w2_ref, y_ref):
    i = pl.program_id(0)

    @pl.when(tv_ref[i] != 0)
    def _():
        x = _unpack_rows(x_ref[...]).astype(BF16)
        h1 = _dot(x, w1_ref[...])
        h3 = _dot(x, w3_ref[...])
        a = (h1 * jax.nn.sigmoid(h1) * h3).astype(BF16)
        y_ref[...] = _pack_rows(_dot(a, w2_ref[...]))

    @pl.when(tv_ref[i] == 0)
    def _():
        y_ref[...] = jnp.zeros_like(y_ref)


def _moe_experts(xs, tile_expert, tile_valid, w1, w3, w2):
    cap = xs.shape[0]
    grid_spec = pltpu.PrefetchScalarGridSpec(
        num_scalar_prefetch=2,
        grid=(cap // MOE_TM,),
        in_specs=[
            pl.BlockSpec((MOE_TM, D_PACK), lambda i, te, tv: (i * tv[i], 0)),
            pl.BlockSpec((D, D_EXPERT), lambda i, te, tv: (te[i], 0)),
            pl.BlockSpec((D, D_EXPERT), lambda i, te, tv: (te[i], 0)),
            pl.BlockSpec((D_EXPERT, D), lambda i, te, tv: (te[i], 0)),
        ],
        out_specs=pl.BlockSpec((MOE_TM, D_PACK), lambda i, te, tv: (i, 0)),
    )
    return pl.pallas_call(
        _moe_kernel,
        out_shape=jax.ShapeDtypeStruct((cap, D_PACK), jnp.uint32),
        grid_spec=grid_spec,
        compiler_params=_cparams(1),
        name="moe_experts",
    )(tile_expert, tile_valid, xs, w1, w3, w2)


FIN_TM = 256


def _final_kernel(pos_ref, h_ref, rt_ref, mod_ref, lng_ref, lnb_ref, ys_hbm, o_ref, abuf, bbuf, sem):
    i = pl.program_id(0)
    n = pl.num_programs(0)
    slot = i % 2

    def start_gather(tile, sl):
        def make(r, src, second):
            buf = bbuf if second else abuf
            return pltpu.make_async_copy(ys_hbm.at[pl.ds(src, 1), :], buf.at[sl, pl.ds(r, 1), :], sem.at[sl])

        _start_row_copies(pos_ref, tile, FIN_TM, make)

    @pl.when(i == 0)
    def _():
        start_gather(0, 0)

    @pl.when(i + 1 < n)
    def _():
        start_gather(i + 1, 1 - slot)

    for buf in (abuf, bbuf):
        pltpu.make_async_copy(ys_hbm.at[pl.ds(0, FIN_TM), :], buf.at[slot], sem.at[slot]).wait()

    y = (_unpack_rows(abuf[slot]) * rt_ref[:, RT_W:RT_W + 1]
         + _unpack_rows(bbuf[slot]) * rt_ref[:, RT_W + 1:RT_W + 2])
    gate2 = mod_ref[0, 5:6, :]
    o_ref[...] = _layer_norm(ALPHA * h_ref[...] + gate2 * y, lng_ref[...], lnb_ref[...])


def _ffn_residual(h1, route, pos, ys, mod, ln_g, ln_b):
    rows = h1.shape[0]
    grid_spec = pltpu.PrefetchScalarGridSpec(
        num_scalar_prefetch=1,
        grid=(rows // FIN_TM,),
        in_specs=[
            pl.BlockSpec((FIN_TM, D), lambda i, *_: (i, 0)),
            pl.BlockSpec((FIN_TM, LANE), lambda i, *_: (i, 0)),
            pl.BlockSpec((1, 6, D), lambda i, *_: (_mod_row(FIN_TM)(i), 0, 0)),
            pl.BlockSpec((1, D), lambda i, *_: (0, 0)),
            pl.BlockSpec((1, D), lambda i, *_: (0, 0)),
            pl.BlockSpec(memory_space=pl.ANY),
        ],
        out_specs=pl.BlockSpec((FIN_TM, D), lambda i, *_: (i, 0)),
        scratch_shapes=[
            pltpu.VMEM((2, FIN_TM, D_PACK), jnp.uint32),
            pltpu.VMEM((2, FIN_TM, D_PACK), jnp.uint32),
            pltpu.SemaphoreType.DMA((2,)),
        ],
    )
    return pl.pallas_call(
        _final_kernel,
        out_shape=jax.ShapeDtypeStruct((rows, D), F32),
        grid_spec=grid_spec,
        compiler_params=_cparams(1),
        name="ffn_residual_ln",
    )(pos, h1, route, mod, ln_g, ln_b, ys)


def _moe_ffn(h1, f, route, counts, mod, w1, w3, w2, ln_g, ln_b):
    tile_expert, tile_valid, pos, fill_tile, fill_on, n_tiles = _plan(route, counts)
    xs = _dispatch(f, pos, fill_tile, fill_on, n_tiles * MOE_TM)
    ys = _moe_experts(xs, tile_expert, tile_valid, w1, w3, w2)
    return _ffn_residual(h1, route, pos, ys, mod, ln_g, ln_b)


def _router_weights(w_group, b_group, w_expert, b_expert):
    pad = LANE - N_GROUPS - N_EXPERTS
    w = jnp.concatenate([w_group, w_expert, jnp.zeros((D, pad), F32)], axis=1)
    b = jnp.concatenate([b_group, b_expert, jnp.zeros((pad,), F32)])[None, :]
    return w, b


def kernel(x, c, ctx, c_ctx, ada_w, ada_b, ln_g, ln_b, gla_w_in, gla_w_gate, gla_b_gate, gla_norm_g, gla_w_out,
           nat_w_in, nat_rpb, nat_w_out, moe_w_group, moe_b_group, moe_w_expert, moe_b_expert,
           moe_w1, moe_w3, moe_w2):
    stream = (x.reshape(R_LAT, D), ctx.reshape(R_CTX, D), 0)
    cc = jnp.concatenate([c, c_ctx[None, :], jnp.zeros((SUBLANE - B - 1, D), F32)], axis=0)
    mod = _ada_table(cc, ada_w, ada_b).reshape(DEPTH, SUBLANE, 6, D)
    w1_all = moe_w1.reshape(DEPTH * N_EXPERTS * D, D_EXPERT)
    w3_all = moe_w3.reshape(DEPTH * N_EXPERTS * D, D_EXPERT)
    w2_all = moe_w2.reshape(DEPTH * N_EXPERTS * D_EXPERT, D)

    def expert_casts(layer):
        up, down = N_EXPERTS * D, N_EXPERTS * D_EXPERT
        return ((w1_all, up, layer * up), (w3_all, up, layer * up), (w2_all, down, layer * down))

    w_in = gla_w_in[0]
    w_main = jnp.concatenate([w_in[:, :GLA_QK] * (GLA_DK ** -0.5), w_in[:, GLA_QK:GLA_MAIN]], axis=1).astype(BF16)
    w_glr = jnp.concatenate([w_in[:, GLA_MAIN:], jnp.zeros((D, LANE - 2 * GLA_RANK), F32)], axis=1).astype(BF16)
    p, glr = _proj(stream, mod[0], w_main, w_glr)
    wg = jnp.zeros((2, LANE, GLA_QK), F32)
    wg = wg.at[0, :GLA_RANK].set(gla_w_gate[0, 0]).at[1, GLA_RANK:2 * GLA_RANK].set(gla_w_gate[0, 1])
    bg = gla_b_gate[0][:, None, :]
    cos, sin = _rope_tables()
    o_f, *experts0 = _gla_scan(p, glr, wg, bg, cos, sin, False, expert_casts(0))
    o_b, *experts1 = _gla_scan(p, glr, wg, bg, cos, sin, True, expert_casts(1))
    w_r, b_r = _router_weights(moe_w_group[0], moe_b_group[0], moe_w_expert[0], moe_b_expert[0])
    h1, f, route, counts = _mix_out((o_f, o_b, p, gla_norm_g[0][None, :]), stream, mod[0], gla_w_out[0].astype(BF16),
                                    ln_g[0, 0][None, :], ln_b[0, 0][None, :], w_r, b_r, R_ALL, gla=True)
    h = _moe_ffn(h1, f, route, counts, mod[0], *experts0, ln_g[0, 1][None, :], ln_b[0, 1][None, :])

    w_in = nat_w_in[0]
    w_nat = jnp.concatenate([w_in[:, :D] * (NAT_DH ** -0.5), w_in[:, D:]], axis=1).astype(BF16)
    stream = (h, h, R_LAT)
    p = _proj(stream, mod[1], w_nat)
    o = _nat_attention(p, _nat_bias_table(nat_rpb[0]))
    w_r, b_r = _router_weights(moe_w_group[1], moe_b_group[1], moe_w_expert[1], moe_b_expert[1])
    h1, f, route, counts = _mix_out((o,), stream, mod[1], nat_w_out[0].astype(BF16),
                                    ln_g[1, 0][None, :], ln_b[1, 0][None, :], w_r, b_r, R_LAT, gla=False)
    out = _moe_ffn(h1, f, route, counts, mod[1], *experts1, ln_g[1, 1][None, :], ln_b[1, 1][None, :])
    return out.reshape(B, SEQ, D)
```

```python
import functools

import jax
import jax.numpy as jnp
import numpy as np
from jax import lax
from jax.experimental import pallas as pl
from jax.experimental.pallas import tpu as pltpu

F32 = jnp.float32
BF16 = jnp.bfloat16

D = 2048
B = 4
SEQ = 4096
DEPTH = 2
GRID_W = 64
CTX = 256
R_LAT = B * SEQ
R_CTX = B * CTX
R_ALL = R_LAT + R_CTX

GLA_H = 4
GLA_DK = 256
GLA_DV = 512
GLA_CHUNK = 64
GLA_RANK = 16
GLA_QK = GLA_H * GLA_DK
GLA_VD = GLA_H * GLA_DV
GLA_MAIN = 2 * GLA_QK + 2 * GLA_VD
GLA_GATE_NORM = 16.0
ROPE_BASE = 10000.0

NAT_H = 16
NAT_DH = 128
WIN_R = 8
WIN_C = 16
NAT_QR = 4
NAT_BAND = 12
NAT_G = 8

N_GROUPS = 4
EPG = 4
N_EXPERTS = 16
D_EXPERT = 1024
MOE_TM = 256

ALPHA = (2 * DEPTH) ** 0.25
LN_EPS = 1e-5
NEG_INF = -1e30

LANE = 128
SUBLANE = 8
VMEM_LIMIT = 56 * 1024 * 1024


def _cparams(n_axes):
    return pltpu.CompilerParams(dimension_semantics=("arbitrary",) * n_axes,
                                vmem_limit_bytes=VMEM_LIMIT)


def _split_bf16(x):
    hi = x.astype(BF16)
    lo = (x - hi.astype(F32)).astype(BF16)
    return hi, lo


def _dot(a, b):
    return jnp.dot(a, b, preferred_element_type=F32)


def _dot_hi(a, b):
    ah, al = _split_bf16(a)
    bh, bl = _split_bf16(b)
    return _dot(ah, bh) + _dot(al, bh) + _dot(ah, bl)


def _dot_nt(a, b):
    return lax.dot_general(a, b, (((1,), (1,)), ((), ())), preferred_element_type=F32)


def _dot_tn(a, b):
    return lax.dot_general(a, b, (((0,), (0,)), ((), ())), preferred_element_type=F32)


D_PACK = D // 2


def _pack_rows(x):
    lo = pltpu.bitcast(x[:, :D_PACK].astype(BF16).astype(F32), jnp.uint32)
    hi = pltpu.bitcast(x[:, D_PACK:].astype(BF16).astype(F32), jnp.uint32)
    return (lo >> 16) | hi


def _unpack_rows(u):
    lo = pltpu.bitcast(u << 16, F32)
    hi = pltpu.bitcast(u & jnp.uint32(0xFFFF0000), F32)
    return jnp.concatenate([lo, hi], axis=1)


CAST_STEPS = 64


def _cast_specs(casts, step_of):
    in_specs, out_shape, out_specs, args = [], [], [], []
    for arr, n_rows, first_row in casts:
        c_rows = n_rows // CAST_STEPS
        first = first_row // c_rows
        chunk = lambda *idx: jnp.minimum(step_of(*idx), CAST_STEPS - 1)
        in_specs.append(pl.BlockSpec((c_rows, arr.shape[1]), lambda *idx, first=first: (first + chunk(*idx), 0)))
        out_shape.append(jax.ShapeDtypeStruct((n_rows, arr.shape[1]), BF16))
        out_specs.append(pl.BlockSpec((c_rows, arr.shape[1]), lambda *idx: (chunk(*idx), 0)))
        args.append(arr)
    return in_specs, out_shape, out_specs, args


def _mod_row(tile_rows):
    return lambda i: jnp.minimum((i * tile_rows) // SEQ, B)


def _stream_specs(stream, tile_rows, lag=0):
    n_lat = R_LAT // tile_rows
    base = stream[2] // tile_rows
    return (pl.BlockSpec((tile_rows, D), lambda i, *_: (jnp.clip(i - lag, 0, n_lat - 1), 0)),
            pl.BlockSpec((tile_rows, D), lambda i, *_: (base + jnp.maximum(i - lag - n_lat, 0), 0)))


ADA_TN = 1024


def _ada_kernel(c_ref, w_ref, b_ref, o_ref):
    x = c_ref[...]
    s = x * jax.nn.sigmoid(x)
    o_ref[0] = _dot_hi(s, w_ref[0]) + b_ref[0]


def _ada_table(cc, ada_w, ada_b):
    n = 6 * D
    return pl.pallas_call(
        _ada_kernel,
        out_shape=jax.ShapeDtypeStruct((DEPTH, SUBLANE, n), F32),
        grid=(DEPTH, n // ADA_TN),
        in_specs=[
            pl.BlockSpec((SUBLANE, D), lambda l, j: (0, 0)),
            pl.BlockSpec((1, D, ADA_TN), lambda l, j: (l, 0, j)),
            pl.BlockSpec((1, 1, ADA_TN), lambda l, j: (l, 0, j)),
        ],
        out_specs=pl.BlockSpec((1, SUBLANE, ADA_TN), lambda l, j: (l, 0, j)),
        compiler_params=_cparams(2),
        name="ada_table",
    )(cc, ada_w, ada_b.reshape(DEPTH, 1, n))


PROJ_TM = 1024
PROJ_TN = 1024


def _proj_kernel(h_ref, hc_ref, mod_ref, w_ref, *rest, with_extra):
    if with_extra:
        wx_ref, o_ref, ox_ref, a_scr = rest
    else:
        o_ref, a_scr = rest
    j = pl.program_id(1)

    @pl.when(j == 0)
    def _():
        sh = mod_ref[0, 0:1, :]
        sc = mod_ref[0, 1:2, :]
        h = jnp.where(pl.program_id(0) < R_LAT // PROJ_TM, h_ref[...], hc_ref[...])
        a_scr[...] = (h * (1.0 + sc) + sh).astype(BF16)
        if with_extra:
            ox_ref[...] = _dot(a_scr[...], wx_ref[...])

    o_ref[...] = _dot(a_scr[...], w_ref[...]).astype(o_ref.dtype)


def _proj(stream, mod, w, w_extra=None):
    rows = R_ALL
    n = w.shape[1]
    with_extra = w_extra is not None
    in_specs = [
        *_stream_specs(stream, PROJ_TM),
        pl.BlockSpec((1, 6, D), lambda i, j: (_mod_row(PROJ_TM)(i), 0, 0)),
        pl.BlockSpec((D, PROJ_TN), lambda i, j: (0, j)),
    ]
    out_shape = [jax.ShapeDtypeStruct((rows, n), BF16)]
    out_specs = [pl.BlockSpec((PROJ_TM, PROJ_TN), lambda i, j: (i, j))]
    args = [stream[0], stream[1], mod, w]
    if with_extra:
        in_specs.append(pl.BlockSpec((D, LANE), lambda i, j: (0, 0)))
        out_shape.append(jax.ShapeDtypeStruct((rows, LANE), F32))
        out_specs.append(pl.BlockSpec((PROJ_TM, LANE), lambda i, j: (i, 0)))
        args.append(w_extra)
    out = pl.pallas_call(
        functools.partial(_proj_kernel, with_extra=with_extra),
        out_shape=out_shape,
        grid=(rows // PROJ_TM, n // PROJ_TN),
        in_specs=in_specs,
        out_specs=out_specs,
        scratch_shapes=[pltpu.VMEM((PROJ_TM, D), BF16)],
        compiler_params=_cparams(2),
        name="mod_proj",
    )(*args)
    return out if with_extra else out[0]


GLA_TB = 256
GLA_STEPS = 1 + SEQ // GLA_TB


def _rope_tables():
    half = GLA_DK // 4
    freqs = ROPE_BASE ** (-np.arange(half, dtype=np.float64) / half)
    t = np.arange(SEQ)
    ang_r = (t // GRID_W)[:, None] * freqs
    ang_c = (t % GRID_W)[:, None] * freqs
    cos = np.concatenate([np.cos(ang_r)] * 2 + [np.cos(ang_c)] * 2, axis=1)
    sin = np.concatenate([-np.sin(ang_r), np.sin(ang_r), -np.sin(ang_c), np.sin(ang_c)], axis=1)
    cos = np.concatenate([cos, np.ones((GLA_TB, GLA_DK))], axis=0)
    sin = np.concatenate([sin, np.zeros((GLA_TB, GLA_DK))], axis=0)
    return jnp.asarray(cos, F32), jnp.asarray(sin, F32)


def _rope(x, cos, sin):
    half = LANE // 2
    parts = [pltpu.roll(x[:, g * LANE:(g + 1) * LANE], half, 1) for g in range(x.shape[1] // LANE)]
    return x * cos + jnp.concatenate(parts, axis=1) * sin


def _gla_kernel(q_ref, k_ref, v_ref, glr_ref, wg_ref, bg_ref, cos_ref, sin_ref, *rest, reverse, n_cast):
    cast_in, o_ref, cast_out, s_scr = rest[:n_cast], rest[n_cast], rest[n_cast + 1:-1], rest[-1]
    for src, dst in zip(cast_in, cast_out):
        dst[...] = src[...].astype(BF16)

    @pl.when(pl.program_id(1) == 0)
    def _():
        s_scr[...] = jnp.zeros_like(s_scr)

    row = lax.broadcasted_iota(jnp.int32, (GLA_CHUNK, GLA_CHUNK), 0)
    col = lax.broadcasted_iota(jnp.int32, (GLA_CHUNK, GLA_CHUNK), 1)
    keep = (col >= row) if reverse else (col <= row)
    tri = jnp.where(keep, 1.0, 0.0).astype(BF16)
    ones = jnp.ones((GLA_CHUNK, LANE), BF16)
    n_chunks = GLA_TB // GLA_CHUNK
    order = range(n_chunks - 1, -1, -1) if reverse else range(n_chunks)
    for ci in order:
        sl = slice(ci * GLA_CHUNK, (ci + 1) * GLA_CHUNK)
        cos = cos_ref[sl, :]
        sin = sin_ref[sl, :]
        z = _dot_hi(glr_ref[sl, :], wg_ref[0]) + bg_ref[0]
        g = (jnp.minimum(z, 0.0) - jnp.log1p(jnp.exp(-jnp.abs(z)))) * (1.0 / GLA_GATE_NORM)
        g_hi, g_lo = _split_bf16(g)
        cum_all = _dot(tri, g_hi) + _dot(tri, g_lo)
        decay_all = jnp.exp(_dot_tn(g_hi, ones) + _dot_tn(g_lo, ones))
        for hd in range(GLA_H):
            ks = slice(hd * GLA_DK, (hd + 1) * GLA_DK)
            vs = slice(hd * GLA_DV, (hd + 1) * GLA_DV)
            q = _rope(q_ref[sl, ks].astype(F32), cos, sin)
            k = _rope(k_ref[sl, ks].astype(F32), cos, sin)
            v = v_ref[sl, vs]
            cum = cum_all[:, ks]
            tot = cum[0:1, :] if reverse else cum[GLA_CHUNK - 1:GLA_CHUNK, :]
            q_dec = (q * jnp.exp(cum)).astype(BF16)
            k_inv = (k * jnp.exp(-cum)).astype(BF16)
            k_end = (k * jnp.exp(tot - cum)).astype(BF16)
            att = jnp.where(keep, _dot_nt(q_dec, k_inv), 0.0).astype(BF16)
            state = s_scr[hd]
            o_ref[sl, vs] = (_dot(att, v) + _dot(q_dec, state.astype(BF16))).astype(o_ref.dtype)
            decay = jnp.concatenate([decay_all[ks, :]] * (GLA_DV // LANE), axis=1)
            s_scr[hd] = state * decay + _dot_tn(k_end, v)


def _gla_scan(p, glr, wg, bg, cos, sin, reverse, casts):
    lat_blocks = SEQ // GLA_TB

    def tblock(s):
        if reverse:
            return jnp.where(s == 0, 0, GLA_STEPS - s)
        return s

    def rows(b, s):
        tb = tblock(s)
        return jnp.where(tb == 0, R_LAT // GLA_TB + b, b * lat_blocks + tb - 1)

    def rope_rows(s):
        tb = tblock(s)
        return jnp.where(tb == 0, lat_blocks, tb - 1)

    d = 1 if reverse else 0
    c_in, c_shape, c_out, c_args = _cast_specs(casts, lambda b, s: b * GLA_STEPS + s)
    return pl.pallas_call(
        functools.partial(_gla_kernel, reverse=reverse, n_cast=len(casts)),
        out_shape=[jax.ShapeDtypeStruct((R_ALL, GLA_VD), BF16), *c_shape],
        grid=(B, GLA_STEPS),
        in_specs=[
            pl.BlockSpec((GLA_TB, GLA_QK), lambda b, s: (rows(b, s), 0)),
            pl.BlockSpec((GLA_TB, GLA_QK), lambda b, s: (rows(b, s), 1)),
            pl.BlockSpec((GLA_TB, GLA_VD), lambda b, s: (rows(b, s), 2 * GLA_QK // GLA_VD)),
            pl.BlockSpec((GLA_TB, LANE), lambda b, s: (rows(b, s), 0)),
            pl.BlockSpec((1, LANE, GLA_QK), lambda b, s: (d, 0, 0)),
            pl.BlockSpec((1, 1, GLA_QK), lambda b, s: (d, 0, 0)),
            pl.BlockSpec((GLA_TB, GLA_DK), lambda b, s: (rope_rows(s), 0)),
            pl.BlockSpec((GLA_TB, GLA_DK), lambda b, s: (rope_rows(s), 0)),
            *c_in,
        ],
        out_specs=[pl.BlockSpec((GLA_TB, GLA_VD), lambda b, s: (rows(b, s), 0)), *c_out],
        scratch_shapes=[pltpu.VMEM((GLA_H, GLA_DK, GLA_DV), F32)],
        compiler_params=_cparams(2),
        name="gla_scan_bwd" if reverse else "gla_scan_fwd",
    )(p, p, p, glr, wg, bg, cos, sin, *c_args)


def _nat_bias_table(rpb):
    qc = np.arange(GRID_W)[:, None]
    kc = np.arange(GRID_W)[None, :]
    cs = np.clip(qc - WIN_C // 2, 0, GRID_W - WIN_C)
    col_ok = (kc >= cs) & (kc < cs + WIN_C)
    dc = np.clip(kc - qc + WIN_C - 1, 0, 2 * WIN_C - 2)
    pick = (dc[:, :, None] == np.arange(2 * WIN_C - 1)).astype(np.float32)
    tc = jnp.einsum('had,qkd->hqak', rpb.astype(F32), jnp.asarray(pick), precision=lax.Precision.HIGHEST)
    tc = jnp.where(col_ok[None, :, None, :], tc, NEG_INF)
    n_dr = 2 * WIN_R - 1
    pad = NAT_BAND - 1
    strip = jnp.pad(tc.reshape(NAT_H, GRID_W, n_dr * GRID_W), ((0, 0), (0, 0), (pad * GRID_W, pad * GRID_W)),
                    constant_values=NEG_INF)
    rows_n = SEQ // GRID_W
    half = WIN_R // 2
    n_keys = NAT_BAND * GRID_W
    bands, valid = [], np.zeros((3, NAT_QR, 1, NAT_BAND, 1), bool)
    for p_i, (r0, bs) in enumerate(((0, 0), (NAT_QR, NAT_QR - half), (rows_n - NAT_QR, rows_n - NAT_BAND))):
        for i in range(NAT_QR):
            qr = r0 + i
            rs = min(max(qr - half, 0), rows_n - WIN_R)
            first = (pad + bs - qr + WIN_R - 1) * GRID_W
            bands.append(strip[:, :, first:first + n_keys])
            valid[p_i, i, 0, :, 0] = [rs <= kr < rs + WIN_R for kr in range(bs, bs + NAT_BAND)]
    tab = jnp.stack(bands).reshape(3, NAT_QR, NAT_H, GRID_W, n_keys).transpose(0, 2, 1, 3, 4)
    valid = np.broadcast_to(valid, (3, NAT_QR, GRID_W, NAT_BAND, GRID_W)).reshape(3, 1, NAT_QR * GRID_W, n_keys)
    return jnp.where(valid, tab.reshape(3, NAT_H, NAT_QR * GRID_W, n_keys), NEG_INF)


def _nat_kernel(q_ref, k_ref, v_ref, kc_ref, vc_ref, bias_ref, o_ref):
    rb = pl.program_id(2)
    rows_n = SEQ // GRID_W
    band_start = jnp.clip(rb * NAT_QR - WIN_R // 2, 0, rows_n - NAT_BAND)
    start = pl.multiple_of(band_start * GRID_W, GRID_W)
    n_keys = NAT_BAND * GRID_W
    for g in range(NAT_G):
        hs = slice(g * NAT_DH, (g + 1) * NAT_DH)
        q = q_ref[:, hs]
        s_loc = _dot_nt(q, k_ref[pl.ds(start, n_keys), hs]) + bias_ref[0, g]
        s_ctx = _dot_nt(q, kc_ref[:, hs])
        m = jnp.maximum(jnp.max(s_loc, axis=1, keepdims=True), jnp.max(s_ctx, axis=1, keepdims=True))
        p_loc = jnp.exp(s_loc - m)
        p_ctx = jnp.exp(s_ctx - m)
        denom = jnp.sum(p_loc, axis=1, keepdims=True) + jnp.sum(p_ctx, axis=1, keepdims=True)
        o = _dot(p_loc.astype(BF16), v_ref[pl.ds(start, n_keys), hs]) + _dot(p_ctx.astype(BF16), vc_ref[:, hs])
        o_ref[:, hs] = (o / denom).astype(o_ref.dtype)


def _nat_attention(p, bias_tab):
    n_rb = SEQ // (NAT_QR * GRID_W)
    tq = NAT_QR * GRID_W
    width = NAT_G * NAT_DH
    hq = D // width

    def pattern(rb):
        return jnp.where(rb == 0, 0, jnp.where(rb == n_rb - 1, 2, 1))

    return pl.pallas_call(
        _nat_kernel,
        out_shape=jax.ShapeDtypeStruct((R_LAT, D), BF16),
        grid=(B, NAT_H // NAT_G, n_rb),
        in_specs=[
            pl.BlockSpec((tq, width), lambda b, h, rb: (b * n_rb + rb, h)),
            pl.BlockSpec((SEQ, width), lambda b, h, rb: (b, hq + h)),
            pl.BlockSpec((SEQ, width), lambda b, h, rb: (b, 2 * hq + h)),
            pl.BlockSpec((CTX, width), lambda b, h, rb: (R_LAT // CTX + b, hq + h)),
            pl.BlockSpec((CTX, width), lambda b, h, rb: (R_LAT // CTX + b, 2 * hq + h)),
            pl.BlockSpec((1, NAT_G, tq, NAT_BAND * GRID_W), lambda b, h, rb: (pattern(rb), h, 0, 0)),
        ],
        out_specs=pl.BlockSpec((tq, width), lambda b, h, rb: (b * n_rb + rb, h)),
        compiler_params=_cparams(3),
        name="nat_attention",
    )(p, p, p, p, p, bias_tab)


MIX_TM = 256


def _layer_norm(u, g, b):
    mu = jnp.mean(u, axis=-1, keepdims=True)
    var = jnp.mean(jnp.square(u - mu), axis=-1, keepdims=True)
    return (u - mu) * lax.rsqrt(var + LN_EPS) * g + b


def _mix_out_kernel(*refs, gla, lat_tiles):
    if gla:
        of_ref, ob_ref, r_ref, ng_ref = refs[:4]
        refs = refs[4:]
    else:
        x_ref = refs[0]
        refs = refs[1:]
    (h_ref, hc_ref, mod_ref, w_ref, lng_ref, lnb_ref, wr_ref, br_ref,
     h1_ref, f_ref, rt_ref, cnt_ref, y_even, y_odd, cnt_scr) = refs
    s = pl.program_id(0)

    @pl.when(s == 0)
    def _():
        cnt_scr[...] = jnp.zeros_like(cnt_scr)
        y_odd[...] = jnp.zeros_like(y_odd)

    def step(y_new, y_prev):
        if gla:
            parts = []
            for hd in range(GLA_H):
                cs = slice(hd * GLA_DV, (hd + 1) * GLA_DV)
                o = of_ref[:, cs].astype(F32) + ob_ref[:, cs].astype(F32)
                o = o * lax.rsqrt(jnp.mean(jnp.square(o), axis=-1, keepdims=True) + LN_EPS) * ng_ref[...]
                r = r_ref[:, cs].astype(F32)
                parts.append((o * (r * jax.nn.sigmoid(r))).astype(BF16))
            x = jnp.concatenate(parts, axis=1)
        else:
            x = x_ref[...]
        y_new[...] = _dot(x, w_ref[...])

        gate1 = mod_ref[0, 2:3, :]
        h_in = jnp.where(s - 1 < lat_tiles, h_ref[...], hc_ref[...])
        h1 = _layer_norm(ALPHA * h_in + gate1 * y_prev[...], lng_ref[...], lnb_ref[...])
        h1_ref[...] = h1
        f = h1 * (1.0 + mod_ref[0, 4:5, :]) + mod_ref[0, 3:4, :]
        f_ref[...] = _pack_rows(f)
        live = jnp.where(s > 0, 1.0, 0.0)
        w_hi, w_lo = _split_bf16(wr_ref[...])
        f_bf = f.astype(BF16)
        logits = _dot(f_bf, w_hi) + _dot(f_bf, w_lo) + br_ref[...]
        route, counts = _route_tile(logits, cnt_scr[...], live)
        rt_ref[...] = route
        cnt_scr[...] = counts
        cnt_ref[...] = jnp.broadcast_to(counts, cnt_ref.shape)

    @pl.when(s % 2 == 0)
    def _():
        step(y_even, y_odd)

    @pl.when(s % 2 == 1)
    def _():
        step(y_odd, y_even)


RT_E, RT_W, RT_RANK = 0, 2, 4


def _route_tile(logits, run, live):
    tm = logits.shape[0]
    lane = lax.broadcasted_iota(jnp.int32, (tm, LANE), 1).astype(F32)

    def first_max(vals):
        top = jnp.max(vals, axis=1, keepdims=True)
        return top, jnp.min(jnp.where(vals == top, lane, float(LANE)), axis=1, keepdims=True)

    gl = jnp.where(lane < N_GROUPS, logits, NEG_INF)
    g_max, g_idx = first_max(gl)
    g_p = 1.0 / jnp.sum(jnp.exp(gl - g_max), axis=1, keepdims=True)
    lo = N_GROUPS + EPG * g_idx
    el = jnp.where((lane >= lo) & (lane < lo + EPG), logits, NEG_INF)
    e1, i1 = first_max(el)
    e2, i2 = first_max(jnp.where(lane == i1, NEG_INF, el))
    t = jnp.exp(e2 - e1)
    p1 = 1.0 / (1.0 + t)
    x1 = i1 - N_GROUPS
    x2 = i2 - N_GROUPS
    onehot = jnp.where((lane == x1) | (lane == x2), 1.0, 0.0)
    row = lax.broadcasted_iota(jnp.int32, (tm, tm), 0)
    col = lax.broadcasted_iota(jnp.int32, (tm, tm), 1)
    before = jnp.where(col < row, 1.0, 0.0).astype(BF16)
    seen = _dot(before, onehot.astype(BF16)) + run
    rank1 = jnp.sum(jnp.where(lane == x1, seen, 0.0), axis=1, keepdims=True)
    rank2 = jnp.sum(jnp.where(lane == x2, seen, 0.0), axis=1, keepdims=True)
    route = jnp.zeros((tm, LANE), F32)
    for k, val in ((RT_E, x1), (RT_E + 1, x2), (RT_W, g_p * p1), (RT_W + 1, g_p * p1 * t),
                   (RT_RANK, rank1), (RT_RANK + 1, rank2)):
        route = jnp.where(lane == k, val, route)
    return route, run + live * jnp.sum(onehot, axis=0, keepdims=True)


def _mix_out(mix_inputs, stream, mod, w_out, ln_g, ln_b, w_router, b_router, rows, gla):
    n_tiles = rows // MIX_TM
    row_spec = lambda width: pl.BlockSpec((MIX_TM, width), lambda i: (jnp.minimum(i, n_tiles - 1), 0))
    done_spec = lambda width: pl.BlockSpec((MIX_TM, width), lambda i: (jnp.maximum(i - 1, 0), 0))
    full = lambda a: pl.BlockSpec(a.shape, lambda i: (0,) * a.ndim)
    if gla:
        o_f, o_b, p, norm_g = mix_inputs
        in_specs = [row_spec(GLA_VD), row_spec(GLA_VD),
                    pl.BlockSpec((MIX_TM, GLA_VD),
                                 lambda i: (jnp.minimum(i, n_tiles - 1), (2 * GLA_QK + GLA_VD) // GLA_VD)),
                    full(norm_g)]
        args = [o_f, o_b, p, norm_g]
    else:
        (x,) = mix_inputs
        in_specs = [row_spec(D)]
        args = [x]
    in_specs += [*_stream_specs(stream, MIX_TM, lag=1),
                 pl.BlockSpec((1, 6, D), lambda i: (_mod_row(MIX_TM)(jnp.maximum(i - 1, 0)), 0, 0)),
                 full(w_out), full(ln_g), full(ln_b), full(w_router), full(b_router)]
    args += [stream[0], stream[1], mod, w_out, ln_g, ln_b, w_router, b_router]
    return pl.pallas_call(
        functools.partial(_mix_out_kernel, gla=gla, lat_tiles=R_LAT // MIX_TM),
        out_shape=[jax.ShapeDtypeStruct((rows, D), F32), jax.ShapeDtypeStruct((rows, D_PACK), jnp.uint32),
                   jax.ShapeDtypeStruct((rows, LANE), F32), jax.ShapeDtypeStruct((SUBLANE, LANE), F32)],
        grid=(n_tiles + 1,),
        in_specs=in_specs,
        out_specs=[done_spec(D), done_spec(D_PACK), done_spec(LANE), pl.BlockSpec((SUBLANE, LANE), lambda i: (0, 0))],
        scratch_shapes=[pltpu.VMEM((MIX_TM, D), F32), pltpu.VMEM((MIX_TM, D), F32), pltpu.VMEM((1, LANE), F32)],
        compiler_params=_cparams(1),
        name="gla_out_ln" if gla else "nat_out_ln",
    )(*args)


def _plan(route, counts_row):
    t = route.shape[0]
    e = route[:, RT_E:RT_E + 2].astype(jnp.int32)
    rank = route[:, RT_RANK:RT_RANK + 2].astype(jnp.int32)
    counts = counts_row[0, :N_EXPERTS].astype(jnp.int32)
    tiles_per = (counts + MOE_TM - 1) // MOE_TM
    tile_end = jnp.cumsum(tiles_per)
    tile_start = tile_end - tiles_per
    n_tiles = 2 * t // MOE_TM + N_EXPERTS
    ids = jnp.arange(n_tiles, dtype=jnp.int32)
    used = tile_end[-1]
    tile_valid = (ids < used).astype(jnp.int32)
    lookup = jnp.minimum(ids, used - 1)
    tile_expert = jnp.sum((tile_end[None, :] <= lookup[:, None]).astype(jnp.int32), axis=1)
    tile_expert = jnp.minimum(tile_expert, N_EXPERTS - 1).astype(jnp.int32)
    base = tile_start * MOE_TM
    experts = jnp.arange(N_EXPERTS, dtype=jnp.int32)
    pos = jnp.sum(jnp.where(e[..., None] == experts, base, 0), axis=-1) + rank
    trailing = n_tiles - 1 - experts
    fill_tile = jnp.concatenate([jnp.maximum(tile_end - 1, 0), trailing]).astype(jnp.int32)
    fill_on = jnp.concatenate([tiles_per > 0, trailing >= used]).astype(jnp.int32)
    return tile_expert, tile_valid, pos.reshape(-1).astype(jnp.int32), fill_tile, fill_on, n_tiles


def _start_row_copies(pos_ref, tile, rows, make):
    base = 2 * tile * rows
    for r in range(rows):
        make(r, pos_ref[base + 2 * r], False).start()
        make(r, pos_ref[base + 2 * r + 1], True).start()


DISP_TM = 256


def _dispatch_kernel(pos_ref, lt_ref, ht_ref, f_ref, xs_hbm, sbuf, zbuf, sem, zsem):
    i = pl.program_id(0)
    n = pl.num_programs(0)
    slot = i % 2

    def fill(e):
        return pltpu.make_async_copy(zbuf, xs_hbm.at[pl.ds(lt_ref[e] * MOE_TM, MOE_TM), :], zsem)

    @pl.when(i == 0)
    def _():
        zbuf[...] = jnp.zeros_like(zbuf)
        for e in range(2 * N_EXPERTS):
            @pl.when(ht_ref[e] != 0)
            def _():
                fill(e).start()
        for e in range(2 * N_EXPERTS):
            @pl.when(ht_ref[e] != 0)
            def _():
                fill(e).wait()

    def wait_scatter(sl):
        for _ in range(2):
            pltpu.make_async_copy(sbuf.at[sl], xs_hbm.at[pl.ds(0, DISP_TM), :], sem.at[sl]).wait()

    @pl.when(i >= 2)
    def _():
        wait_scatter(slot)

    sbuf[slot] = f_ref[...]

    def make(r, dst, second):
        return pltpu.make_async_copy(sbuf.at[slot, pl.ds(r, 1), :], xs_hbm.at[pl.ds(dst, 1), :], sem.at[slot])

    _start_row_copies(pos_ref, i, DISP_TM, make)

    @pl.when(i == n - 1)
    def _():
        wait_scatter(slot)

        @pl.when(n >= 2)
        def _():
            wait_scatter(1 - slot)


def _dispatch(f, pos, last_tile, has_tile, cap):
    t = f.shape[0]
    grid_spec = pltpu.PrefetchScalarGridSpec(
        num_scalar_prefetch=3,
        grid=(t // DISP_TM,),
        in_specs=[pl.BlockSpec((DISP_TM, D_PACK), lambda i, *_: (i, 0))],
        out_specs=pl.BlockSpec(memory_space=pl.ANY),
        scratch_shapes=[
            pltpu.VMEM((2, DISP_TM, D_PACK), jnp.uint32),
            pltpu.VMEM((MOE_TM, D_PACK), jnp.uint32),
            pltpu.SemaphoreType.DMA((2,)),
            pltpu.SemaphoreType.DMA,
        ],
    )
    return pl.pallas_call(
        _dispatch_kernel,
        out_shape=jax.ShapeDtypeStruct((cap, D_PACK), jnp.uint32),
        grid_spec=grid_spec,
        compiler_params=_cparams(1),
        name="moe_dispatch",
    )(pos, last_tile, has_tile, f)


def _moe_kernel(te_ref, tv_ref, x_ref, w1_ref, w3_ref, w2_ref, y_ref):
    i = pl.program_id(0)

    @pl.when(tv_ref[i] != 0)
    def _():
        x = _unpack_rows(x_ref[...]).astype(BF16)
        h1 = _dot(x, w1_ref[...])
        h3 = _dot(x, w3_ref[...])
        a = (h1 * jax.nn.sigmoid(h1) * h3).astype(BF16)
        y_ref[...] = _pack_rows(_dot(a, w2_ref[...]))

    @pl.when(tv_ref[i] == 0)
    def _():
        y_ref[...] = jnp.zeros_like(y_ref)


def _moe_experts(xs, tile_expert, tile_valid, w1, w3, w2):
    cap = xs.shape[0]
    grid_spec = pltpu.PrefetchScalarGridSpec(
        num_scalar_prefetch=2,
        grid=(cap // MOE_TM,),
        in_specs=[
            pl.BlockSpec((MOE_TM, D_PACK), lambda i, te, tv: (i * tv[i], 0)),
            pl.BlockSpec((D, D_EXPERT), lambda i, te, tv: (te[i], 0)),
            pl.BlockSpec((D, D_EXPERT), lambda i, te, tv: (te[i], 0)),
            pl.BlockSpec((D_EXPERT, D), lambda i, te, tv: (te[i], 0)),
        ],
        out_specs=pl.BlockSpec((MOE_TM, D_PACK), lambda i, te, tv: (i, 0)),
    )
    return pl.pallas_call(
        _moe_kernel,
        out_shape=jax.ShapeDtypeStruct((cap, D_PACK), jnp.uint32),
        grid_spec=grid_spec,
        compiler_params=_cparams(1),
        name="moe_experts",
    )(tile_expert, tile_valid, xs, w1, w3, w2)


FIN_TM = 256


def _final_kernel(pos_ref, h_ref, rt_ref, mod_ref, lng_ref, lnb_ref, ys_hbm, o_ref, abuf, bbuf, sem):
    i = pl.program_id(0)
    n = pl.num_programs(0)
    slot = i % 2

    def start_gather(tile, sl):
        def make(r, src, second):
            buf = bbuf if second else abuf
            return pltpu.make_async_copy(ys_hbm.at[pl.ds(src, 1), :], buf.at[sl, pl.ds(r, 1), :], sem.at[sl])

        _start_row_copies(pos_ref, tile, FIN_TM, make)

    @pl.when(i == 0)
    def _():
        start_gather(0, 0)

    @pl.when(i + 1 < n)
    def _():
        start_gather(i + 1, 1 - slot)

    for buf in (abuf, bbuf):
        pltpu.make_async_copy(ys_hbm.at[pl.ds(0, FIN_TM), :], buf.at[slot], sem.at[slot]).wait()

    y = (_unpack_rows(abuf[slot]) * rt_ref[:, RT_W:RT_W + 1]
         + _unpack_rows(bbuf[slot]) * rt_ref[:, RT_W + 1:RT_W + 2])
    gate2 = mod_ref[0, 5:6, :]
    o_ref[...] = _layer_norm(ALPHA * h_ref[...] + gate2 * y, lng_ref[...], lnb_ref[...])


def _ffn_residual(h1, route, pos, ys, mod, ln_g, ln_b):
    rows = h1.shape[0]
    grid_spec = pltpu.PrefetchScalarGridSpec(
        num_scalar_prefetch=1,
        grid=(rows // FIN_TM,),
        in_specs=[
            pl.BlockSpec((FIN_TM, D), lambda i, *_: (i, 0)),
            pl.BlockSpec((FIN_TM, LANE), lambda i, *_: (i, 0)),
            pl.BlockSpec((1, 6, D), lambda i, *_: (_mod_row(FIN_TM)(i), 0, 0)),
            pl.BlockSpec((1, D), lambda i, *_: (0, 0)),
            pl.BlockSpec((1, D), lambda i, *_: (0, 0)),
            pl.BlockSpec(memory_space=pl.ANY),
        ],
        out_specs=pl.BlockSpec((FIN_TM, D), lambda i, *_: (i, 0)),
        scratch_shapes=[
            pltpu.VMEM((2, FIN_TM, D_PACK), jnp.uint32),
            pltpu.VMEM((2, FIN_TM, D_PACK), jnp.uint32),
            pltpu.SemaphoreType.DMA((2,)),
        ],
    )
    return pl.pallas_call(
        _final_kernel,
        out_shape=jax.ShapeDtypeStruct((rows, D), F32),
        grid_spec=grid_spec,
        compiler_params=_cparams(1),
        name="ffn_residual_ln",
    )(pos, h1, route, mod, ln_g, ln_b, ys)


def _moe_ffn(h1, f, route, counts, mod, w1, w3, w2, ln_g, ln_b):
    tile_expert, tile_valid, pos, fill_tile, fill_on, n_tiles = _plan(route, counts)
    xs = _dispatch(f, pos, fill_tile, fill_on, n_tiles * MOE_TM)
    ys = _moe_experts(xs, tile_expert, tile_valid, w1, w3, w2)
    return _ffn_residual(h1, route, pos, ys, mod, ln_g, ln_b)


def _router_weights(w_group, b_group, w_expert, b_expert):
    pad = LANE - N_GROUPS - N_EXPERTS
    w = jnp.concatenate([w_group, w_expert, jnp.zeros((D, pad), F32)], axis=1)
    b = jnp.concatenate([b_group, b_expert, jnp.zeros((pad,), F32)])[None, :]
    return w, b


def kernel(x, c, ctx, c_ctx, ada_w, ada_b, ln_g, ln_b, gla_w_in, gla_w_gate, gla_b_gate, gla_norm_g, gla_w_out,
           nat_w_in, nat_rpb, nat_w_out, moe_w_group, moe_b_group, moe_w_expert, moe_b_expert,
           moe_w1, moe_w3, moe_w2):
    stream = (x.reshape(R_LAT, D), ctx.reshape(R_CTX, D), 0)
    cc = jnp.concatenate([c, c_ctx[None, :], jnp.zeros((SUBLANE - B - 1, D), F32)], axis=0)
    mod = _ada_table(cc, ada_w, ada_b).reshape(DEPTH, SUBLANE, 6, D)
    w1_all = moe_w1.reshape(DEPTH * N_EXPERTS * D, D_EXPERT)
    w3_all = moe_w3.reshape(DEPTH * N_EXPERTS * D, D_EXPERT)
    w2_all = moe_w2.reshape(DEPTH * N_EXPERTS * D_EXPERT, D)

    def expert_casts(layer):
        up, down = N_EXPERTS * D, N_EXPERTS * D_EXPERT
        return ((w1_all, up, layer * up), (w3_all, up, layer * up), (w2_all, down, layer * down))

    w_in = gla_w_in[0]
    w_main = jnp.concatenate([w_in[:, :GLA_QK] * (GLA_DK ** -0.5), w_in[:, GLA_QK:GLA_MAIN]], axis=1).astype(BF16)
    w_glr = jnp.concatenate([w_in[:, GLA_MAIN:], jnp.zeros((D, LANE - 2 * GLA_RANK), F32)], axis=1).astype(BF16)
    p, glr = _proj(stream, mod[0], w_main, w_glr)
    wg = jnp.zeros((2, LANE, GLA_QK), F32)
    wg = wg.at[0, :GLA_RANK].set(gla_w_gate[0, 0]).at[1, GLA_RANK:2 * GLA_RANK].set(gla_w_gate[0, 1])
    bg = gla_b_gate[0][:, None, :]
    cos, sin = _rope_tables()
    o_f, *experts0 = _gla_scan(p, glr, wg, bg, cos, sin, False, expert_casts(0))
    o_b, *experts1 = _gla_scan(p, glr, wg, bg, cos, sin, True, expert_casts(1))
    w_r, b_r = _router_weights(moe_w_group[0], moe_b_group[0], moe_w_expert[0], moe_b_expert[0])
    h1, f, route, counts = _mix_out((o_f, o_b, p, gla_norm_g[0][None, :]), stream, mod[0], gla_w_out[0].astype(BF16),
                                    ln_g[0, 0][None, :], ln_b[0, 0][None, :], w_r, b_r, R_ALL, gla=True)
    h = _moe_ffn(h1, f, route, counts, mod[0], *experts0, ln_g[0, 1][None, :], ln_b[0, 1][None, :])

    w_in = nat_w_in[0]
    w_nat = jnp.concatenate([w_in[:, :D] * (NAT_DH ** -0.5), w_in[:, D:]], axis=1).astype(BF16)
    stream = (h, h, R_LAT)
    p = _proj(stream, mod[1], w_nat)
    o = _nat_attention(p, _nat_bias_table(nat_rpb[0]))
    w_r, b_r = _router_weights(moe_w_group[1], moe_b_group[1], moe_w_expert[1], moe_b_expert[1])
    h1, f, route, counts = _mix_out((o,), stream, mod[1], nat_w_out[0].astype(BF16),
                                    ln_g[1, 0][None, :], ln_b[1, 0][None, :], w_r, b_r, R_LAT, gla=False)
    out = _moe_ffn(h1, f, route, counts, mod[1], *experts1, ln_g[1, 1][None, :], ln_b[1, 1][None, :])
    return out.reshape(B, SEQ, D)
```

```python
import functools

import jax
import jax.numpy as jnp
import numpy as np
from jax import lax
from jax.experimental import pallas as pl
from jax.experimental.pallas import tpu as pltpu

F32 = jnp.float32
BF16 = jnp.bfloat16

D = 2048
B = 4
SEQ = 4096
DEPTH = 2
GRID_W = 64
CTX = 256
R_LAT = B * SEQ
R_CTX = B * CTX
R_ALL = R_LAT + R_CTX

GLA_H = 4
GLA_DK = 256
GLA_DV = 512
GLA_CHUNK = 64
GLA_RANK = 16
GLA_QK = GLA_H * GLA_DK
GLA_VD = GLA_H * GLA_DV
GLA_MAIN = 2 * GLA_QK + 2 * GLA_VD
GLA_GATE_NORM = 16.0
ROPE_BASE = 10000.0

NAT_H = 16
NAT_DH = 128
WIN_R = 8
WIN_C = 16
NAT_QR = 4
NAT_BAND = 12
NAT_G = 8

N_GROUPS = 4
EPG = 4
N_EXPERTS = 16
D_EXPERT = 1024
MOE_TM = 256

ALPHA = (2 * DEPTH) ** 0.25
LN_EPS = 1e-5
NEG_INF = -1e30

LANE = 128
SUBLANE = 8
VMEM_LIMIT = 56 * 1024 * 1024


def _cparams(n_axes):
    return pltpu.CompilerParams(dimension_semantics=("arbitrary",) * n_axes,
                                vmem_limit_bytes=VMEM_LIMIT)


def _split_bf16(x):
    hi = x.astype(BF16)
    lo = (x - hi.astype(F32)).astype(BF16)
    return hi, lo


def _dot(a, b):
    return jnp.dot(a, b, preferred_element_type=F32)


def _dot_hi(a, b):
    ah, al = _split_bf16(a)
    bh, bl = _split_bf16(b)
    return _dot(ah, bh) + _dot(al, bh) + _dot(ah, bl)


def _dot_nt(a, b):
    return lax.dot_general(a, b, (((1,), (1,)), ((), ())), preferred_element_type=F32)


def _dot_tn(a, b):
    return lax.dot_general(a, b, (((0,), (0,)), ((), ())), preferred_element_type=F32)


D_PACK = D // 2


def _pack_rows(x):
    lo = pltpu.bitcast(x[:, :D_PACK].astype(BF16).astype(F32), jnp.uint32)
    hi = pltpu.bitcast(x[:, D_PACK:].astype(BF16).astype(F32), jnp.uint32)
    return (lo >> 16) | hi


def _unpack_rows(u):
    lo = pltpu.bitcast(u << 16, F32)
    hi = pltpu.bitcast(u & jnp.uint32(0xFFFF0000), F32)
    return jnp.concatenate([lo, hi], axis=1)


CAST_STEPS = 64


def _cast_specs(casts, step_of):
    in_specs, out_shape, out_specs, args = [], [], [], []
    for arr, n_rows, first_row in casts:
        c_rows = n_rows // CAST_STEPS
        first = first_row // c_rows
        chunk = lambda *idx: jnp.minimum(step_of(*idx), CAST_STEPS - 1)
        in_specs.append(pl.BlockSpec((c_rows, arr.shape[1]), lambda *idx, first=first: (first + chunk(*idx), 0)))
        out_shape.append(jax.ShapeDtypeStruct((n_rows, arr.shape[1]), BF16))
        out_specs.append(pl.BlockSpec((c_rows, arr.shape[1]), lambda *idx: (chunk(*idx), 0)))
        args.append(arr)
    return in_specs, out_shape, out_specs, args


def _mod_row(tile_rows):
    return lambda i: jnp.minimum((i * tile_rows) // SEQ, B)


def _stream_specs(stream, tile_rows, lag=0):
    n_lat = R_LAT // tile_rows
    base = stream[2] // tile_rows
    return (pl.BlockSpec((tile_rows, D), lambda i, *_: (jnp.clip(i - lag, 0, n_lat - 1), 0)),
            pl.BlockSpec((tile_rows, D), lambda i, *_: (base + jnp.maximum(i - lag - n_lat, 0), 0)))


ADA_TN = 1024


def _ada_kernel(c_ref, w_ref, b_ref, o_ref):
    x = c_ref[...]
    s = x * jax.nn.sigmoid(x)
    o_ref[0] = _dot_hi(s, w_ref[0]) + b_ref[0]


def _ada_table(cc, ada_w, ada_b):
    n = 6 * D
    return pl.pallas_call(
        _ada_kernel,
        out_shape=jax.ShapeDtypeStruct((DEPTH, SUBLANE, n), F32),
        grid=(DEPTH, n // ADA_TN),
        in_specs=[
            pl.BlockSpec((SUBLANE, D), lambda l, j: (0, 0)),
            pl.BlockSpec((1, D, ADA_TN), lambda l, j: (l, 0, j)),
            pl.BlockSpec((1, 1, ADA_TN), lambda l, j: (l, 0, j)),
        ],
        out_specs=pl.BlockSpec((1, SUBLANE, ADA_TN), lambda l, j: (l, 0, j)),
        compiler_params=_cparams(2),
        name="ada_table",
    )(cc, ada_w, ada_b.reshape(DEPTH, 1, n))


PROJ_TM = 1024
PROJ_TN = 1024


def _proj_kernel(h_ref, hc_ref, mod_ref, w_ref, *rest, with_extra, q_tiles, q_scale):
    if with_extra:
        wx_ref, o_ref, ox_ref, a_scr = rest
    else:
        o_ref, a_scr = rest
    j = pl.program_id(1)

    @pl.when(j == 0)
    def _():
        sh = mod_ref[0, 0:1, :]
        sc = mod_ref[0, 1:2, :]
        h = jnp.where(pl.program_id(0) < R_LAT // PROJ_TM, h_ref[...], hc_ref[...])
        a_scr[...] = (h * (1.0 + sc) + sh).astype(BF16)
        if with_extra:
            ox_ref[...] = _dot(a_scr[...], wx_ref[...])

    scale = jnp.where(j < q_tiles, q_scale, 1.0)
    o_ref[...] = (_dot(a_scr[...], w_ref[...]) * scale).astype(o_ref.dtype)


def _proj(stream, mod, w, n, q_cols, q_scale, w_extra=None):
    rows = R_ALL
    with_extra = w_extra is not None
    in_specs = [
        *_stream_specs(stream, PROJ_TM),
        pl.BlockSpec((1, 6, D), lambda i, j: (_mod_row(PROJ_TM)(i), 0, 0)),
        pl.BlockSpec((D, PROJ_TN), lambda i, j: (0, j)),
    ]
    out_shape = [jax.ShapeDtypeStruct((rows, n), BF16)]
    out_specs = [pl.BlockSpec((PROJ_TM, PROJ_TN), lambda i, j: (i, j))]
    args = [stream[0], stream[1], mod, w]
    if with_extra:
        in_specs.append(pl.BlockSpec((D, LANE), lambda i, j: (0, 0)))
        out_shape.append(jax.ShapeDtypeStruct((rows, LANE), F32))
        out_specs.append(pl.BlockSpec((PROJ_TM, LANE), lambda i, j: (i, 0)))
        args.append(w_extra)
    out = pl.pallas_call(
        functools.partial(_proj_kernel, with_extra=with_extra, q_tiles=q_cols // PROJ_TN, q_scale=q_scale),
        out_shape=out_shape,
        grid=(rows // PROJ_TM, n // PROJ_TN),
        in_specs=in_specs,
        out_specs=out_specs,
        scratch_shapes=[pltpu.VMEM((PROJ_TM, D), BF16)],
        compiler_params=_cparams(2),
        name="mod_proj",
    )(*args)
    return out if with_extra else out[0]


GLA_TB = 256
GLA_STEPS = 1 + SEQ // GLA_TB


def _rope_tables():
    half = GLA_DK // 4
    freqs = ROPE_BASE ** (-np.arange(half, dtype=np.float64) / half)
    t = np.arange(SEQ)
    ang_r = (t // GRID_W)[:, None] * freqs
    ang_c = (t % GRID_W)[:, None] * freqs
    cos = np.concatenate([np.cos(ang_r)] * 2 + [np.cos(ang_c)] * 2, axis=1)
    sin = np.concatenate([-np.sin(ang_r), np.sin(ang_r), -np.sin(ang_c), np.sin(ang_c)], axis=1)
    cos = np.concatenate([cos, np.ones((GLA_TB, GLA_DK))], axis=0)
    sin = np.concatenate([sin, np.zeros((GLA_TB, GLA_DK))], axis=0)
    return jnp.asarray(cos, F32), jnp.asarray(sin, F32)


def _rope(x, cos, sin):
    half = LANE // 2
    parts = [pltpu.roll(x[:, g * LANE:(g + 1) * LANE], half, 1) for g in range(x.shape[1] // LANE)]
    return x * cos + jnp.concatenate(parts, axis=1) * sin


def _gla_kernel(q_ref, k_ref, v_ref, glr_ref, wg_ref, bg_ref, cos_ref, sin_ref, *rest, reverse, n_cast):
    cast_in, o_ref, cast_out, s_scr = rest[:n_cast], rest[n_cast], rest[n_cast + 1:-1], rest[-1]
    for src, dst in zip(cast_in, cast_out):
        dst[...] = src[...].astype(BF16)

    @pl.when(pl.program_id(1) == 0)
    def _():
        s_scr[...] = jnp.zeros_like(s_scr)

    row = lax.broadcasted_iota(jnp.int32, (GLA_CHUNK, GLA_CHUNK), 0)
    col = lax.broadcasted_iota(jnp.int32, (GLA_CHUNK, GLA_CHUNK), 1)
    keep = (col >= row) if reverse else (col <= row)
    tri = jnp.where(keep, 1.0, 0.0).astype(BF16)
    ones = jnp.ones((GLA_CHUNK, LANE), BF16)
    n_chunks = GLA_TB // GLA_CHUNK
    order = range(n_chunks - 1, -1, -1) if reverse else range(n_chunks)
    for ci in order:
        sl = slice(ci * GLA_CHUNK, (ci + 1) * GLA_CHUNK)
        cos = cos_ref[sl, :]
        sin = sin_ref[sl, :]
        z = _dot(glr_ref[sl, :].astype(BF16), wg_ref[0]) + bg_ref[0]
        g = (jnp.minimum(z, 0.0) - jnp.log1p(jnp.exp(-jnp.abs(z)))) * (1.0 / GLA_GATE_NORM)
        g_hi, g_lo = _split_bf16(g)
        cum_all = _dot(tri, g_hi) + _dot(tri, g_lo)
        decay_all = jnp.exp(_dot_tn(g_hi, ones) + _dot_tn(g_lo, ones))
        for hd in range(GLA_H):
            ks = slice(hd * GLA_DK, (hd + 1) * GLA_DK)
            vs = slice(hd * GLA_DV, (hd + 1) * GLA_DV)
            q = _rope(q_ref[sl, ks].astype(F32), cos, sin)
            k = _rope(k_ref[sl, ks].astype(F32), cos, sin)
            v = v_ref[sl, vs]
            cum = cum_all[:, ks]
            tot = cum[0:1, :] if reverse else cum[GLA_CHUNK - 1:GLA_CHUNK, :]
            q_dec = (q * jnp.exp(cum)).astype(BF16)
            k_inv = (k * jnp.exp(-cum)).astype(BF16)
            k_end = (k * jnp.exp(tot - cum)).astype(BF16)
            att = jnp.where(keep, _dot_nt(q_dec, k_inv), 0.0).astype(BF16)
            state = s_scr[hd]
            o_ref[sl, vs] = (_dot(att, v) + _dot(q_dec, state.astype(BF16))).astype(o_ref.dtype)
            decay = jnp.concatenate([decay_all[ks, :]] * (GLA_DV // LANE), axis=1)
            s_scr[hd] = state * decay + _dot_tn(k_end, v)


def _gla_scan(p, glr, wg, bg, cos, sin, reverse, casts):
    lat_blocks = SEQ // GLA_TB

    def tblock(s):
        if reverse:
            return jnp.where(s == 0, 0, GLA_STEPS - s)
        return s

    def rows(b, s):
        tb = tblock(s)
        return jnp.where(tb == 0, R_LAT // GLA_TB + b, b * lat_blocks + tb - 1)

    def rope_rows(s):
        tb = tblock(s)
        return jnp.where(tb == 0, lat_blocks, tb - 1)

    d = 1 if reverse else 0
    c_in, c_shape, c_out, c_args = _cast_specs(casts, lambda b, s: b * GLA_STEPS + s)
    return pl.pallas_call(
        functools.partial(_gla_kernel, reverse=reverse, n_cast=len(casts)),
        out_shape=[jax.ShapeDtypeStruct((R_ALL, GLA_VD), BF16), *c_shape],
        grid=(B, GLA_STEPS),
        in_specs=[
            pl.BlockSpec((GLA_TB, GLA_QK), lambda b, s: (rows(b, s), 0)),
            pl.BlockSpec((GLA_TB, GLA_QK), lambda b, s: (rows(b, s), 1)),
            pl.BlockSpec((GLA_TB, GLA_VD), lambda b, s: (rows(b, s), 2 * GLA_QK // GLA_VD)),
            pl.BlockSpec((GLA_TB, LANE), lambda b, s: (rows(b, s), 0)),
            pl.BlockSpec((1, LANE, GLA_QK), lambda b, s: (d, 0, 0)),
            pl.BlockSpec((1, 1, GLA_QK), lambda b, s: (d, 0, 0)),
            pl.BlockSpec((GLA_TB, GLA_DK), lambda b, s: (rope_rows(s), 0)),
            pl.BlockSpec((GLA_TB, GLA_DK), lambda b, s: (rope_rows(s), 0)),
            *c_in,
        ],
        out_specs=[pl.BlockSpec((GLA_TB, GLA_VD), lambda b, s: (rows(b, s), 0)), *c_out],
        scratch_shapes=[pltpu.VMEM((GLA_H, GLA_DK, GLA_DV), F32)],
        compiler_params=_cparams(2),
        name="gla_scan_bwd" if reverse else "gla_scan_fwd",
    )(p, p, p, glr, wg, bg, cos, sin, *c_args)


def _nat_bias_table(rpb):
    qc = np.arange(GRID_W)[:, None]
    kc = np.arange(GRID_W)[None, :]
    cs = np.clip(qc - WIN_C // 2, 0, GRID_W - WIN_C)
    col_ok = (kc >= cs) & (kc < cs + WIN_C)
    dc = np.clip(kc - qc + WIN_C - 1, 0, 2 * WIN_C - 2)
    pick = (dc[:, :, None] == np.arange(2 * WIN_C - 1)).astype(np.float32)
    tc = jnp.einsum('had,qkd->hqak', rpb.astype(F32), jnp.asarray(pick), precision=lax.Precision.HIGHEST)
    tc = jnp.where(col_ok[None, :, None, :], tc, NEG_INF)
    n_dr = 2 * WIN_R - 1
    pad = NAT_BAND - 1
    strip = jnp.pad(tc.reshape(NAT_H, GRID_W, n_dr * GRID_W), ((0, 0), (0, 0), (pad * GRID_W, pad * GRID_W)),
                    constant_values=NEG_INF)
    rows_n = SEQ // GRID_W
    half = WIN_R // 2
    n_keys = NAT_BAND * GRID_W
    bands, valid = [], np.zeros((3, NAT_QR, 1, NAT_BAND, 1), bool)
    for p_i, (r0, bs) in enumerate(((0, 0), (NAT_QR, NAT_QR - half), (rows_n - NAT_QR, rows_n - NAT_BAND))):
        for i in range(NAT_QR):
            qr = r0 + i
            rs = min(max(qr - half, 0), rows_n - WIN_R)
            first = (pad + bs - qr + WIN_R - 1) * GRID_W
            bands.append(strip[:, :, first:first + n_keys])
            valid[p_i, i, 0, :, 0] = [rs <= kr < rs + WIN_R for kr in range(bs, bs + NAT_BAND)]
    tab = jnp.stack(bands).reshape(3, NAT_QR, NAT_H, GRID_W, n_keys)
    valid = np.broadcast_to(valid, (3, NAT_QR, GRID_W, NAT_BAND, GRID_W)).reshape(3, NAT_QR, 1, GRID_W, n_keys)
    return jnp.where(valid, tab, NEG_INF)


def _nat_kernel(q_ref, k_ref, v_ref, kc_ref, vc_ref, bias_ref, o_ref):
    rb = pl.program_id(2)
    rows_n = SEQ // GRID_W
    band_start = jnp.clip(rb * NAT_QR - WIN_R // 2, 0, rows_n - NAT_BAND)
    start = pl.multiple_of(band_start * GRID_W, GRID_W)
    n_keys = NAT_BAND * GRID_W
    for g in range(NAT_G):
        hs = slice(g * NAT_DH, (g + 1) * NAT_DH)
        q = q_ref[:, hs]
        bias = bias_ref[0, :, g].reshape(NAT_QR * GRID_W, n_keys)
        s_loc = _dot_nt(q, k_ref[pl.ds(start, n_keys), hs]) + bias
        s_ctx = _dot_nt(q, kc_ref[:, hs])
        m = jnp.maximum(jnp.max(s_loc, axis=1, keepdims=True), jnp.max(s_ctx, axis=1, keepdims=True))
        p_loc = jnp.exp(s_loc - m)
        p_ctx = jnp.exp(s_ctx - m)
        denom = jnp.sum(p_loc, axis=1, keepdims=True) + jnp.sum(p_ctx, axis=1, keepdims=True)
        o = _dot(p_loc.astype(BF16), v_ref[pl.ds(start, n_keys), hs]) + _dot(p_ctx.astype(BF16), vc_ref[:, hs])
        o_ref[:, hs] = (o / denom).astype(o_ref.dtype)


def _nat_attention(p, bias_tab):
    n_rb = SEQ // (NAT_QR * GRID_W)
    tq = NAT_QR * GRID_W
    width = NAT_G * NAT_DH
    hq = D // width

    def pattern(rb):
        return jnp.where(rb == 0, 0, jnp.where(rb == n_rb - 1, 2, 1))

    return pl.pallas_call(
        _nat_kernel,
        out_shape=jax.ShapeDtypeStruct((R_LAT, D), BF16),
        grid=(B, NAT_H // NAT_G, n_rb),
        in_specs=[
            pl.BlockSpec((tq, width), lambda b, h, rb: (b * n_rb + rb, h)),
            pl.BlockSpec((SEQ, width), lambda b, h, rb: (b, hq + h)),
            pl.BlockSpec((SEQ, width), lambda b, h, rb: (b, 2 * hq + h)),
            pl.BlockSpec((CTX, width), lambda b, h, rb: (R_LAT // CTX + b, hq + h)),
            pl.BlockSpec((CTX, width), lambda b, h, rb: (R_LAT // CTX + b, 2 * hq + h)),
            pl.BlockSpec((1, NAT_QR, NAT_G, GRID_W, NAT_BAND * GRID_W), lambda b, h, rb: (pattern(rb), 0, h, 0, 0)),
        ],
        out_specs=pl.BlockSpec((tq, width), lambda b, h, rb: (b * n_rb + rb, h)),
        compiler_params=_cparams(3),
        name="nat_attention",
    )(p, p, p, p, p, bias_tab)


MIX_TM = 256


def _layer_norm(u, g, b):
    mu = jnp.mean(u, axis=-1, keepdims=True)
    var = jnp.mean(jnp.square(u - mu), axis=-1, keepdims=True)
    return (u - mu) * lax.rsqrt(var + LN_EPS) * g + b


def _mix_out_kernel(*refs, gla, lat_tiles):
    if gla:
        of_ref, ob_ref, r_ref, ng_ref = refs[:4]
        refs = refs[4:]
    else:
        x_ref = refs[0]
        refs = refs[1:]
    (h_ref, hc_ref, mod_ref, w_ref, lng_ref, lnb_ref, wr_ref, br_ref,
     h1_ref, f_ref, rt_ref, cnt_ref, y_even, y_odd, cnt_scr) = refs
    s = pl.program_id(0)

    @pl.when(s == 0)
    def _():
        cnt_scr[...] = jnp.zeros_like(cnt_scr)
        y_odd[...] = jnp.zeros_like(y_odd)

    def step(y_new, y_prev):
        if gla:
            parts = []
            for hd in range(GLA_H):
                cs = slice(hd * GLA_DV, (hd + 1) * GLA_DV)
                o = of_ref[:, cs].astype(F32) + ob_ref[:, cs].astype(F32)
                o = o * lax.rsqrt(jnp.mean(jnp.square(o), axis=-1, keepdims=True) + LN_EPS) * ng_ref[...]
                r = r_ref[:, cs].astype(F32)
                parts.append((o * (r * jax.nn.sigmoid(r))).astype(BF16))
            x = jnp.concatenate(parts, axis=1)
        else:
            x = x_ref[...]
        y_new[...] = _dot(x, w_ref[...])

        gate1 = mod_ref[0, 2:3, :]
        h_in = jnp.where(s - 1 < lat_tiles, h_ref[...], hc_ref[...])
        h1 = _layer_norm(ALPHA * h_in + gate1 * y_prev[...], lng_ref[...], lnb_ref[...])
        h1_ref[...] = h1
        f = h1 * (1.0 + mod_ref[0, 4:5, :]) + mod_ref[0, 3:4, :]
        f_ref[...] = _pack_rows(f)
        live = jnp.where(s > 0, 1.0, 0.0)
        logits = _dot(f.astype(BF16), wr_ref[...]) + br_ref[...]
        route, counts = _route_tile(logits, cnt_scr[...], live)
        rt_ref[...] = route
        cnt_scr[...] = counts
        cnt_ref[...] = jnp.broadcast_to(counts, cnt_ref.shape)

    @pl.when(s % 2 == 0)
    def _():
        step(y_even, y_odd)

    @pl.when(s % 2 == 1)
    def _():
        step(y_odd, y_even)


RT_E, RT_W, RT_RANK = 0, 2, 4


def _route_tile(logits, run, live):
    tm = logits.shape[0]
    lane = lax.broadcasted_iota(jnp.int32, (tm, LANE), 1).astype(F32)

    def first_max(vals):
        top = jnp.max(vals, axis=1, keepdims=True)
        return top, jnp.min(jnp.where(vals == top, lane, float(LANE)), axis=1, keepdims=True)

    gl = jnp.where(lane < N_GROUPS, logits, NEG_INF)
    g_max, g_idx = first_max(gl)
    g_p = 1.0 / jnp.sum(jnp.exp(gl - g_max), axis=1, keepdims=True)
    lo = N_GROUPS + EPG * g_idx
    el = jnp.where((lane >= lo) & (lane < lo + EPG), logits, NEG_INF)
    e1, i1 = first_max(el)
    e2, i2 = first_max(jnp.where(lane == i1, NEG_INF, el))
    t = jnp.exp(e2 - e1)
    p1 = 1.0 / (1.0 + t)
    x1 = i1 - N_GROUPS
    x2 = i2 - N_GROUPS
    onehot = jnp.where((lane == x1) | (lane == x2), 1.0, 0.0)
    row = lax.broadcasted_iota(jnp.int32, (tm, tm), 0)
    col = lax.broadcasted_iota(jnp.int32, (tm, tm), 1)
    before = jnp.where(col < row, 1.0, 0.0).astype(BF16)
    seen = _dot(before, onehot.astype(BF16)) + run
    rank1 = jnp.sum(jnp.where(lane == x1, seen, 0.0), axis=1, keepdims=True)
    rank2 = jnp.sum(jnp.where(lane == x2, seen, 0.0), axis=1, keepdims=True)
    route = jnp.zeros((tm, LANE), F32)
    for k, val in ((RT_E, x1), (RT_E + 1, x2), (RT_W, g_p * p1), (RT_W + 1, g_p * p1 * t),
                   (RT_RANK, rank1), (RT_RANK + 1, rank2)):
        route = jnp.where(lane == k, val, route)
    return route, run + live * jnp.sum(onehot, axis=0, keepdims=True)


def _mix_out(mix_inputs, stream, mod, w_out, ln_g, ln_b, w_router, b_router, rows, gla):
    n_tiles = rows // MIX_TM
    row_spec = lambda width: pl.BlockSpec((MIX_TM, width), lambda i: (jnp.minimum(i, n_tiles - 1), 0))
    done_spec = lambda width: pl.BlockSpec((MIX_TM, width), lambda i: (jnp.maximum(i - 1, 0), 0))
    full = lambda a: pl.BlockSpec(a.shape, lambda i: (0,) * a.ndim)
    if gla:
        o_f, o_b, p, norm_g = mix_inputs
        in_specs = [row_spec(GLA_VD), row_spec(GLA_VD),
                    pl.BlockSpec((MIX_TM, GLA_VD),
                                 lambda i: (jnp.minimum(i, n_tiles - 1), (2 * GLA_QK + GLA_VD) // GLA_VD)),
                    full(norm_g)]
        args = [o_f, o_b, p, norm_g]
    else:
        (x,) = mix_inputs
        in_specs = [row_spec(D)]
        args = [x]
    in_specs += [*_stream_specs(stream, MIX_TM, lag=1),
                 pl.BlockSpec((1, 6, D), lambda i: (_mod_row(MIX_TM)(jnp.maximum(i - 1, 0)), 0, 0)),
                 full(w_out), full(ln_g), full(ln_b), full(w_router), full(b_router)]
    args += [stream[0], stream[1], mod, w_out, ln_g, ln_b, w_router, b_router]
    return pl.pallas_call(
        functools.partial(_mix_out_kernel, gla=gla, lat_tiles=R_LAT // MIX_TM),
        out_shape=[jax.ShapeDtypeStruct((rows, D), F32), jax.ShapeDtypeStruct((rows, D_PACK), jnp.uint32),
                   jax.ShapeDtypeStruct((rows, LANE), F32), jax.ShapeDtypeStruct((SUBLANE, LANE), F32)],
        grid=(n_tiles + 1,),
        in_specs=in_specs,
        out_specs=[done_spec(D), done_spec(D_PACK), done_spec(LANE), pl.BlockSpec((SUBLANE, LANE), lambda i: (0, 0))],
        scratch_shapes=[pltpu.VMEM((MIX_TM, D), F32), pltpu.VMEM((MIX_TM, D), F32), pltpu.VMEM((1, LANE), F32)],
        compiler_params=_cparams(1),
        name="gla_out_ln" if gla else "nat_out_ln",
    )(*args)


def _plan(route, counts_row):
    t = route.shape[0]
    e = route[:, RT_E:RT_E + 2].astype(jnp.int32)
    rank = route[:, RT_RANK:RT_RANK + 2].astype(jnp.int32)
    counts = counts_row[0, :N_EXPERTS].astype(jnp.int32)
    tiles_per = (counts + MOE_TM - 1) // MOE_TM
    tile_end = jnp.cumsum(tiles_per)
    tile_start = tile_end - tiles_per
    n_tiles = 2 * t // MOE_TM + N_EXPERTS
    ids = jnp.arange(n_tiles, dtype=jnp.int32)
    used = tile_end[-1]
    tile_valid = (ids < used).astype(jnp.int32)
    lookup = jnp.minimum(ids, used - 1)
    tile_expert = jnp.sum((tile_end[None, :] <= lookup[:, None]).astype(jnp.int32), axis=1)
    tile_expert = jnp.minimum(tile_expert, N_EXPERTS - 1).astype(jnp.int32)
    base = tile_start * MOE_TM
    experts = jnp.arange(N_EXPERTS, dtype=jnp.int32)
    pos = jnp.sum(jnp.where(e[..., None] == experts, base, 0), axis=-1) + rank
    trailing = n_tiles - 1 - experts
    fill_tile = jnp.concatenate([jnp.maximum(tile_end - 1, 0), trailing]).astype(jnp.int32)
    fill_on = jnp.concatenate([tiles_per > 0, trailing >= used]).astype(jnp.int32)
    return tile_expert, tile_valid, pos.reshape(-1).astype(jnp.int32), fill_tile, fill_on, n_tiles


def _start_row_copies(pos_ref, tile, rows, make):
    base = 2 * tile * rows
    for r in range(rows):
        make(r, pos_ref[base + 2 * r], False).start()
        make(r, pos_ref[base + 2 * r + 1], True).start()


DISP_TM = 256


def _dispatch_kernel(pos_ref, lt_ref, ht_ref, f_ref, xs_hbm, sbuf, zbuf, sem, zsem):
    i = pl.program_id(0)
    n = pl.num_programs(0)
    slot = i % 2

    def fill(e):
        return pltpu.make_async_copy(zbuf, xs_hbm.at[pl.ds(lt_ref[e] * MOE_TM, MOE_TM), :], zsem)

    @pl.when(i == 0)
    def _():
        zbuf[...] = jnp.zeros_like(zbuf)
        for e in range(2 * N_EXPERTS):
            @pl.when(ht_ref[e] != 0)
            def _():
                fill(e).start()
        for e in range(2 * N_EXPERTS):
            @pl.when(ht_ref[e] != 0)
            def _():
                fill(e).wait()

    def wait_scatter(sl):
        for _ in range(2):
            pltpu.make_async_copy(sbuf.at[sl], xs_hbm.at[pl.ds(0, DISP_TM), :], sem.at[sl]).wait()

    @pl.when(i >= 2)
    def _():
        wait_scatter(slot)

    sbuf[slot] = f_ref[...]

    def make(r, dst, second):
        return pltpu.make_async_copy(sbuf.at[slot, pl.ds(r, 1), :], xs_hbm.at[pl.ds(dst, 1), :], sem.at[slot])

    _start_row_copies(pos_ref, i, DISP_TM, make)

    @pl.when(i == n - 1)
    def _():
        wait_scatter(slot)

        @pl.when(n >= 2)
        def _():
            wait_scatter(1 - slot)


def _dispatch(f, pos, last_tile, has_tile, cap):
    t = f.shape[0]
    grid_spec = pltpu.PrefetchScalarGridSpec(
        num_scalar_prefetch=3,
        grid=(t // DISP_TM,),
        in_specs=[pl.BlockSpec((DISP_TM, D_PACK), lambda i, *_: (i, 0))],
        out_specs=pl.BlockSpec(memory_space=pl.ANY),
        scratch_shapes=[
            pltpu.VMEM((2, DISP_TM, D_PACK), jnp.uint32),
            pltpu.VMEM((MOE_TM, D_PACK), jnp.uint32),
            pltpu.SemaphoreType.DMA((2,)),
            pltpu.SemaphoreType.DMA,
        ],
    )
    return pl.pallas_call(
        _dispatch_kernel,
        out_shape=jax.ShapeDtypeStruct((cap, D_PACK), jnp.uint32),
        grid_spec=grid_spec,
        compiler_params=_cparams(1),
        name="moe_dispatch",
    )(pos, last_tile, has_tile, f)


def _moe_kernel(te_ref, tv_ref, x_ref, w1_ref, w3_ref, w2_ref, y_ref):
    i = pl.program_id(0)

    @pl.when(tv_ref[i] != 0)
    def _():
        x = _unpack_rows(x_ref[...]).astype(BF16)
        h1 = _dot(x, w1_ref[...])
        h3 = _dot(x, w3_ref[...])
        a = (h1 * jax.nn.sigmoid(h1) * h3).astype(BF16)
        y_ref[...] = _pack_rows(_dot(a, w2_ref[...]))

    @pl.when(tv_ref[i] == 0)
    def _():
        y_ref[...] = jnp.zeros_like(y_ref)


def _moe_experts(xs, tile_expert, tile_valid, w1, w3, w2):
    cap = xs.shape[0]
    grid_spec = pltpu.PrefetchScalarGridSpec(
        num_scalar_prefetch=2,
        grid=(cap // MOE_TM,),
        in_specs=[
            pl.BlockSpec((MOE_TM, D_PACK), lambda i, te, tv: (i * tv[i], 0)),
            pl.BlockSpec((D, D_EXPERT), lambda i, te, tv: (te[i], 0)),
            pl.BlockSpec((D, D_EXPERT), lambda i, te, tv: (te[i], 0)),
            pl.BlockSpec((D_EXPERT, D), lambda i, te, tv: (te[i], 0)),
        ],
        out_specs=pl.BlockSpec((MOE_TM, D_PACK), lambda i, te, tv: (i, 0)),
    )
    return pl.pallas_call(
        _moe_kernel,
        out_shape=jax.ShapeDtypeStruct((cap, D_PACK), jnp.uint32),
        grid_spec=grid_spec,
        compiler_params=_cparams(1),
        name="moe_experts",
    )(tile_expert, tile_valid, xs, w1, w3, w2)


FIN_TM = 256


def _final_kernel(pos_ref, h_ref, rt_ref, mod_ref, lng_ref, lnb_ref, ys_hbm, o_ref, abuf, bbuf, sem):
    i = pl.program_id(0)
    n = pl.num_programs(0)
    slot = i % 2

    def start_gather(tile, sl):
        def make(r, src, second):
            buf = bbuf if second else abuf
            return pltpu.make_async_copy(ys_hbm.at[pl.ds(src, 1), :], buf.at[sl, pl.ds(r, 1), :], sem.at[sl])

        _start_row_copies(pos_ref, tile, FIN_TM, make)

    @pl.when(i == 0)
    def _():
        start_gather(0, 0)

    @pl.when(i + 1 < n)
    def _():
        start_gather(i + 1, 1 - slot)

    for buf in (abuf, bbuf):
        pltpu.make_async_copy(ys_hbm.at[pl.ds(0, FIN_TM), :], buf.at[slot], sem.at[slot]).wait()

    y = (_unpack_rows(abuf[slot]) * rt_ref[:, RT_W:RT_W + 1]
         + _unpack_rows(bbuf[slot]) * rt_ref[:, RT_W + 1:RT_W + 2])
    gate2 = mod_ref[0, 5:6, :]
    o_ref[...] = _layer_norm(ALPHA * h_ref[...] + gate2 * y, lng_ref[...], lnb_ref[...])


def _ffn_residual(h1, route, pos, ys, mod, ln_g, ln_b):
    rows = h1.shape[0]
    grid_spec = pltpu.PrefetchScalarGridSpec(
        num_scalar_prefetch=1,
        grid=(rows // FIN_TM,),
        in_specs=[
            pl.BlockSpec((FIN_TM, D), lambda i, *_: (i, 0)),
            pl.BlockSpec((FIN_TM, LANE), lambda i, *_: (i, 0)),
            pl.BlockSpec((1, 6, D), lambda i, *_: (_mod_row(FIN_TM)(i), 0, 0)),
            pl.BlockSpec((1, D), lambda i, *_: (0, 0)),
            pl.BlockSpec((1, D), lambda i, *_: (0, 0)),
            pl.BlockSpec(memory_space=pl.ANY),
        ],
        out_specs=pl.BlockSpec((FIN_TM, D), lambda i, *_: (i, 0)),
        scratch_shapes=[
            pltpu.VMEM((2, FIN_TM, D_PACK), jnp.uint32),
            pltpu.VMEM((2, FIN_TM, D_PACK), jnp.uint32),
            pltpu.SemaphoreType.DMA((2,)),
        ],
    )
    return pl.pallas_call(
        _final_kernel,
        out_shape=jax.ShapeDtypeStruct((rows, D), F32),
        grid_spec=grid_spec,
        compiler_params=_cparams(1),
        name="ffn_residual_ln",
    )(pos, h1, route, mod, ln_g, ln_b, ys)


def _moe_ffn(h1, f, route, counts, mod, w1, w3, w2, ln_g, ln_b):
    tile_expert, tile_valid, pos, fill_tile, fill_on, n_tiles = _plan(route, counts)
    xs = _dispatch(f, pos, fill_tile, fill_on, n_tiles * MOE_TM)
    ys = _moe_experts(xs, tile_expert, tile_valid, w1, w3, w2)
    return _ffn_residual(h1, route, pos, ys, mod, ln_g, ln_b)


def _router_weights(w_group, b_group, w_expert, b_expert):
    pad = LANE - N_GROUPS - N_EXPERTS
    w = jnp.concatenate([w_group, w_expert, jnp.zeros((D, pad), F32)], axis=1).astype(BF16)
    b = jnp.concatenate([b_group, b_expert, jnp.zeros((pad,), F32)])[None, :]
    return w, b


def kernel(x, c, ctx, c_ctx, ada_w, ada_b, ln_g, ln_b, gla_w_in, gla_w_gate, gla_b_gate, gla_norm_g, gla_w_out,
           nat_w_in, nat_rpb, nat_w_out, moe_w_group, moe_b_group, moe_w_expert, moe_b_expert,
           moe_w1, moe_w3, moe_w2):
    stream = (x.reshape(R_LAT, D), ctx.reshape(R_CTX, D), 0)
    cc = jnp.concatenate([c, c_ctx[None, :], jnp.zeros((SUBLANE - B - 1, D), F32)], axis=0)
    mod = _ada_table(cc, ada_w, ada_b).reshape(DEPTH, SUBLANE, 6, D)
    w1_all = moe_w1.reshape(DEPTH * N_EXPERTS * D, D_EXPERT)
    w3_all = moe_w3.reshape(DEPTH * N_EXPERTS * D, D_EXPERT)
    w2_all = moe_w2.reshape(DEPTH * N_EXPERTS * D_EXPERT, D)

    def expert_casts(layer):
        up, down = N_EXPERTS * D, N_EXPERTS * D_EXPERT
        return ((w1_all, up, layer * up), (w3_all, up, layer * up), (w2_all, down, layer * down))

    w_in = gla_w_in[0]
    w_glr = jnp.concatenate([w_in[:, GLA_MAIN:], jnp.zeros((D, LANE - 2 * GLA_RANK), F32)], axis=1).astype(BF16)
    p, glr = _proj(stream, mod[0], w_in.astype(BF16), GLA_MAIN, GLA_QK, GLA_DK ** -0.5, w_glr)
    wg = jnp.zeros((2, LANE, GLA_QK), F32)
    wg = wg.at[0, :GLA_RANK].set(gla_w_gate[0, 0]).at[1, GLA_RANK:2 * GLA_RANK].set(gla_w_gate[0, 1]).astype(BF16)
    bg = gla_b_gate[0][:, None, :]
    cos, sin = _rope_tables()
    o_f, *experts0 = _gla_scan(p, glr, wg, bg, cos, sin, False, expert_casts(0))
    o_b, *experts1 = _gla_scan(p, glr, wg, bg, cos, sin, True, expert_casts(1))
    w_r, b_r = _router_weights(moe_w_group[0], moe_b_group[0], moe_w_expert[0], moe_b_expert[0])
    h1, f, route, counts = _mix_out((o_f, o_b, p, gla_norm_g[0][None, :]), stream, mod[0], gla_w_out[0].astype(BF16),
                                    ln_g[0, 0][None, :], ln_b[0, 0][None, :], w_r, b_r, R_ALL, gla=True)
    h = _moe_ffn(h1, f, route, counts, mod[0], *experts0, ln_g[0, 1][None, :], ln_b[0, 1][None, :])

    stream = (h, h, R_LAT)
    p = _proj(stream, mod[1], nat_w_in[0].astype(BF16), 3 * D, D, NAT_DH ** -0.5)
    o = _nat_attention(p, _nat_bias_table(nat_rpb[0]))
    w_r, b_r = _router_weights(moe_w_group[1], moe_b_group[1], moe_w_expert[1], moe_b_expert[1])
    h1, f, route, counts = _mix_out((o,), stream, mod[1], nat_w_out[0].astype(BF16),
                                    ln_g[1, 0][None, :], ln_b[1, 0][None, :], w_r, b_r, R_LAT, gla=False)
    out = _moe_ffn(h1, f, route, counts, mod[1], *experts1, ln_g[1, 1][None, :], ln_b[1, 1][None, :])
    return out.reshape(B, SEQ, D)
```

```python
import functools

import jax
import jax.numpy as jnp
import numpy as np
from jax import lax
from jax.experimental import pallas as pl
from jax.experimental.pallas import tpu as pltpu

F32 = jnp.float32
BF16 = jnp.bfloat16

D = 2048
B = 4
SEQ = 4096
DEPTH = 2
GRID_W = 64
CTX = 256
R_LAT = B * SEQ
R_CTX = B * CTX
R_ALL = R_LAT + R_CTX

GLA_H = 4
GLA_DK = 256
GLA_DV = 512
GLA_CHUNK = 64
GLA_RANK = 16
GLA_QK = GLA_H * GLA_DK
GLA_VD = GLA_H * GLA_DV
GLA_MAIN = 2 * GLA_QK + 2 * GLA_VD
GLA_GATE_NORM = 16.0
ROPE_BASE = 10000.0

NAT_H = 16
NAT_DH = 128
WIN_R = 8
WIN_C = 16
NAT_QR = 4
NAT_BAND = 12
NAT_G = 8

N_GROUPS = 4
EPG = 4
N_EXPERTS = 16
D_EXPERT = 1024
MOE_TM = 256

ALPHA = (2 * DEPTH) ** 0.25
LN_EPS = 1e-5
NEG_INF = -1e30

LANE = 128
SUBLANE = 8
VMEM_LIMIT = 56 * 1024 * 1024


def _cparams(n_axes):
    return pltpu.CompilerParams(dimension_semantics=("arbitrary",) * n_axes,
                                vmem_limit_bytes=VMEM_LIMIT)


def _split_bf16(x):
    hi = x.astype(BF16)
    lo = (x - hi.astype(F32)).astype(BF16)
    return hi, lo


def _dot(a, b):
    return jnp.dot(a, b, preferred_element_type=F32)


def _dot_hi(a, b):
    ah, al = _split_bf16(a)
    bh, bl = _split_bf16(b)
    return _dot(ah, bh) + _dot(al, bh) + _dot(ah, bl)


def _dot_nt(a, b):
    return lax.dot_general(a, b, (((1,), (1,)), ((), ())), preferred_element_type=F32)


def _dot_tn(a, b):
    return lax.dot_general(a, b, (((0,), (0,)), ((), ())), preferred_element_type=F32)


D_PACK = D // 2


def _pack_rows(x):
    lo = pltpu.bitcast(x[:, :D_PACK].astype(BF16).astype(F32), jnp.uint32)
    hi = pltpu.bitcast(x[:, D_PACK:].astype(BF16).astype(F32), jnp.uint32)
    return (lo >> 16) | hi


def _unpack_rows(u):
    lo = pltpu.bitcast(u << 16, F32)
    hi = pltpu.bitcast(u & jnp.uint32(0xFFFF0000), F32)
    return jnp.concatenate([lo, hi], axis=1)


CAST_STEPS = 64


def _cast_specs(casts, step_of):
    in_specs, out_shape, out_specs, args = [], [], [], []
    for arr, n_rows, first_row in casts:
        c_rows = n_rows // CAST_STEPS
        first = first_row // c_rows
        chunk = lambda *idx: jnp.minimum(step_of(*idx), CAST_STEPS - 1)
        in_specs.append(pl.BlockSpec((c_rows, arr.shape[1]), lambda *idx, first=first: (first + chunk(*idx), 0)))
        out_shape.append(jax.ShapeDtypeStruct((n_rows, arr.shape[1]), BF16))
        out_specs.append(pl.BlockSpec((c_rows, arr.shape[1]), lambda *idx: (chunk(*idx), 0)))
        args.append(arr)
    return in_specs, out_shape, out_specs, args


def _mod_row(tile_rows):
    return lambda i: jnp.minimum((i * tile_rows) // SEQ, B)


def _stream_specs(stream, tile_rows, lag=0):
    n_lat = R_LAT // tile_rows
    base = stream[2] // tile_rows
    return (pl.BlockSpec((tile_rows, D), lambda i, *_: (jnp.clip(i - lag, 0, n_lat - 1), 0)),
            pl.BlockSpec((tile_rows, D), lambda i, *_: (base + jnp.maximum(i - lag - n_lat, 0), 0)))


ADA_TN = 1024


def _ada_kernel(c_ref, w_ref, b_ref, o_ref):
    x = c_ref[...]
    s = x * jax.nn.sigmoid(x)
    o_ref[0] = _dot_hi(s, w_ref[0]) + b_ref[0]


def _ada_table(cc, ada_w, ada_b):
    n = 6 * D
    return pl.pallas_call(
        _ada_kernel,
        out_shape=jax.ShapeDtypeStruct((DEPTH, SUBLANE, n), F32),
        grid=(DEPTH, n // ADA_TN),
        in_specs=[
            pl.BlockSpec((SUBLANE, D), lambda l, j: (0, 0)),
            pl.BlockSpec((1, D, ADA_TN), lambda l, j: (l, 0, j)),
            pl.BlockSpec((1, 1, ADA_TN), lambda l, j: (l, 0, j)),
        ],
        out_specs=pl.BlockSpec((1, SUBLANE, ADA_TN), lambda l, j: (l, 0, j)),
        compiler_params=_cparams(2),
        name="ada_table",
    )(cc, ada_w, ada_b.reshape(DEPTH, 1, n))


PROJ_TM = 1024
PROJ_TN = 1024


def _proj_kernel(h_ref, hc_ref, mod_ref, w_ref, *rest, with_extra, q_tiles, q_scale):
    if with_extra:
        wx_ref, o_ref, ox_ref, a_scr = rest
    else:
        o_ref, a_scr = rest
    j = pl.program_id(1)

    @pl.when(j == 0)
    def _():
        sh = mod_ref[0, 0:1, :]
        sc = mod_ref[0, 1:2, :]
        h = jnp.where(pl.program_id(0) < R_LAT // PROJ_TM, h_ref[...], hc_ref[...])
        a_scr[...] = (h * (1.0 + sc) + sh).astype(BF16)
        if with_extra:
            ox_ref[...] = _dot(a_scr[...], wx_ref[...])

    scale = jnp.where(j < q_tiles, q_scale, 1.0)
    o_ref[...] = (_dot(a_scr[...], w_ref[...]) * scale).astype(o_ref.dtype)


def _proj(stream, mod, w, n, q_cols, q_scale, w_extra=None):
    rows = R_ALL
    with_extra = w_extra is not None
    in_specs = [
        *_stream_specs(stream, PROJ_TM),
        pl.BlockSpec((1, 6, D), lambda i, j: (_mod_row(PROJ_TM)(i), 0, 0)),
        pl.BlockSpec((D, PROJ_TN), lambda i, j: (0, j)),
    ]
    out_shape = [jax.ShapeDtypeStruct((rows, n), BF16)]
    out_specs = [pl.BlockSpec((PROJ_TM, PROJ_TN), lambda i, j: (i, j))]
    args = [stream[0], stream[1], mod, w]
    if with_extra:
        in_specs.append(pl.BlockSpec((D, LANE), lambda i, j: (0, 0)))
        out_shape.append(jax.ShapeDtypeStruct((rows, LANE), F32))
        out_specs.append(pl.BlockSpec((PROJ_TM, LANE), lambda i, j: (i, 0)))
        args.append(w_extra)
    out = pl.pallas_call(
        functools.partial(_proj_kernel, with_extra=with_extra, q_tiles=q_cols // PROJ_TN, q_scale=q_scale),
        out_shape=out_shape,
        grid=(rows // PROJ_TM, n // PROJ_TN),
        in_specs=in_specs,
        out_specs=out_specs,
        scratch_shapes=[pltpu.VMEM((PROJ_TM, D), BF16)],
        compiler_params=_cparams(2),
        name="mod_proj",
    )(*args)
    return out if with_extra else out[0]


GLA_TB = 256
GLA_STEPS = 1 + SEQ // GLA_TB


def _rope_tables():
    half = GLA_DK // 4
    freqs = ROPE_BASE ** (-np.arange(half, dtype=np.float64) / half)
    t = np.arange(SEQ)
    ang_r = (t // GRID_W)[:, None] * freqs
    ang_c = (t % GRID_W)[:, None] * freqs
    cos = np.concatenate([np.cos(ang_r)] * 2 + [np.cos(ang_c)] * 2, axis=1)
    sin = np.concatenate([-np.sin(ang_r), np.sin(ang_r), -np.sin(ang_c), np.sin(ang_c)], axis=1)
    cos = np.concatenate([cos, np.ones((GLA_TB, GLA_DK))], axis=0)
    sin = np.concatenate([sin, np.zeros((GLA_TB, GLA_DK))], axis=0)
    return jnp.asarray(cos, F32), jnp.asarray(sin, F32)


def _rope(x, cos, sin):
    half = LANE // 2
    parts = [pltpu.roll(x[:, g * LANE:(g + 1) * LANE], half, 1) for g in range(x.shape[1] // LANE)]
    return x * cos + jnp.concatenate(parts, axis=1) * sin


def _gla_kernel(q_ref, k_ref, v_ref, glr_ref, wg_ref, bg_ref, cos_ref, sin_ref, *rest, reverse, n_cast):
    cast_in, o_ref, cast_out, s_scr = rest[:n_cast], rest[n_cast], rest[n_cast + 1:-1], rest[-1]
    for src, dst in zip(cast_in, cast_out):
        dst[...] = src[...].astype(BF16)

    @pl.when(pl.program_id(1) == 0)
    def _():
        s_scr[...] = jnp.zeros_like(s_scr)

    row = lax.broadcasted_iota(jnp.int32, (GLA_CHUNK, GLA_CHUNK), 0)
    col = lax.broadcasted_iota(jnp.int32, (GLA_CHUNK, GLA_CHUNK), 1)
    keep = (col >= row) if reverse else (col <= row)
    tri = jnp.where(keep, 1.0, 0.0).astype(BF16)
    ones = jnp.ones((GLA_CHUNK, LANE), BF16)
    n_chunks = GLA_TB // GLA_CHUNK
    order = range(n_chunks - 1, -1, -1) if reverse else range(n_chunks)
    for ci in order:
        sl = slice(ci * GLA_CHUNK, (ci + 1) * GLA_CHUNK)
        cos = cos_ref[sl, :]
        sin = sin_ref[sl, :]
        z = _dot(glr_ref[sl, :].astype(BF16), wg_ref[0]) + bg_ref[0]
        g = (jnp.minimum(z, 0.0) - jnp.log1p(jnp.exp(-jnp.abs(z)))) * (1.0 / GLA_GATE_NORM)
        g_hi, g_lo = _split_bf16(g)
        cum_all = _dot(tri, g_hi) + _dot(tri, g_lo)
        decay_all = jnp.exp(_dot_tn(g_hi, ones) + _dot_tn(g_lo, ones))
        for hd in range(GLA_H):
            ks = slice(hd * GLA_DK, (hd + 1) * GLA_DK)
            vs = slice(hd * GLA_DV, (hd + 1) * GLA_DV)
            q = _rope(q_ref[sl, ks].astype(F32), cos, sin)
            k = _rope(k_ref[sl, ks].astype(F32), cos, sin)
            v = v_ref[sl, vs]
            cum = cum_all[:, ks]
            tot = cum[0:1, :] if reverse else cum[GLA_CHUNK - 1:GLA_CHUNK, :]
            q_dec = (q * jnp.exp(cum)).astype(BF16)
            k_inv = (k * jnp.exp(-cum)).astype(BF16)
            k_end = (k * jnp.exp(tot - cum)).astype(BF16)
            att = jnp.where(keep, _dot_nt(q_dec, k_inv), 0.0).astype(BF16)
            state = s_scr[hd]
            o_ref[sl, vs] = (_dot(att, v) + _dot(q_dec, state.astype(BF16))).astype(o_ref.dtype)
            decay = jnp.concatenate([decay_all[ks, :]] * (GLA_DV // LANE), axis=1)
            s_scr[hd] = state * decay + _dot_tn(k_end, v)


def _gla_scan(p, glr, wg, bg, cos, sin, reverse, casts):
    lat_blocks = SEQ // GLA_TB

    def tblock(s):
        if reverse:
            return jnp.where(s == 0, 0, GLA_STEPS - s)
        return s

    def rows(b, s):
        tb = tblock(s)
        return jnp.where(tb == 0, R_LAT // GLA_TB + b, b * lat_blocks + tb - 1)

    def rope_rows(s):
        tb = tblock(s)
        return jnp.where(tb == 0, lat_blocks, tb - 1)

    d = 1 if reverse else 0
    c_in, c_shape, c_out, c_args = _cast_specs(casts, lambda b, s: b * GLA_STEPS + s)
    return pl.pallas_call(
        functools.partial(_gla_kernel, reverse=reverse, n_cast=len(casts)),
        out_shape=[jax.ShapeDtypeStruct((R_ALL, GLA_VD), BF16), *c_shape],
        grid=(B, GLA_STEPS),
        in_specs=[
            pl.BlockSpec((GLA_TB, GLA_QK), lambda b, s: (rows(b, s), 0)),
            pl.BlockSpec((GLA_TB, GLA_QK), lambda b, s: (rows(b, s), 1)),
            pl.BlockSpec((GLA_TB, GLA_VD), lambda b, s: (rows(b, s), 2 * GLA_QK // GLA_VD)),
            pl.BlockSpec((GLA_TB, LANE), lambda b, s: (rows(b, s), 0)),
            pl.BlockSpec((1, LANE, GLA_QK), lambda b, s: (d, 0, 0)),
            pl.BlockSpec((1, 1, GLA_QK), lambda b, s: (d, 0, 0)),
            pl.BlockSpec((GLA_TB, GLA_DK), lambda b, s: (rope_rows(s), 0)),
            pl.BlockSpec((GLA_TB, GLA_DK), lambda b, s: (rope_rows(s), 0)),
            *c_in,
        ],
        out_specs=[pl.BlockSpec((GLA_TB, GLA_VD), lambda b, s: (rows(b, s), 0)), *c_out],
        scratch_shapes=[pltpu.VMEM((GLA_H, GLA_DK, GLA_DV), F32)],
        compiler_params=_cparams(2),
        name="gla_scan_bwd" if reverse else "gla_scan_fwd",
    )(p, p, p, glr, wg, bg, cos, sin, *c_args)


def _nat_bias_table(rpb):
    qc = np.arange(GRID_W)[:, None]
    kc = np.arange(GRID_W)[None, :]
    cs = np.clip(qc - WIN_C // 2, 0, GRID_W - WIN_C)
    col_ok = (kc >= cs) & (kc < cs + WIN_C)
    dc = np.clip(kc - qc + WIN_C - 1, 0, 2 * WIN_C - 2)
    pick = (dc[:, :, None] == np.arange(2 * WIN_C - 1)).astype(np.float32)
    tc = jnp.einsum('had,qkd->hqak', rpb.astype(F32), jnp.asarray(pick), precision=lax.Precision.HIGHEST)
    tc = jnp.where(col_ok[None, :, None, :], tc, NEG_INF)
    n_dr = 2 * WIN_R - 1
    pad = NAT_BAND - 1
    strip = jnp.pad(tc.reshape(NAT_H, GRID_W, n_dr * GRID_W), ((0, 0), (0, 0), (pad * GRID_W, pad * GRID_W)),
                    constant_values=NEG_INF)
    rows_n = SEQ // GRID_W
    half = WIN_R // 2
    n_keys = NAT_BAND * GRID_W
    bands, valid = [], np.zeros((3, NAT_QR, 1, NAT_BAND, 1), bool)
    for p_i, (r0, bs) in enumerate(((0, 0), (NAT_QR, NAT_QR - half), (rows_n - NAT_QR, rows_n - NAT_BAND))):
        for i in range(NAT_QR):
            qr = r0 + i
            rs = min(max(qr - half, 0), rows_n - WIN_R)
            first = (pad + bs - qr + WIN_R - 1) * GRID_W
            bands.append(strip[:, :, first:first + n_keys])
            valid[p_i, i, 0, :, 0] = [rs <= kr < rs + WIN_R for kr in range(bs, bs + NAT_BAND)]
    tab = jnp.stack(bands).reshape(3, NAT_QR, NAT_H, GRID_W, n_keys)
    valid = np.broadcast_to(valid, (3, NAT_QR, GRID_W, NAT_BAND, GRID_W)).reshape(3, NAT_QR, 1, GRID_W, n_keys)
    return jnp.where(valid, tab, NEG_INF)


def _nat_kernel(q_ref, k_ref, v_ref, kc_ref, vc_ref, bias_ref, o_ref):
    rb = pl.program_id(2)
    rows_n = SEQ // GRID_W
    band_start = jnp.clip(rb * NAT_QR - WIN_R // 2, 0, rows_n - NAT_BAND)
    start = pl.multiple_of(band_start * GRID_W, GRID_W)
    n_keys = NAT_BAND * GRID_W
    for g in range(NAT_G):
        hs = slice(g * NAT_DH, (g + 1) * NAT_DH)
        q = q_ref[:, hs]
        bias = bias_ref[0, :, g].reshape(NAT_QR * GRID_W, n_keys)
        s_loc = _dot_nt(q, k_ref[pl.ds(start, n_keys), hs]) + bias
        s_ctx = _dot_nt(q, kc_ref[:, hs])
        m = jnp.maximum(jnp.max(s_loc, axis=1, keepdims=True), jnp.max(s_ctx, axis=1, keepdims=True))
        p_loc = jnp.exp(s_loc - m)
        p_ctx = jnp.exp(s_ctx - m)
        denom = jnp.sum(p_loc, axis=1, keepdims=True) + jnp.sum(p_ctx, axis=1, keepdims=True)
        o = _dot(p_loc.astype(BF16), v_ref[pl.ds(start, n_keys), hs]) + _dot(p_ctx.astype(BF16), vc_ref[:, hs])
        o_ref[:, hs] = (o / denom).astype(o_ref.dtype)


def _nat_attention(p, bias_tab):
    n_rb = SEQ // (NAT_QR * GRID_W)
    tq = NAT_QR * GRID_W
    width = NAT_G * NAT_DH
    hq = D // width

    def pattern(rb):
        return jnp.where(rb == 0, 0, jnp.where(rb == n_rb - 1, 2, 1))

    return pl.pallas_call(
        _nat_kernel,
        out_shape=jax.ShapeDtypeStruct((R_LAT, D), BF16),
        grid=(B, NAT_H // NAT_G, n_rb),
        in_specs=[
            pl.BlockSpec((tq, width), lambda b, h, rb: (b * n_rb + rb, h)),
            pl.BlockSpec((SEQ, width), lambda b, h, rb: (b, hq + h)),
            pl.BlockSpec((SEQ, width), lambda b, h, rb: (b, 2 * hq + h)),
            pl.BlockSpec((CTX, width), lambda b, h, rb: (R_LAT // CTX + b, hq + h)),
            pl.BlockSpec((CTX, width), lambda b, h, rb: (R_LAT // CTX + b, 2 * hq + h)),
            pl.BlockSpec((1, NAT_QR, NAT_G, GRID_W, NAT_BAND * GRID_W), lambda b, h, rb: (pattern(rb), 0, h, 0, 0)),
        ],
        out_specs=pl.BlockSpec((tq, width), lambda b, h, rb: (b * n_rb + rb, h)),
        compiler_params=_cparams(3),
        name="nat_attention",
    )(p, p, p, p, p, bias_tab)


MIX_TM = 256


def _layer_norm(u, g, b):
    mu = jnp.mean(u, axis=-1, keepdims=True)
    var = jnp.mean(jnp.square(u - mu), axis=-1, keepdims=True)
    return (u - mu) * lax.rsqrt(var + LN_EPS) * g + b


def _mix_out_kernel(*refs, gla, lat_tiles):
    if gla:
        of_ref, ob_ref, r_ref, ng_ref = refs[:4]
        refs = refs[4:]
    else:
        x_ref = refs[0]
        refs = refs[1:]
    (h_ref, hc_ref, mod_ref, w_ref, lng_ref, lnb_ref, wr_ref, br_ref,
     h1_ref, f_ref, rt_ref, cnt_ref, y_even, y_odd, cnt_scr) = refs
    s = pl.program_id(0)

    @pl.when(s == 0)
    def _():
        cnt_scr[...] = jnp.zeros_like(cnt_scr)
        y_odd[...] = jnp.zeros_like(y_odd)

    def step(y_new, y_prev):
        if gla:
            parts = []
            for hd in range(GLA_H):
                cs = slice(hd * GLA_DV, (hd + 1) * GLA_DV)
                o = of_ref[:, cs].astype(F32) + ob_ref[:, cs].astype(F32)
                o = o * lax.rsqrt(jnp.mean(jnp.square(o), axis=-1, keepdims=True) + LN_EPS) * ng_ref[...]
                r = r_ref[:, cs].astype(F32)
                parts.append((o * (r * jax.nn.sigmoid(r))).astype(BF16))
            x = jnp.concatenate(parts, axis=1)
        else:
            x = x_ref[...]
        y_new[...] = _dot(x, w_ref[...])

        gate1 = mod_ref[0, 2:3, :]
        h_in = jnp.where(s - 1 < lat_tiles, h_ref[...], hc_ref[...])
        h1 = _layer_norm(ALPHA * h_in + gate1 * y_prev[...], lng_ref[...], lnb_ref[...])
        h1_ref[...] = h1
        f = h1 * (1.0 + mod_ref[0, 4:5, :]) + mod_ref[0, 3:4, :]
        f_ref[...] = _pack_rows(f)
        live = jnp.where(s > 0, 1.0, 0.0)
        logits = _dot(f.astype(BF16), wr_ref[...]) + br_ref[...]
        route, counts = _route_tile(logits, cnt_scr[...], live)
        rt_ref[...] = route
        cnt_scr[...] = counts
        cnt_ref[...] = jnp.broadcast_to(counts, cnt_ref.shape)

    @pl.when(s % 2 == 0)
    def _():
        step(y_even, y_odd)

    @pl.when(s % 2 == 1)
    def _():
        step(y_odd, y_even)


RT_E, RT_W, RT_RANK = 0, 2, 4


def _route_tile(logits, run, live):
    tm = logits.shape[0]
    lane = lax.broadcasted_iota(jnp.int32, (tm, LANE), 1).astype(F32)

    def first_max(vals):
        top = jnp.max(vals, axis=1, keepdims=True)
        return top, jnp.min(jnp.where(vals == top, lane, float(LANE)), axis=1, keepdims=True)

    gl = jnp.where(lane < N_GROUPS, logits, NEG_INF)
    g_max, g_idx = first_max(gl)
    g_p = 1.0 / jnp.sum(jnp.exp(gl - g_max), axis=1, keepdims=True)
    lo = N_GROUPS + EPG * g_idx
    el = jnp.where((lane >= lo) & (lane < lo + EPG), logits, NEG_INF)
    e1, i1 = first_max(el)
    e2, i2 = first_max(jnp.where(lane == i1, NEG_INF, el))
    t = jnp.exp(e2 - e1)
    p1 = 1.0 / (1.0 + t)
    x1 = i1 - N_GROUPS
    x2 = i2 - N_GROUPS
    onehot = jnp.where((lane == x1) | (lane == x2), 1.0, 0.0)
    row = lax.broadcasted_iota(jnp.int32, (tm, tm), 0)
    col = lax.broadcasted_iota(jnp.int32, (tm, tm), 1)
    before = jnp.where(col < row, 1.0, 0.0).astype(BF16)
    seen = _dot(before, onehot.astype(BF16)) + run
    rank1 = jnp.sum(jnp.where(lane == x1, seen, 0.0), axis=1, keepdims=True)
    rank2 = jnp.sum(jnp.where(lane == x2, seen, 0.0), axis=1, keepdims=True)
    route = jnp.zeros((tm, LANE), F32)
    for k, val in ((RT_E, x1), (RT_E + 1, x2), (RT_W, g_p * p1), (RT_W + 1, g_p * p1 * t),
                   (RT_RANK, rank1), (RT_RANK + 1, rank2)):
        route = jnp.where(lane == k, val, route)
    return route, run + live * jnp.sum(onehot, axis=0, keepdims=True)


def _mix_out(mix_inputs, stream, mod, w_out, ln_g, ln_b, w_router, b_router, rows, gla):
    n_tiles = rows // MIX_TM
    row_spec = lambda width: pl.BlockSpec((MIX_TM, width), lambda i: (jnp.minimum(i, n_tiles - 1), 0))
    done_spec = lambda width: pl.BlockSpec((MIX_TM, width), lambda i: (jnp.maximum(i - 1, 0), 0))
    full = lambda a: pl.BlockSpec(a.shape, lambda i: (0,) * a.ndim)
    if gla:
        o_f, o_b, p, norm_g = mix_inputs
        in_specs = [row_spec(GLA_VD), row_spec(GLA_VD),
                    pl.BlockSpec((MIX_TM, GLA_VD),
                                 lambda i: (jnp.minimum(i, n_tiles - 1), (2 * GLA_QK + GLA_VD) // GLA_VD)),
                    full(norm_g)]
        args = [o_f, o_b, p, norm_g]
    else:
        (x,) = mix_inputs
        in_specs = [row_spec(D)]
        args = [x]
    in_specs += [*_stream_specs(stream, MIX_TM, lag=1),
                 pl.BlockSpec((1, 6, D), lambda i: (_mod_row(MIX_TM)(jnp.maximum(i - 1, 0)), 0, 0)),
                 full(w_out), full(ln_g), full(ln_b), full(w_router), full(b_router)]
    args += [stream[0], stream[1], mod, w_out, ln_g, ln_b, w_router, b_router]
    return pl.pallas_call(
        functools.partial(_mix_out_kernel, gla=gla, lat_tiles=R_LAT // MIX_TM),
        out_shape=[jax.ShapeDtypeStruct((rows, D), F32), jax.ShapeDtypeStruct((rows, D_PACK), jnp.uint32),
                   jax.ShapeDtypeStruct((rows, LANE), F32), jax.ShapeDtypeStruct((SUBLANE, LANE), F32)],
        grid=(n_tiles + 1,),
        in_specs=in_specs,
        out_specs=[done_spec(D), done_spec(D_PACK), done_spec(LANE), pl.BlockSpec((SUBLANE, LANE), lambda i: (0, 0))],
        scratch_shapes=[pltpu.VMEM((MIX_TM, D), F32), pltpu.VMEM((MIX_TM, D), F32), pltpu.VMEM((1, LANE), F32)],
        compiler_params=_cparams(1),
        name="gla_out_ln" if gla else "nat_out_ln",
    )(*args)


def _plan(route, counts_row):
    t = route.shape[0]
    e = route[:, RT_E:RT_E + 2].astype(jnp.int32)
    rank = route[:, RT_RANK:RT_RANK + 2].astype(jnp.int32)
    counts = counts_row[0, :N_EXPERTS].astype(jnp.int32)
    tiles_per = (counts + MOE_TM - 1) // MOE_TM
    tile_end = jnp.cumsum(tiles_per)
    tile_start = tile_end - tiles_per
    n_tiles = 2 * t // MOE_TM + N_EXPERTS
    ids = jnp.arange(n_tiles, dtype=jnp.int32)
    used = tile_end[-1]
    tile_valid = (ids < used).astype(jnp.int32)
    lookup = jnp.minimum(ids, used - 1)
    tile_expert = jnp.sum((tile_end[None, :] <= lookup[:, None]).astype(jnp.int32), axis=1)
    tile_expert = jnp.minimum(tile_expert, N_EXPERTS - 1).astype(jnp.int32)
    base = tile_start * MOE_TM
    experts = jnp.arange(N_EXPERTS, dtype=jnp.int32)
    pos = jnp.sum(jnp.where(e[..., None] == experts, base, 0), axis=-1) + rank
    trailing = n_tiles - 1 - experts
    fill_tile = jnp.concatenate([jnp.maximum(tile_end - 1, 0), trailing]).astype(jnp.int32)
    fill_on = jnp.concatenate([tiles_per > 0, trailing >= used]).astype(jnp.int32)
    return tile_expert, tile_valid, pos.reshape(-1).astype(jnp.int32), fill_tile, fill_on, n_tiles


def _start_row_copies(pos_ref, tile, rows, make):
    base = 2 * tile * rows
    for r in range(rows):
        make(r, pos_ref[base + 2 * r], False).start()
        make(r, pos_ref[base + 2 * r + 1], True).start()


DISP_TM = 512


def _dispatch_kernel(pos_ref, lt_ref, ht_ref, f_ref, xs_hbm, sbuf, zbuf, sem, zsem):
    i = pl.program_id(0)
    n = pl.num_programs(0)
    slot = i % 2

    def fill(e):
        return pltpu.make_async_copy(zbuf, xs_hbm.at[pl.ds(lt_ref[e] * MOE_TM, MOE_TM), :], zsem)

    @pl.when(i == 0)
    def _():
        zbuf[...] = jnp.zeros_like(zbuf)
        for e in range(2 * N_EXPERTS):
            @pl.when(ht_ref[e] != 0)
            def _():
                fill(e).start()
        for e in range(2 * N_EXPERTS):
            @pl.when(ht_ref[e] != 0)
            def _():
                fill(e).wait()

    def wait_scatter(sl):
        for _ in range(2):
            pltpu.make_async_copy(sbuf.at[sl], xs_hbm.at[pl.ds(0, DISP_TM), :], sem.at[sl]).wait()

    @pl.when(i >= 2)
    def _():
        wait_scatter(slot)

    sbuf[slot] = f_ref[...]

    def make(r, dst, second):
        return pltpu.make_async_copy(sbuf.at[slot, pl.ds(r, 1), :], xs_hbm.at[pl.ds(dst, 1), :], sem.at[slot])

    _start_row_copies(pos_ref, i, DISP_TM, make)

    @pl.when(i == n - 1)
    def _():
        wait_scatter(slot)

        @pl.when(n >= 2)
        def _():
            wait_scatter(1 - slot)


def _dispatch(f, pos, last_tile, has_tile, cap):
    t = f.shape[0]
    grid_spec = pltpu.PrefetchScalarGridSpec(
        num_scalar_prefetch=3,
        grid=(t // DISP_TM,),
        in_specs=[pl.BlockSpec((DISP_TM, D_PACK), lambda i, *_: (i, 0))],
        out_specs=pl.BlockSpec(memory_space=pl.ANY),
        scratch_shapes=[
            pltpu.VMEM((2, DISP_TM, D_PACK), jnp.uint32),
            pltpu.VMEM((MOE_TM, D_PACK), jnp.uint32),
            pltpu.SemaphoreType.DMA((2,)),
            pltpu.SemaphoreType.DMA,
        ],
    )
    return pl.pallas_call(
        _dispatch_kernel,
        out_shape=jax.ShapeDtypeStruct((cap, D_PACK), jnp.uint32),
        grid_spec=grid_spec,
        compiler_params=_cparams(1),
        name="moe_dispatch",
    )(pos, last_tile, has_tile, f)


def _moe_kernel(te_ref, tv_ref, x_ref, w1_ref, w3_ref, w2_ref, y_ref):
    i = pl.program_id(0)

    @pl.when(tv_ref[i] != 0)
    def _():
        x = _unpack_rows(x_ref[...]).astype(BF16)
        h1 = _dot(x, w1_ref[...])
        h3 = _dot(x, w3_ref[...])
        a = (h1 * jax.nn.sigmoid(h1) * h3).astype(BF16)
        y_ref[...] = _pack_rows(_dot(a, w2_ref[...]))

    @pl.when(tv_ref[i] == 0)
    def _():
        y_ref[...] = jnp.zeros_like(y_ref)


def _moe_experts(xs, tile_expert, tile_valid, w1, w3, w2):
    cap = xs.shape[0]
    grid_spec = pltpu.PrefetchScalarGridSpec(
        num_scalar_prefetch=2,
        grid=(cap // MOE_TM,),
        in_specs=[
            pl.BlockSpec((MOE_TM, D_PACK), lambda i, te, tv: (i * tv[i], 0)),
            pl.BlockSpec((D, D_EXPERT), lambda i, te, tv: (te[i], 0)),
            pl.BlockSpec((D, D_EXPERT), lambda i, te, tv: (te[i], 0)),
            pl.BlockSpec((D_EXPERT, D), lambda i, te, tv: (te[i], 0)),
        ],
        out_specs=pl.BlockSpec((MOE_TM, D_PACK), lambda i, te, tv: (i, 0)),
    )
    return pl.pallas_call(
        _moe_kernel,
        out_shape=jax.ShapeDtypeStruct((cap, D_PACK), jnp.uint32),
        grid_spec=grid_spec,
        compiler_params=_cparams(1),
        name="moe_experts",
    )(tile_expert, tile_valid, xs, w1, w3, w2)


FIN_TM = 512


def _final_kernel(pos_ref, h_ref, rt_ref, mod_ref, lng_ref, lnb_ref, ys_hbm, o_ref, abuf, bbuf, sem):
    i = pl.program_id(0)
    n = pl.num_programs(0)
    slot = i % 2

    def start_gather(tile, sl):
        def make(r, src, second):
            buf = bbuf if second else abuf
            return pltpu.make_async_copy(ys_hbm.at[pl.ds(src, 1), :], buf.at[sl, pl.ds(r, 1), :], sem.at[sl])

        _start_row_copies(pos_ref, tile, FIN_TM, make)

    @pl.when(i == 0)
    def _():
        start_gather(0, 0)

    @pl.when(i + 1 < n)
    def _():
        start_gather(i + 1, 1 - slot)

    for buf in (abuf, bbuf):
        pltpu.make_async_copy(ys_hbm.at[pl.ds(0, FIN_TM), :], buf.at[slot], sem.at[slot]).wait()

    y = (_unpack_rows(abuf[slot]) * rt_ref[:, RT_W:RT_W + 1]
         + _unpack_rows(bbuf[slot]) * rt_ref[:, RT_W + 1:RT_W + 2])
    gate2 = mod_ref[0, 5:6, :]
    o_ref[...] = _layer_norm(ALPHA * h_ref[...] + gate2 * y, lng_ref[...], lnb_ref[...])


def _ffn_residual(h1, route, pos, ys, mod, ln_g, ln_b):
    rows = h1.shape[0]
    grid_spec = pltpu.PrefetchScalarGridSpec(
        num_scalar_prefetch=1,
        grid=(rows // FIN_TM,),
        in_specs=[
            pl.BlockSpec((FIN_TM, D), lambda i, *_: (i, 0)),
            pl.BlockSpec((FIN_TM, LANE), lambda i, *_: (i, 0)),
            pl.BlockSpec((1, 6, D), lambda i, *_: (_mod_row(FIN_TM)(i), 0, 0)),
            pl.BlockSpec((1, D), lambda i, *_: (0, 0)),
            pl.BlockSpec((1, D), lambda i, *_: (0, 0)),
            pl.BlockSpec(memory_space=pl.ANY),
        ],
        out_specs=pl.BlockSpec((FIN_TM, D), lambda i, *_: (i, 0)),
        scratch_shapes=[
            pltpu.VMEM((2, FIN_TM, D_PACK), jnp.uint32),
            pltpu.VMEM((2, FIN_TM, D_PACK), jnp.uint32),
            pltpu.SemaphoreType.DMA((2,)),
        ],
    )
    return pl.pallas_call(
        _final_kernel,
        out_shape=jax.ShapeDtypeStruct((rows, D), F32),
        grid_spec=grid_spec,
        compiler_params=_cparams(1),
        name="ffn_residual_ln",
    )(pos, h1, route, mod, ln_g, ln_b, ys)


def _moe_ffn(h1, f, route, counts, mod, w1, w3, w2, ln_g, ln_b):
    tile_expert, tile_valid, pos, fill_tile, fill_on, n_tiles = _plan(route, counts)
    xs = _dispatch(f, pos, fill_tile, fill_on, n_tiles * MOE_TM)
    ys = _moe_experts(xs, tile_expert, tile_valid, w1, w3, w2)
    return _ffn_residual(h1, route, pos, ys, mod, ln_g, ln_b)


def _router_weights(w_group, b_group, w_expert, b_expert):
    pad = LANE - N_GROUPS - N_EXPERTS
    w = jnp.concatenate([w_group, w_expert, jnp.zeros((D, pad), F32)], axis=1).astype(BF16)
    b = jnp.concatenate([b_group, b_expert, jnp.zeros((pad,), F32)])[None, :]
    return w, b


def kernel(x, c, ctx, c_ctx, ada_w, ada_b, ln_g, ln_b, gla_w_in, gla_w_gate, gla_b_gate, gla_norm_g, gla_w_out,
           nat_w_in, nat_rpb, nat_w_out, moe_w_group, moe_b_group, moe_w_expert, moe_b_expert,
           moe_w1, moe_w3, moe_w2):
    stream = (x.reshape(R_LAT, D), ctx.reshape(R_CTX, D), 0)
    cc = jnp.concatenate([c, c_ctx[None, :], jnp.zeros((SUBLANE - B - 1, D), F32)], axis=0)
    mod = _ada_table(cc, ada_w, ada_b).reshape(DEPTH, SUBLANE, 6, D)
    w1_all = moe_w1.reshape(DEPTH * N_EXPERTS * D, D_EXPERT)
    w3_all = moe_w3.reshape(DEPTH * N_EXPERTS * D, D_EXPERT)
    w2_all = moe_w2.reshape(DEPTH * N_EXPERTS * D_EXPERT, D)

    def expert_casts(layer):
        up, down = N_EXPERTS * D, N_EXPERTS * D_EXPERT
        return ((w1_all, up, layer * up), (w3_all, up, layer * up), (w2_all, down, layer * down))

    w_in = gla_w_in[0]
    w_glr = jnp.concatenate([w_in[:, GLA_MAIN:], jnp.zeros((D, LANE - 2 * GLA_RANK), F32)], axis=1).astype(BF16)
    p, glr = _proj(stream, mod[0], w_in.astype(BF16), GLA_MAIN, GLA_QK, GLA_DK ** -0.5, w_glr)
    wg = jnp.zeros((2, LANE, GLA_QK), F32)
    wg = wg.at[0, :GLA_RANK].set(gla_w_gate[0, 0]).at[1, GLA_RANK:2 * GLA_RANK].set(gla_w_gate[0, 1]).astype(BF16)
    bg = gla_b_gate[0][:, None, :]
    cos, sin = _rope_tables()
    o_f, *experts0 = _gla_scan(p, glr, wg, bg, cos, sin, False, expert_casts(0))
    o_b, *experts1 = _gla_scan(p, glr, wg, bg, cos, sin, True, expert_casts(1))
    w_r, b_r = _router_weights(moe_w_group[0], moe_b_group[0], moe_w_expert[0], moe_b_expert[0])
    h1, f, route, counts = _mix_out((o_f, o_b, p, gla_norm_g[0][None, :]), stream, mod[0], gla_w_out[0].astype(BF16),
                                    ln_g[0, 0][None, :], ln_b[0, 0][None, :], w_r, b_r, R_ALL, gla=True)
    h = _moe_ffn(h1, f, route, counts, mod[0], *experts0, ln_g[0, 1][None, :], ln_b[0, 1][None, :])

    stream = (h, h, R_LAT)
    p = _proj(stream, mod[1], nat_w_in[0].astype(BF16), 3 * D, D, NAT_DH ** -0.5)
    o = _nat_attention(p, _nat_bias_table(nat_rpb[0]))
    w_r, b_r = _router_weights(moe_w_group[1], moe_b_group[1], moe_w_expert[1], moe_b_expert[1])
    h1, f, route, counts = _mix_out((o,), stream, mod[1], nat_w_out[0].astype(BF16),
                                    ln_g[1, 0][None, :], ln_b[1, 0][None, :], w_r, b_r, R_LAT, gla=False)
    out = _moe_ffn(h1, f, route, counts, mod[1], *experts1, ln_g[1, 1][None, :], ln_b[1, 1][None, :])
    return out.reshape(B, SEQ, D)
```

```python
import functools

import jax
import jax.numpy as jnp
import numpy as np
from jax import lax
from jax.experimental import pallas as pl
from jax.experimental.pallas import tpu as pltpu

F32 = jnp.float32
BF16 = jnp.bfloat16

D = 2048
B = 4
SEQ = 4096
DEPTH = 2
GRID_W = 64
CTX = 256
R_LAT = B * SEQ
R_CTX = B * CTX
R_ALL = R_LAT + R_CTX

GLA_H = 4
GLA_DK = 256
GLA_DV = 512
GLA_CHUNK = 64
GLA_RANK = 16
GLA_QK = GLA_H * GLA_DK
GLA_VD = GLA_H * GLA_DV
GLA_MAIN = 2 * GLA_QK + 2 * GLA_VD
GLA_GATE_NORM = 16.0
ROPE_BASE = 10000.0

NAT_H = 16
NAT_DH = 128
WIN_R = 8
WIN_C = 16
NAT_QR = 4
NAT_BAND = 12
NAT_G = 8

N_GROUPS = 4
EPG = 4
N_EXPERTS = 16
D_EXPERT = 1024
MOE_TM = 256

ALPHA = (2 * DEPTH) ** 0.25
LN_EPS = 1e-5
NEG_INF = -1e30

LANE = 128
SUBLANE = 8
VMEM_LIMIT = 56 * 1024 * 1024


def _cparams(n_axes):
    return pltpu.CompilerParams(dimension_semantics=("arbitrary",) * n_axes,
                                vmem_limit_bytes=VMEM_LIMIT)


def _split_bf16(x):
    hi = x.astype(BF16)
    lo = (x - hi.astype(F32)).astype(BF16)
    return hi, lo


def _dot(a, b):
    return jnp.dot(a, b, preferred_element_type=F32)


def _dot_hi(a, b):
    ah, al = _split_bf16(a)
    bh, bl = _split_bf16(b)
    return _dot(ah, bh) + _dot(al, bh) + _dot(ah, bl)


def _dot_nt(a, b):
    return lax.dot_general(a, b, (((1,), (1,)), ((), ())), preferred_element_type=F32)


def _dot_tn(a, b):
    return lax.dot_general(a, b, (((0,), (0,)), ((), ())), preferred_element_type=F32)


D_PACK = D // 2


def _pack_rows(x):
    lo = pltpu.bitcast(x[:, :D_PACK].astype(BF16).astype(F32), jnp.uint32)
    hi = pltpu.bitcast(x[:, D_PACK:].astype(BF16).astype(F32), jnp.uint32)
    return (lo >> 16) | hi


def _unpack_rows(u):
    lo = pltpu.bitcast(u << 16, F32)
    hi = pltpu.bitcast(u & jnp.uint32(0xFFFF0000), F32)
    return jnp.concatenate([lo, hi], axis=1)


CAST_STEPS = 64


def _cast_specs(casts, step_of):
    in_specs, out_shape, out_specs, args = [], [], [], []
    for arr, n_rows, first_row in casts:
        c_rows = n_rows // CAST_STEPS
        first = first_row // c_rows
        chunk = lambda *idx: jnp.minimum(step_of(*idx), CAST_STEPS - 1)
        in_specs.append(pl.BlockSpec((c_rows, arr.shape[1]), lambda *idx, first=first: (first + chunk(*idx), 0)))
        out_shape.append(jax.ShapeDtypeStruct((n_rows, arr.shape[1]), BF16))
        out_specs.append(pl.BlockSpec((c_rows, arr.shape[1]), lambda *idx: (chunk(*idx), 0)))
        args.append(arr)
    return in_specs, out_shape, out_specs, args


def _mod_row(tile_rows):
    return lambda i: jnp.minimum((i * tile_rows) // SEQ, B)


def _stream_specs(stream, tile_rows, lag=0):
    n_lat = R_LAT // tile_rows
    base = stream[2] // tile_rows
    return (pl.BlockSpec((tile_rows, D), lambda i, *_: (jnp.clip(i - lag, 0, n_lat - 1), 0)),
            pl.BlockSpec((tile_rows, D), lambda i, *_: (base + jnp.maximum(i - lag - n_lat, 0), 0)))


ADA_TN = 1024


def _ada_kernel(c_ref, w_ref, b_ref, o_ref):
    x = c_ref[...]
    s = x * jax.nn.sigmoid(x)
    o_ref[0] = _dot_hi(s, w_ref[0]) + b_ref[0]


def _ada_table(cc, ada_w, ada_b):
    n = 6 * D
    return pl.pallas_call(
        _ada_kernel,
        out_shape=jax.ShapeDtypeStruct((DEPTH, SUBLANE, n), F32),
        grid=(DEPTH, n // ADA_TN),
        in_specs=[
            pl.BlockSpec((SUBLANE, D), lambda l, j: (0, 0)),
            pl.BlockSpec((1, D, ADA_TN), lambda l, j: (l, 0, j)),
            pl.BlockSpec((1, 1, ADA_TN), lambda l, j: (l, 0, j)),
        ],
        out_specs=pl.BlockSpec((1, SUBLANE, ADA_TN), lambda l, j: (l, 0, j)),
        compiler_params=_cparams(2),
        name="ada_table",
    )(cc, ada_w, ada_b.reshape(DEPTH, 1, n))


PROJ_TM = 1024
PROJ_TN = 1024


def _proj_kernel(h_ref, hc_ref, mod_ref, w_ref, *rest, with_extra, q_tiles, q_scale):
    if with_extra:
        wx_ref, o_ref, ox_ref, a_scr = rest
    else:
        o_ref, a_scr = rest
    j = pl.program_id(1)

    @pl.when(j == 0)
    def _():
        sh = mod_ref[0, 0:1, :]
        sc = mod_ref[0, 1:2, :]
        h = jnp.where(pl.program_id(0) < R_LAT // PROJ_TM, h_ref[...], hc_ref[...])
        a_scr[...] = (h * (1.0 + sc) + sh).astype(BF16)
        if with_extra:
            ox_ref[...] = _dot(a_scr[...], wx_ref[...])

    scale = jnp.where(j < q_tiles, q_scale, 1.0)
    o_ref[...] = (_dot(a_scr[...], w_ref[...]) * scale).astype(o_ref.dtype)


def _proj(stream, mod, w, n, q_cols, q_scale, w_extra=None):
    rows = R_ALL
    with_extra = w_extra is not None
    in_specs = [
        *_stream_specs(stream, PROJ_TM),
        pl.BlockSpec((1, 6, D), lambda i, j: (_mod_row(PROJ_TM)(i), 0, 0)),
        pl.BlockSpec((D, PROJ_TN), lambda i, j: (0, j)),
    ]
    out_shape = [jax.ShapeDtypeStruct((rows, n), BF16)]
    out_specs = [pl.BlockSpec((PROJ_TM, PROJ_TN), lambda i, j: (i, j))]
    args = [stream[0], stream[1], mod, w]
    if with_extra:
        in_specs.append(pl.BlockSpec((D, LANE), lambda i, j: (0, 0)))
        out_shape.append(jax.ShapeDtypeStruct((rows, LANE), F32))
        out_specs.append(pl.BlockSpec((PROJ_TM, LANE), lambda i, j: (i, 0)))
        args.append(w_extra)
    out = pl.pallas_call(
        functools.partial(_proj_kernel, with_extra=with_extra, q_tiles=q_cols // PROJ_TN, q_scale=q_scale),
        out_shape=out_shape,
        grid=(rows // PROJ_TM, n // PROJ_TN),
        in_specs=in_specs,
        out_specs=out_specs,
        scratch_shapes=[pltpu.VMEM((PROJ_TM, D), BF16)],
        compiler_params=_cparams(2),
        name="mod_proj",
    )(*args)
    return out if with_extra else out[0]


GLA_TB = 256
GLA_STEPS = 1 + SEQ // GLA_TB


def _rope_tables():
    half = GLA_DK // 4
    freqs = ROPE_BASE ** (-np.arange(half, dtype=np.float64) / half)
    t = np.arange(SEQ)
    ang_r = (t // GRID_W)[:, None] * freqs
    ang_c = (t % GRID_W)[:, None] * freqs
    cos = np.concatenate([np.cos(ang_r)] * 2 + [np.cos(ang_c)] * 2, axis=1)
    sin = np.concatenate([-np.sin(ang_r), np.sin(ang_r), -np.sin(ang_c), np.sin(ang_c)], axis=1)
    cos = np.concatenate([cos, np.ones((GLA_TB, GLA_DK))], axis=0)
    sin = np.concatenate([sin, np.zeros((GLA_TB, GLA_DK))], axis=0)
    return jnp.asarray(cos, F32), jnp.asarray(sin, F32)


def _rope(x, cos, sin):
    half = LANE // 2
    parts = [pltpu.roll(x[:, g * LANE:(g + 1) * LANE], half, 1) for g in range(x.shape[1] // LANE)]
    return x * cos + jnp.concatenate(parts, axis=1) * sin


def _gla_kernel(q_ref, k_ref, v_ref, glr_ref, wg_ref, bg_ref, cos_ref, sin_ref, *rest, reverse, n_cast):
    cast_in, o_ref, cast_out, s_scr = rest[:n_cast], rest[n_cast], rest[n_cast + 1:-1], rest[-1]
    for src, dst in zip(cast_in, cast_out):
        dst[...] = src[...].astype(BF16)

    @pl.when(pl.program_id(1) == 0)
    def _():
        s_scr[...] = jnp.zeros_like(s_scr)

    row = lax.broadcasted_iota(jnp.int32, (GLA_CHUNK, GLA_CHUNK), 0)
    col = lax.broadcasted_iota(jnp.int32, (GLA_CHUNK, GLA_CHUNK), 1)
    keep = (col >= row) if reverse else (col <= row)
    tri = jnp.where(keep, 1.0, 0.0).astype(BF16)
    ones = jnp.ones((GLA_CHUNK, LANE), BF16)
    n_chunks = GLA_TB // GLA_CHUNK
    order = range(n_chunks - 1, -1, -1) if reverse else range(n_chunks)
    for ci in order:
        sl = slice(ci * GLA_CHUNK, (ci + 1) * GLA_CHUNK)
        cos = cos_ref[sl, :]
        sin = sin_ref[sl, :]
        z = _dot(glr_ref[sl, :].astype(BF16), wg_ref[0]) + bg_ref[0]
        g = (jnp.minimum(z, 0.0) - jnp.log1p(jnp.exp(-jnp.abs(z)))) * (1.0 / GLA_GATE_NORM)
        g_hi, g_lo = _split_bf16(g)
        cum_all = _dot(tri, g_hi) + _dot(tri, g_lo)
        decay_all = jnp.exp(_dot_tn(g_hi, ones) + _dot_tn(g_lo, ones))
        for hd in range(GLA_H):
            ks = slice(hd * GLA_DK, (hd + 1) * GLA_DK)
            vs = slice(hd * GLA_DV, (hd + 1) * GLA_DV)
            q = _rope(q_ref[sl, ks].astype(F32), cos, sin)
            k = _rope(k_ref[sl, ks].astype(F32), cos, sin)
            v = v_ref[sl, vs]
            cum = cum_all[:, ks]
            tot = cum[0:1, :] if reverse else cum[GLA_CHUNK - 1:GLA_CHUNK, :]
            q_dec = (q * jnp.exp(cum)).astype(BF16)
            k_inv = (k * jnp.exp(-cum)).astype(BF16)
            k_end = (k * jnp.exp(tot - cum)).astype(BF16)
            att = jnp.where(keep, _dot_nt(q_dec, k_inv), 0.0).astype(BF16)
            state = s_scr[hd]
            o_ref[sl, vs] = (_dot(att, v) + _dot(q_dec, state.astype(BF16))).astype(o_ref.dtype)
            decay = jnp.concatenate([decay_all[ks, :]] * (GLA_DV // LANE), axis=1)
            s_scr[hd] = state * decay + _dot_tn(k_end, v)


def _gla_scan(p, glr, wg, bg, cos, sin, reverse, casts):
    lat_blocks = SEQ // GLA_TB

    def tblock(s):
        if reverse:
            return jnp.where(s == 0, 0, GLA_STEPS - s)
        return s

    def rows(b, s):
        tb = tblock(s)
        return jnp.where(tb == 0, R_LAT // GLA_TB + b, b * lat_blocks + tb - 1)

    def rope_rows(s):
        tb = tblock(s)
        return jnp.where(tb == 0, lat_blocks, tb - 1)

    d = 1 if reverse else 0
    c_in, c_shape, c_out, c_args = _cast_specs(casts, lambda b, s: b * GLA_STEPS + s)
    return pl.pallas_call(
        functools.partial(_gla_kernel, reverse=reverse, n_cast=len(casts)),
        out_shape=[jax.ShapeDtypeStruct((R_ALL, GLA_VD), BF16), *c_shape],
        grid=(B, GLA_STEPS),
        in_specs=[
            pl.BlockSpec((GLA_TB, GLA_QK), lambda b, s: (rows(b, s), 0)),
            pl.BlockSpec((GLA_TB, GLA_QK), lambda b, s: (rows(b, s), 1)),
            pl.BlockSpec((GLA_TB, GLA_VD), lambda b, s: (rows(b, s), 2 * GLA_QK // GLA_VD)),
            pl.BlockSpec((GLA_TB, LANE), lambda b, s: (rows(b, s), 0)),
            pl.BlockSpec((1, LANE, GLA_QK), lambda b, s: (d, 0, 0)),
            pl.BlockSpec((1, 1, GLA_QK), lambda b, s: (d, 0, 0)),
            pl.BlockSpec((GLA_TB, GLA_DK), lambda b, s: (rope_rows(s), 0)),
            pl.BlockSpec((GLA_TB, GLA_DK), lambda b, s: (rope_rows(s), 0)),
            *c_in,
        ],
        out_specs=[pl.BlockSpec((GLA_TB, GLA_VD), lambda b, s: (rows(b, s), 0)), *c_out],
        scratch_shapes=[pltpu.VMEM((GLA_H, GLA_DK, GLA_DV), F32)],
        compiler_params=_cparams(2),
        name="gla_scan_bwd" if reverse else "gla_scan_fwd",
    )(p, p, p, glr, wg, bg, cos, sin, *c_args)


def _nat_bias_table(rpb):
    qc = np.arange(GRID_W)[:, None]
    kc = np.arange(GRID_W)[None, :]
    cs = np.clip(qc - WIN_C // 2, 0, GRID_W - WIN_C)
    col_ok = (kc >= cs) & (kc < cs + WIN_C)
    dc = np.clip(kc - qc + WIN_C - 1, 0, 2 * WIN_C - 2)
    pick = (dc[:, :, None] == np.arange(2 * WIN_C - 1)).astype(np.float32)
    tc = jnp.einsum('had,qkd->hqak', rpb.astype(F32), jnp.asarray(pick), precision=lax.Precision.HIGHEST)
    tc = jnp.where(col_ok[None, :, None, :], tc, NEG_INF)
    n_dr = 2 * WIN_R - 1
    pad = NAT_BAND - 1
    strip = jnp.pad(tc.reshape(NAT_H, GRID_W, n_dr * GRID_W), ((0, 0), (0, 0), (pad * GRID_W, pad * GRID_W)),
                    constant_values=NEG_INF)
    rows_n = SEQ // GRID_W
    half = WIN_R // 2
    n_keys = NAT_BAND * GRID_W
    bands, valid = [], np.zeros((3, NAT_QR, 1, NAT_BAND, 1), bool)
    for p_i, (r0, bs) in enumerate(((0, 0), (NAT_QR, NAT_QR - half), (rows_n - NAT_QR, rows_n - NAT_BAND))):
        for i in range(NAT_QR):
            qr = r0 + i
            rs = min(max(qr - half, 0), rows_n - WIN_R)
            first = (pad + bs - qr + WIN_R - 1) * GRID_W
            bands.append(strip[:, :, first:first + n_keys])
            valid[p_i, i, 0, :, 0] = [rs <= kr < rs + WIN_R for kr in range(bs, bs + NAT_BAND)]
    tab = jnp.stack(bands).reshape(3, NAT_QR, NAT_H, GRID_W, n_keys)
    valid = np.broadcast_to(valid, (3, NAT_QR, GRID_W, NAT_BAND, GRID_W)).reshape(3, NAT_QR, 1, GRID_W, n_keys)
    return jnp.where(valid, tab, NEG_INF)


def _nat_kernel(q_ref, k_ref, v_ref, kc_ref, vc_ref, bias_ref, o_ref):
    rb = pl.program_id(2)
    rows_n = SEQ // GRID_W
    band_start = jnp.clip(rb * NAT_QR - WIN_R // 2, 0, rows_n - NAT_BAND)
    start = pl.multiple_of(band_start * GRID_W, GRID_W)
    n_keys = NAT_BAND * GRID_W
    for g in range(NAT_G):
        hs = slice(g * NAT_DH, (g + 1) * NAT_DH)
        q = q_ref[:, hs]
        bias = bias_ref[0, :, g].reshape(NAT_QR * GRID_W, n_keys)
        s_loc = _dot_nt(q, k_ref[pl.ds(start, n_keys), hs]) + bias
        s_ctx = _dot_nt(q, kc_ref[:, hs])
        m = jnp.maximum(jnp.max(s_loc, axis=1, keepdims=True), jnp.max(s_ctx, axis=1, keepdims=True))
        p_loc = jnp.exp(s_loc - m)
        p_ctx = jnp.exp(s_ctx - m)
        denom = jnp.sum(p_loc, axis=1, keepdims=True) + jnp.sum(p_ctx, axis=1, keepdims=True)
        o = _dot(p_loc.astype(BF16), v_ref[pl.ds(start, n_keys), hs]) + _dot(p_ctx.astype(BF16), vc_ref[:, hs])
        o_ref[:, hs] = (o / denom).astype(o_ref.dtype)


def _nat_attention(p, bias_tab):
    n_rb = SEQ // (NAT_QR * GRID_W)
    tq = NAT_QR * GRID_W
    width = NAT_G * NAT_DH
    hq = D // width

    def pattern(rb):
        return jnp.where(rb == 0, 0, jnp.where(rb == n_rb - 1, 2, 1))

    return pl.pallas_call(
        _nat_kernel,
        out_shape=jax.ShapeDtypeStruct((R_LAT, D), BF16),
        grid=(B, NAT_H // NAT_G, n_rb),
        in_specs=[
            pl.BlockSpec((tq, width), lambda b, h, rb: (b * n_rb + rb, h)),
            pl.BlockSpec((SEQ, width), lambda b, h, rb: (b, hq + h)),
            pl.BlockSpec((SEQ, width), lambda b, h, rb: (b, 2 * hq + h)),
            pl.BlockSpec((CTX, width), lambda b, h, rb: (R_LAT // CTX + b, hq + h)),
            pl.BlockSpec((CTX, width), lambda b, h, rb: (R_LAT // CTX + b, 2 * hq + h)),
            pl.BlockSpec((1, NAT_QR, NAT_G, GRID_W, NAT_BAND * GRID_W), lambda b, h, rb: (pattern(rb), 0, h, 0, 0)),
        ],
        out_specs=pl.BlockSpec((tq, width), lambda b, h, rb: (b * n_rb + rb, h)),
        compiler_params=_cparams(3),
        name="nat_attention",
    )(p, p, p, p, p, bias_tab)


MIX_TM = 256


def _layer_norm(u, g, b):
    mu = jnp.mean(u, axis=-1, keepdims=True)
    var = jnp.mean(jnp.square(u - mu), axis=-1, keepdims=True)
    return (u - mu) * lax.rsqrt(var + LN_EPS) * g + b


def _mix_out_kernel(*refs, gla, lat_tiles):
    if gla:
        of_ref, ob_ref, r_ref, ng_ref = refs[:4]
        refs = refs[4:]
    else:
        x_ref = refs[0]
        refs = refs[1:]
    (h_ref, hc_ref, mod_ref, w_ref, lng_ref, lnb_ref, wr_ref, br_ref,
     h1_ref, f_ref, rt_ref, cnt_ref, y_even, y_odd, cnt_scr) = refs
    s = pl.program_id(0)

    @pl.when(s == 0)
    def _():
        cnt_scr[...] = jnp.zeros_like(cnt_scr)
        y_odd[...] = jnp.zeros_like(y_odd)

    def step(y_new, y_prev):
        if gla:
            parts = []
            for hd in range(GLA_H):
                cs = slice(hd * GLA_DV, (hd + 1) * GLA_DV)
                o = of_ref[:, cs].astype(F32) + ob_ref[:, cs].astype(F32)
                o = o * lax.rsqrt(jnp.mean(jnp.square(o), axis=-1, keepdims=True) + LN_EPS) * ng_ref[...]
                r = r_ref[:, cs].astype(F32)
                parts.append((o * (r * jax.nn.sigmoid(r))).astype(BF16))
            x = jnp.concatenate(parts, axis=1)
        else:
            x = x_ref[...]
        y_new[...] = _dot(x, w_ref[...])

        gate1 = mod_ref[0, 2:3, :]
        h_in = jnp.where(s - 1 < lat_tiles, h_ref[...], hc_ref[...])
        h1 = _layer_norm(ALPHA * h_in + gate1 * y_prev[...], lng_ref[...], lnb_ref[...])
        h1_ref[...] = h1
        f = h1 * (1.0 + mod_ref[0, 4:5, :]) + mod_ref[0, 3:4, :]
        f_ref[...] = _pack_rows(f)
        live = jnp.where(s > 0, 1.0, 0.0)
        logits = _dot(f.astype(BF16), wr_ref[...]) + br_ref[...]
        route, counts = _route_tile(logits, cnt_scr[...], live)
        rt_ref[...] = route
        cnt_scr[...] = counts
        cnt_ref[...] = jnp.broadcast_to(counts, cnt_ref.shape)

    @pl.when(s % 2 == 0)
    def _():
        step(y_even, y_odd)

    @pl.when(s % 2 == 1)
    def _():
        step(y_odd, y_even)


RT_E, RT_W, RT_RANK = 0, 2, 4


def _route_tile(logits, run, live):
    tm = logits.shape[0]
    lane = lax.broadcasted_iota(jnp.int32, (tm, LANE), 1).astype(F32)

    def first_max(vals):
        top = jnp.max(vals, axis=1, keepdims=True)
        return top, jnp.min(jnp.where(vals == top, lane, float(LANE)), axis=1, keepdims=True)

    gl = jnp.where(lane < N_GROUPS, logits, NEG_INF)
    g_max, g_idx = first_max(gl)
    g_p = 1.0 / jnp.sum(jnp.exp(gl - g_max), axis=1, keepdims=True)
    lo = N_GROUPS + EPG * g_idx
    el = jnp.where((lane >= lo) & (lane < lo + EPG), logits, NEG_INF)
    e1, i1 = first_max(el)
    e2, i2 = first_max(jnp.where(lane == i1, NEG_INF, el))
    t = jnp.exp(e2 - e1)
    p1 = 1.0 / (1.0 + t)
    x1 = i1 - N_GROUPS
    x2 = i2 - N_GROUPS
    onehot = jnp.where((lane == x1) | (lane == x2), 1.0, 0.0)
    row = lax.broadcasted_iota(jnp.int32, (tm, tm), 0)
    col = lax.broadcasted_iota(jnp.int32, (tm, tm), 1)
    before = jnp.where(col < row, 1.0, 0.0).astype(BF16)
    seen = _dot(before, onehot.astype(BF16)) + run
    rank1 = jnp.sum(jnp.where(lane == x1, seen, 0.0), axis=1, keepdims=True)
    rank2 = jnp.sum(jnp.where(lane == x2, seen, 0.0), axis=1, keepdims=True)
    route = jnp.zeros((tm, LANE), F32)
    for k, val in ((RT_E, x1), (RT_E + 1, x2), (RT_W, g_p * p1), (RT_W + 1, g_p * p1 * t),
                   (RT_RANK, rank1), (RT_RANK + 1, rank2)):
        route = jnp.where(lane == k, val, route)
    return route, run + live * jnp.sum(onehot, axis=0, keepdims=True)


def _mix_out(mix_inputs, stream, mod, w_out, ln_g, ln_b, w_router, b_router, rows, gla):
    n_tiles = rows // MIX_TM
    row_spec = lambda width: pl.BlockSpec((MIX_TM, width), lambda i: (jnp.minimum(i, n_tiles - 1), 0))
    done_spec = lambda width: pl.BlockSpec((MIX_TM, width), lambda i: (jnp.maximum(i - 1, 0), 0))
    full = lambda a: pl.BlockSpec(a.shape, lambda i: (0,) * a.ndim)
    if gla:
        o_f, o_b, p, norm_g = mix_inputs
        in_specs = [row_spec(GLA_VD), row_spec(GLA_VD),
                    pl.BlockSpec((MIX_TM, GLA_VD),
                                 lambda i: (jnp.minimum(i, n_tiles - 1), (2 * GLA_QK + GLA_VD) // GLA_VD)),
                    full(norm_g)]
        args = [o_f, o_b, p, norm_g]
    else:
        (x,) = mix_inputs
        in_specs = [row_spec(D)]
        args = [x]
    in_specs += [*_stream_specs(stream, MIX_TM, lag=1),
                 pl.BlockSpec((1, 6, D), lambda i: (_mod_row(MIX_TM)(jnp.maximum(i - 1, 0)), 0, 0)),
                 full(w_out), full(ln_g), full(ln_b), full(w_router), full(b_router)]
    args += [stream[0], stream[1], mod, w_out, ln_g, ln_b, w_router, b_router]
    return pl.pallas_call(
        functools.partial(_mix_out_kernel, gla=gla, lat_tiles=R_LAT // MIX_TM),
        out_shape=[jax.ShapeDtypeStruct((rows, D), F32), jax.ShapeDtypeStruct((rows, D_PACK), jnp.uint32),
                   jax.ShapeDtypeStruct((rows, LANE), F32), jax.ShapeDtypeStruct((SUBLANE, LANE), F32)],
        grid=(n_tiles + 1,),
        in_specs=in_specs,
        out_specs=[done_spec(D), done_spec(D_PACK), done_spec(LANE), pl.BlockSpec((SUBLANE, LANE), lambda i: (0, 0))],
        scratch_shapes=[pltpu.VMEM((MIX_TM, D), F32), pltpu.VMEM((MIX_TM, D), F32), pltpu.VMEM((1, LANE), F32)],
        compiler_params=_cparams(1),
        name="gla_out_ln" if gla else "nat_out_ln",
    )(*args)


def _plan(route, counts_row):
    t = route.shape[0]
    e = route[:, RT_E:RT_E + 2].astype(jnp.int32)
    rank = route[:, RT_RANK:RT_RANK + 2].astype(jnp.int32)
    counts = counts_row[0, :N_EXPERTS].astype(jnp.int32)
    tiles_per = (counts + MOE_TM - 1) // MOE_TM
    tile_end = jnp.cumsum(tiles_per)
    tile_start = tile_end - tiles_per
    n_tiles = 2 * t // MOE_TM + N_EXPERTS
    ids = jnp.arange(n_tiles, dtype=jnp.int32)
    used = tile_end[-1]
    tile_valid = (ids < used).astype(jnp.int32)
    lookup = jnp.minimum(ids, used - 1)
    tile_expert = jnp.sum((tile_end[None, :] <= lookup[:, None]).astype(jnp.int32), axis=1)
    tile_expert = jnp.minimum(tile_expert, N_EXPERTS - 1).astype(jnp.int32)
    base = tile_start * MOE_TM
    experts = jnp.arange(N_EXPERTS, dtype=jnp.int32)
    pos = jnp.sum(jnp.where(e[..., None] == experts, base, 0), axis=-1) + rank
    trailing = n_tiles - 1 - experts
    fill_tile = jnp.concatenate([jnp.maximum(tile_end - 1, 0), trailing]).astype(jnp.int32)
    fill_on = jnp.concatenate([tiles_per > 0, trailing >= used]).astype(jnp.int32)
    return tile_expert, tile_valid, pos.reshape(-1).astype(jnp.int32), fill_tile, fill_on, n_tiles


def _start_row_copies(pos_ref, tile, rows, make):
    base = 2 * tile * rows
    for r in range(rows):
        make(r, pos_ref[base + 2 * r], False).start()
        make(r, pos_ref[base + 2 * r + 1], True).start()


DISP_TM = 256


def _dispatch_kernel(pos_ref, lt_ref, ht_ref, f_ref, xs_hbm, sbuf, zbuf, sem, zsem):
    i = pl.program_id(0)
    n = pl.num_programs(0)
    slot = i % 2

    def fill(e):
        return pltpu.make_async_copy(zbuf, xs_hbm.at[pl.ds(lt_ref[e] * MOE_TM, MOE_TM), :], zsem)

    @pl.when(i == 0)
    def _():
        zbuf[...] = jnp.zeros_like(zbuf)
        for e in range(2 * N_EXPERTS):
            @pl.when(ht_ref[e] != 0)
            def _():
                fill(e).start()
        for e in range(2 * N_EXPERTS):
            @pl.when(ht_ref[e] != 0)
            def _():
                fill(e).wait()

    def wait_scatter(sl):
        for _ in range(2):
            pltpu.make_async_copy(sbuf.at[sl], xs_hbm.at[pl.ds(0, DISP_TM), :], sem.at[sl]).wait()

    @pl.when(i >= 2)
    def _():
        wait_scatter(slot)

    sbuf[slot] = f_ref[...]

    def make(r, dst, second):
        return pltpu.make_async_copy(sbuf.at[slot, pl.ds(r, 1), :], xs_hbm.at[pl.ds(dst, 1), :], sem.at[slot])

    _start_row_copies(pos_ref, i, DISP_TM, make)

    @pl.when(i == n - 1)
    def _():
        wait_scatter(slot)

        @pl.when(n >= 2)
        def _():
            wait_scatter(1 - slot)


def _dispatch(f, pos, last_tile, has_tile, cap):
    t = f.shape[0]
    grid_spec = pltpu.PrefetchScalarGridSpec(
        num_scalar_prefetch=3,
        grid=(t // DISP_TM,),
        in_specs=[pl.BlockSpec((DISP_TM, D_PACK), lambda i, *_: (i, 0))],
        out_specs=pl.BlockSpec(memory_space=pl.ANY),
        scratch_shapes=[
            pltpu.VMEM((2, DISP_TM, D_PACK), jnp.uint32),
            pltpu.VMEM((MOE_TM, D_PACK), jnp.uint32),
            pltpu.SemaphoreType.DMA((2,)),
            pltpu.SemaphoreType.DMA,
        ],
    )
    return pl.pallas_call(
        _dispatch_kernel,
        out_shape=jax.ShapeDtypeStruct((cap, D_PACK), jnp.uint32),
        grid_spec=grid_spec,
        compiler_params=_cparams(1),
        name="moe_dispatch",
    )(pos, last_tile, has_tile, f)


def _moe_kernel(te_ref, tv_ref, x_ref, w1_ref, w3_ref, w2_ref, y_ref):
    i = pl.program_id(0)

    @pl.when(tv_ref[i] != 0)
    def _():
        x = _unpack_rows(x_ref[...]).astype(BF16)
        h1 = _dot(x, w1_ref[...])
        h3 = _dot(x, w3_ref[...])
        a = (h1 * jax.nn.sigmoid(h1) * h3).astype(BF16)
        y_ref[...] = _pack_rows(_dot(a, w2_ref[...]))

    @pl.when(tv_ref[i] == 0)
    def _():
        y_ref[...] = jnp.zeros_like(y_ref)


def _moe_experts(xs, tile_expert, tile_valid, w1, w3, w2):
    cap = xs.shape[0]
    grid_spec = pltpu.PrefetchScalarGridSpec(
        num_scalar_prefetch=2,
        grid=(cap // MOE_TM,),
        in_specs=[
            pl.BlockSpec((MOE_TM, D_PACK), lambda i, te, tv: (i * tv[i], 0)),
            pl.BlockSpec((D, D_EXPERT), lambda i, te, tv: (te[i], 0)),
            pl.BlockSpec((D, D_EXPERT), lambda i, te, tv: (te[i], 0)),
            pl.BlockSpec((D_EXPERT, D), lambda i, te, tv: (te[i], 0)),
        ],
        out_specs=pl.BlockSpec((MOE_TM, D_PACK), lambda i, te, tv: (i, 0)),
    )
    return pl.pallas_call(
        _moe_kernel,
        out_shape=jax.ShapeDtypeStruct((cap, D_PACK), jnp.uint32),
        grid_spec=grid_spec,
        compiler_params=_cparams(1),
        name="moe_experts",
    )(tile_expert, tile_valid, xs, w1, w3, w2)


FIN_TM = 256
FIN_AHEAD = 2


def _final_kernel(pos_ref, h_ref, rt_ref, mod_ref, lng_ref, lnb_ref, ys_hbm, o_ref, abuf, bbuf, sem):
    i = pl.program_id(0)
    n = pl.num_programs(0)
    n_slots = FIN_AHEAD + 1
    slot = i % n_slots

    def start_gather(tile, sl):
        def make(r, src, second):
            buf = bbuf if second else abuf
            return pltpu.make_async_copy(ys_hbm.at[pl.ds(src, 1), :], buf.at[sl, pl.ds(r, 1), :], sem.at[sl])

        _start_row_copies(pos_ref, tile, FIN_TM, make)

    for t in range(FIN_AHEAD):
        @pl.when((i == 0) & (t < n))
        def _():
            start_gather(t, t)

    @pl.when(i + FIN_AHEAD < n)
    def _():
        start_gather(i + FIN_AHEAD, (i + FIN_AHEAD) % n_slots)

    for buf in (abuf, bbuf):
        pltpu.make_async_copy(ys_hbm.at[pl.ds(0, FIN_TM), :], buf.at[slot], sem.at[slot]).wait()

    y = (_unpack_rows(abuf[slot]) * rt_ref[:, RT_W:RT_W + 1]
         + _unpack_rows(bbuf[slot]) * rt_ref[:, RT_W + 1:RT_W + 2])
    gate2 = mod_ref[0, 5:6, :]
    o_ref[...] = _layer_norm(ALPHA * h_ref[...] + gate2 * y, lng_ref[...], lnb_ref[...])


def _ffn_residual(h1, route, pos, ys, mod, ln_g, ln_b):
    rows = h1.shape[0]
    grid_spec = pltpu.PrefetchScalarGridSpec(
        num_scalar_prefetch=1,
        grid=(rows // FIN_TM,),
        in_specs=[
            pl.BlockSpec((FIN_TM, D), lambda i, *_: (i, 0)),
            pl.BlockSpec((FIN_TM, LANE), lambda i, *_: (i, 0)),
            pl.BlockSpec((1, 6, D), lambda i, *_: (_mod_row(FIN_TM)(i), 0, 0)),
            pl.BlockSpec((1, D), lambda i, *_: (0, 0)),
            pl.BlockSpec((1, D), lambda i, *_: (0, 0)),
            pl.BlockSpec(memory_space=pl.ANY),
        ],
        out_specs=pl.BlockSpec((FIN_TM, D), lambda i, *_: (i, 0)),
        scratch_shapes=[
            pltpu.VMEM((FIN_AHEAD + 1, FIN_TM, D_PACK), jnp.uint32),
            pltpu.VMEM((FIN_AHEAD + 1, FIN_TM, D_PACK), jnp.uint32),
            pltpu.SemaphoreType.DMA((FIN_AHEAD + 1,)),
        ],
    )
    return pl.pallas_call(
        _final_kernel,
        out_shape=jax.ShapeDtypeStruct((rows, D), F32),
        grid_spec=grid_spec,
        compiler_params=_cparams(1),
        name="ffn_residual_ln",
    )(pos, h1, route, mod, ln_g, ln_b, ys)


def _moe_ffn(h1, f, route, counts, mod, w1, w3, w2, ln_g, ln_b):
    tile_expert, tile_valid, pos, fill_tile, fill_on, n_tiles = _plan(route, counts)
    xs = _dispatch(f, pos, fill_tile, fill_on, n_tiles * MOE_TM)
    ys = _moe_experts(xs, tile_expert, tile_valid, w1, w3, w2)
    return _ffn_residual(h1, route, pos, ys, mod, ln_g, ln_b)


def _router_weights(w_group, b_group, w_expert, b_expert):
    pad = LANE - N_GROUPS - N_EXPERTS
    w = jnp.concatenate([w_group, w_expert, jnp.zeros((D, pad), F32)], axis=1).astype(BF16)
    b = jnp.concatenate([b_group, b_expert, jnp.zeros((pad,), F32)])[None, :]
    return w, b


def kernel(x, c, ctx, c_ctx, ada_w, ada_b, ln_g, ln_b, gla_w_in, gla_w_gate, gla_b_gate, gla_norm_g, gla_w_out,
           nat_w_in, nat_rpb, nat_w_out, moe_w_group, moe_b_group, moe_w_expert, moe_b_expert,
           moe_w1, moe_w3, moe_w2):
    stream = (x.reshape(R_LAT, D), ctx.reshape(R_CTX, D), 0)
    cc = jnp.concatenate([c, c_ctx[None, :], jnp.zeros((SUBLANE - B - 1, D), F32)], axis=0)
    mod = _ada_table(cc, ada_w, ada_b).reshape(DEPTH, SUBLANE, 6, D)
    w1_all = moe_w1.reshape(DEPTH * N_EXPERTS * D, D_EXPERT)
    w3_all = moe_w3.reshape(DEPTH * N_EXPERTS * D, D_EXPERT)
    w2_all = moe_w2.reshape(DEPTH * N_EXPERTS * D_EXPERT, D)

    def expert_casts(layer):
        up, down = N_EXPERTS * D, N_EXPERTS * D_EXPERT
        return ((w1_all, up, layer * up), (w3_all, up, layer * up), (w2_all, down, layer * down))

    w_in = gla_w_in[0]
    w_glr = jnp.concatenate([w_in[:, GLA_MAIN:], jnp.zeros((D, LANE - 2 * GLA_RANK), F32)], axis=1).astype(BF16)
    p, glr = _proj(stream, mod[0], w_in.astype(BF16), GLA_MAIN, GLA_QK, GLA_DK ** -0.5, w_glr)
    wg = jnp.zeros((2, LANE, GLA_QK), F32)
    wg = wg.at[0, :GLA_RANK].set(gla_w_gate[0, 0]).at[1, GLA_RANK:2 * GLA_RANK].set(gla_w_gate[0, 1]).astype(BF16)
    bg = gla_b_gate[0][:, None, :]
    cos, sin = _rope_tables()
    o_f, *experts0 = _gla_scan(p, glr, wg, bg, cos, sin, False, expert_casts(0))
    o_b, *experts1 = _gla_scan(p, glr, wg, bg, cos, sin, True, expert_casts(1))
    w_r, b_r = _router_weights(moe_w_group[0], moe_b_group[0], moe_w_expert[0], moe_b_expert[0])
    h1, f, route, counts = _mix_out((o_f, o_b, p, gla_norm_g[0][None, :]), stream, mod[0], gla_w_out[0].astype(BF16),
                                    ln_g[0, 0][None, :], ln_b[0, 0][None, :], w_r, b_r, R_ALL, gla=True)
    h = _moe_ffn(h1, f, route, counts, mod[0], *experts0, ln_g[0, 1][None, :], ln_b[0, 1][None, :])

    stream = (h, h, R_LAT)
    p = _proj(stream, mod[1], nat_w_in[0].astype(BF16), 3 * D, D, NAT_DH ** -0.5)
    o = _nat_attention(p, _nat_bias_table(nat_rpb[0]))
    w_r, b_r = _router_weights(moe_w_group[1], moe_b_group[1], moe_w_expert[1], moe_b_expert[1])
    h1, f, route, counts = _mix_out((o,), stream, mod[1], nat_w_out[0].astype(BF16),
                                    ln_g[1, 0][None, :], ln_b[1, 0][None, :], w_r, b_r, R_LAT, gla=False)
    out = _moe_ffn(h1, f, route, counts, mod[1], *experts1, ln_g[1, 1][None, :], ln_b[1, 1][None, :])
    return out.reshape(B, SEQ, D)
```

```python
import functools

import jax
import jax.numpy as jnp
import numpy as np
from jax import lax
from jax.experimental import pallas as pl
from jax.experimental.pallas import tpu as pltpu

F32 = jnp.float32
BF16 = jnp.bfloat16

D = 2048
B = 4
SEQ = 4096
DEPTH = 2
GRID_W = 64
CTX = 256
R_LAT = B * SEQ
R_CTX = B * CTX
R_ALL = R_LAT + R_CTX

GLA_H = 4
GLA_DK = 256
GLA_DV = 512
GLA_CHUNK = 64
GLA_RANK = 16
GLA_QK = GLA_H * GLA_DK
GLA_VD = GLA_H * GLA_DV
GLA_MAIN = 2 * GLA_QK + 2 * GLA_VD
GLA_GATE_NORM = 16.0
ROPE_BASE = 10000.0

NAT_H = 16
NAT_DH = 128
WIN_R = 8
WIN_C = 16
NAT_QR = 4
NAT_BAND = 12
NAT_G = 8

N_GROUPS = 4
EPG = 4
N_EXPERTS = 16
D_EXPERT = 1024
MOE_TM = 256

ALPHA = (2 * DEPTH) ** 0.25
LN_EPS = 1e-5
NEG_INF = -1e30

LANE = 128
SUBLANE = 8
VMEM_LIMIT = 56 * 1024 * 1024


def _cparams(n_axes):
    return pltpu.CompilerParams(dimension_semantics=("arbitrary",) * n_axes,
                                vmem_limit_bytes=VMEM_LIMIT)


def _split_bf16(x):
    hi = x.astype(BF16)
    lo = (x - hi.astype(F32)).astype(BF16)
    return hi, lo


def _dot(a, b):
    return jnp.dot(a, b, preferred_element_type=F32)


def _dot_hi(a, b):
    ah, al = _split_bf16(a)
    bh, bl = _split_bf16(b)
    return _dot(ah, bh) + _dot(al, bh) + _dot(ah, bl)


def _dot_nt(a, b):
    return lax.dot_general(a, b, (((1,), (1,)), ((), ())), preferred_element_type=F32)


def _dot_tn(a, b):
    return lax.dot_general(a, b, (((0,), (0,)), ((), ())), preferred_element_type=F32)


D_PACK = D // 2


def _pack_rows(x):
    lo = pltpu.bitcast(x[:, :D_PACK].astype(BF16).astype(F32), jnp.uint32)
    hi = pltpu.bitcast(x[:, D_PACK:].astype(BF16).astype(F32), jnp.uint32)
    return (lo >> 16) | hi


def _unpack_rows(u):
    lo = pltpu.bitcast(u << 16, F32)
    hi = pltpu.bitcast(u & jnp.uint32(0xFFFF0000), F32)
    return jnp.concatenate([lo, hi], axis=1)


CAST_STEPS = 64


def _cast_specs(casts, step_of):
    in_specs, out_shape, out_specs, args = [], [], [], []
    for arr, n_rows, first_row in casts:
        c_rows = n_rows // CAST_STEPS
        first = first_row // c_rows
        chunk = lambda *idx: jnp.minimum(step_of(*idx), CAST_STEPS - 1)
        in_specs.append(pl.BlockSpec((c_rows, arr.shape[1]), lambda *idx, first=first: (first + chunk(*idx), 0)))
        out_shape.append(jax.ShapeDtypeStruct((n_rows, arr.shape[1]), BF16))
        out_specs.append(pl.BlockSpec((c_rows, arr.shape[1]), lambda *idx: (chunk(*idx), 0)))
        args.append(arr)
    return in_specs, out_shape, out_specs, args


def _mod_row(tile_rows):
    return lambda i: jnp.minimum((i * tile_rows) // SEQ, B)


def _stream_specs(stream, tile_rows, lag=0):
    n_lat = R_LAT // tile_rows
    base = stream[2] // tile_rows
    return (pl.BlockSpec((tile_rows, D), lambda i, *_: (jnp.clip(i - lag, 0, n_lat - 1), 0)),
            pl.BlockSpec((tile_rows, D), lambda i, *_: (base + jnp.maximum(i - lag - n_lat, 0), 0)))


ADA_TN = 1024


def _ada_kernel(c_ref, w_ref, b_ref, o_ref):
    x = c_ref[...]
    s = x * jax.nn.sigmoid(x)
    o_ref[0] = _dot_hi(s, w_ref[0]) + b_ref[0]


def _ada_table(cc, ada_w, ada_b):
    n = 6 * D
    return pl.pallas_call(
        _ada_kernel,
        out_shape=jax.ShapeDtypeStruct((DEPTH, SUBLANE, n), F32),
        grid=(DEPTH, n // ADA_TN),
        in_specs=[
            pl.BlockSpec((SUBLANE, D), lambda l, j: (0, 0)),
            pl.BlockSpec((1, D, ADA_TN), lambda l, j: (l, 0, j)),
            pl.BlockSpec((1, 1, ADA_TN), lambda l, j: (l, 0, j)),
        ],
        out_specs=pl.BlockSpec((1, SUBLANE, ADA_TN), lambda l, j: (l, 0, j)),
        compiler_params=_cparams(2),
        name="ada_table",
    )(cc, ada_w, ada_b.reshape(DEPTH, 1, n))


PROJ_TM = 1024
PROJ_TN = 1024


def _proj_kernel(h_ref, hc_ref, mod_ref, w_ref, *rest, with_extra, q_tiles, q_scale):
    if with_extra:
        wx_ref, o_ref, ox_ref, a_scr = rest
    else:
        o_ref, a_scr = rest
    j = pl.program_id(1)

    @pl.when(j == 0)
    def _():
        sh = mod_ref[0, 0:1, :]
        sc = mod_ref[0, 1:2, :]
        h = jnp.where(pl.program_id(0) < R_LAT // PROJ_TM, h_ref[...], hc_ref[...])
        a_scr[...] = (h * (1.0 + sc) + sh).astype(BF16)
        if with_extra:
            ox_ref[...] = _dot(a_scr[...], wx_ref[...])

    scale = jnp.where(j < q_tiles, q_scale, 1.0)
    o_ref[...] = (_dot(a_scr[...], w_ref[...]) * scale).astype(o_ref.dtype)


def _proj(stream, mod, w, n, q_cols, q_scale, w_extra=None):
    rows = R_ALL
    with_extra = w_extra is not None
    in_specs = [
        *_stream_specs(stream, PROJ_TM),
        pl.BlockSpec((1, 6, D), lambda i, j: (_mod_row(PROJ_TM)(i), 0, 0)),
        pl.BlockSpec((D, PROJ_TN), lambda i, j: (0, j)),
    ]
    out_shape = [jax.ShapeDtypeStruct((rows, n), BF16)]
    out_specs = [pl.BlockSpec((PROJ_TM, PROJ_TN), lambda i, j: (i, j))]
    args = [stream[0], stream[1], mod, w]
    if with_extra:
        in_specs.append(pl.BlockSpec((D, LANE), lambda i, j: (0, 0)))
        out_shape.append(jax.ShapeDtypeStruct((rows, LANE), F32))
        out_specs.append(pl.BlockSpec((PROJ_TM, LANE), lambda i, j: (i, 0)))
        args.append(w_extra)
    out = pl.pallas_call(
        functools.partial(_proj_kernel, with_extra=with_extra, q_tiles=q_cols // PROJ_TN, q_scale=q_scale),
        out_shape=out_shape,
        grid=(rows // PROJ_TM, n // PROJ_TN),
        in_specs=in_specs,
        out_specs=out_specs,
        scratch_shapes=[pltpu.VMEM((PROJ_TM, D), BF16)],
        compiler_params=_cparams(2),
        name="mod_proj",
    )(*args)
    return out if with_extra else out[0]


GLA_TB = 256
GLA_STEPS = 1 + SEQ // GLA_TB


def _rope_tables():
    half = GLA_DK // 4
    freqs = ROPE_BASE ** (-np.arange(half, dtype=np.float64) / half)
    t = np.arange(SEQ)
    ang_r = (t // GRID_W)[:, None] * freqs
    ang_c = (t % GRID_W)[:, None] * freqs
    cos = np.concatenate([np.cos(ang_r)] * 2 + [np.cos(ang_c)] * 2, axis=1)
    sin = np.concatenate([-np.sin(ang_r), np.sin(ang_r), -np.sin(ang_c), np.sin(ang_c)], axis=1)
    cos = np.concatenate([cos, np.ones((GLA_TB, GLA_DK))], axis=0)
    sin = np.concatenate([sin, np.zeros((GLA_TB, GLA_DK))], axis=0)
    return jnp.asarray(cos, F32), jnp.asarray(sin, F32)


def _rope(x, cos, sin):
    half = LANE // 2
    parts = [pltpu.roll(x[:, g * LANE:(g + 1) * LANE], half, 1) for g in range(x.shape[1] // LANE)]
    return x * cos + jnp.concatenate(parts, axis=1) * sin


def _gla_kernel(q_ref, k_ref, v_ref, glr_ref, wg_ref, bg_ref, cos_ref, sin_ref, *rest, reverse, n_cast):
    cast_in, o_ref, cast_out, s_scr = rest[:n_cast], rest[n_cast], rest[n_cast + 1:-1], rest[-1]
    for src, dst in zip(cast_in, cast_out):
        dst[...] = src[...].astype(BF16)

    @pl.when(pl.program_id(1) == 0)
    def _():
        s_scr[...] = jnp.zeros_like(s_scr)

    row = lax.broadcasted_iota(jnp.int32, (GLA_CHUNK, GLA_CHUNK), 0)
    col = lax.broadcasted_iota(jnp.int32, (GLA_CHUNK, GLA_CHUNK), 1)
    keep = (col >= row) if reverse else (col <= row)
    tri = jnp.where(keep, 1.0, 0.0).astype(BF16)
    ones = jnp.ones((GLA_CHUNK, LANE), BF16)
    n_chunks = GLA_TB // GLA_CHUNK
    order = range(n_chunks - 1, -1, -1) if reverse else range(n_chunks)
    for ci in order:
        sl = slice(ci * GLA_CHUNK, (ci + 1) * GLA_CHUNK)
        cos = cos_ref[sl, :]
        sin = sin_ref[sl, :]
        z = _dot(glr_ref[sl, :].astype(BF16), wg_ref[0]) + bg_ref[0]
        g = (jnp.minimum(z, 0.0) - jnp.log1p(jnp.exp(-jnp.abs(z)))) * (1.0 / GLA_GATE_NORM)
        g_hi, g_lo = _split_bf16(g)
        cum_all = _dot(tri, g_hi) + _dot(tri, g_lo)
        decay_all = jnp.exp(_dot_tn(g_hi, ones) + _dot_tn(g_lo, ones))
        for hd in range(GLA_H):
            ks = slice(hd * GLA_DK, (hd + 1) * GLA_DK)
            vs = slice(hd * GLA_DV, (hd + 1) * GLA_DV)
            q = _rope(q_ref[sl, ks].astype(F32), cos, sin)
            k = _rope(k_ref[sl, ks].astype(F32), cos, sin)
            v = v_ref[sl, vs]
            cum = cum_all[:, ks]
            tot = cum[0:1, :] if reverse else cum[GLA_CHUNK - 1:GLA_CHUNK, :]
            q_dec = (q * jnp.exp(cum)).astype(BF16)
            k_inv = (k * jnp.exp(-cum)).astype(BF16)
            k_end = (k * jnp.exp(tot - cum)).astype(BF16)
            att = jnp.where(keep, _dot_nt(q_dec, k_inv), 0.0).astype(BF16)
            state = s_scr[hd]
            o_ref[sl, vs] = (_dot(att, v) + _dot(q_dec, state.astype(BF16))).astype(o_ref.dtype)
            decay = jnp.concatenate([decay_all[ks, :]] * (GLA_DV // LANE), axis=1)
            s_scr[hd] = state * decay + _dot_tn(k_end, v)


def _gla_scan(p, glr, wg, bg, cos, sin, reverse, casts):
    lat_blocks = SEQ // GLA_TB

    def tblock(s):
        if reverse:
            return jnp.where(s == 0, 0, GLA_STEPS - s)
        return s

    def rows(b, s):
        tb = tblock(s)
        return jnp.where(tb == 0, R_LAT // GLA_TB + b, b * lat_blocks + tb - 1)

    def rope_rows(s):
        tb = tblock(s)
        return jnp.where(tb == 0, lat_blocks, tb - 1)

    d = 1 if reverse else 0
    c_in, c_shape, c_out, c_args = _cast_specs(casts, lambda b, s: b * GLA_STEPS + s)
    return pl.pallas_call(
        functools.partial(_gla_kernel, reverse=reverse, n_cast=len(casts)),
        out_shape=[jax.ShapeDtypeStruct((R_ALL, GLA_VD), BF16), *c_shape],
        grid=(B, GLA_STEPS),
        in_specs=[
            pl.BlockSpec((GLA_TB, GLA_QK), lambda b, s: (rows(b, s), 0)),
            pl.BlockSpec((GLA_TB, GLA_QK), lambda b, s: (rows(b, s), 1)),
            pl.BlockSpec((GLA_TB, GLA_VD), lambda b, s: (rows(b, s), 2 * GLA_QK // GLA_VD)),
            pl.BlockSpec((GLA_TB, LANE), lambda b, s: (rows(b, s), 0)),
            pl.BlockSpec((1, LANE, GLA_QK), lambda b, s: (d, 0, 0)),
            pl.BlockSpec((1, 1, GLA_QK), lambda b, s: (d, 0, 0)),
            pl.BlockSpec((GLA_TB, GLA_DK), lambda b, s: (rope_rows(s), 0)),
            pl.BlockSpec((GLA_TB, GLA_DK), lambda b, s: (rope_rows(s), 0)),
            *c_in,
        ],
        out_specs=[pl.BlockSpec((GLA_TB, GLA_VD), lambda b, s: (rows(b, s), 0)), *c_out],
        scratch_shapes=[pltpu.VMEM((GLA_H, GLA_DK, GLA_DV), F32)],
        compiler_params=_cparams(2),
        name="gla_scan_bwd" if reverse else "gla_scan_fwd",
    )(p, p, p, glr, wg, bg, cos, sin, *c_args)


def _nat_bias_table(rpb):
    qc = np.arange(GRID_W)[:, None]
    kc = np.arange(GRID_W)[None, :]
    cs = np.clip(qc - WIN_C // 2, 0, GRID_W - WIN_C)
    col_ok = (kc >= cs) & (kc < cs + WIN_C)
    dc = np.clip(kc - qc + WIN_C - 1, 0, 2 * WIN_C - 2)
    pick = (dc[:, :, None] == np.arange(2 * WIN_C - 1)).astype(np.float32)
    tc = jnp.einsum('had,qkd->hqak', rpb.astype(F32), jnp.asarray(pick), precision=lax.Precision.HIGHEST)
    tc = jnp.where(col_ok[None, :, None, :], tc, NEG_INF)
    n_dr = 2 * WIN_R - 1
    pad = NAT_BAND - 1
    strip = jnp.pad(tc.reshape(NAT_H, GRID_W, n_dr * GRID_W), ((0, 0), (0, 0), (pad * GRID_W, pad * GRID_W)),
                    constant_values=NEG_INF)
    rows_n = SEQ // GRID_W
    half = WIN_R // 2
    n_keys = NAT_BAND * GRID_W
    bands, valid = [], np.zeros((3, NAT_QR, 1, NAT_BAND, 1), bool)
    for p_i, (r0, bs) in enumerate(((0, 0), (NAT_QR, NAT_QR - half), (rows_n - NAT_QR, rows_n - NAT_BAND))):
        for i in range(NAT_QR):
            qr = r0 + i
            rs = min(max(qr - half, 0), rows_n - WIN_R)
            first = (pad + bs - qr + WIN_R - 1) * GRID_W
            bands.append(strip[:, :, first:first + n_keys])
            valid[p_i, i, 0, :, 0] = [rs <= kr < rs + WIN_R for kr in range(bs, bs + NAT_BAND)]
    tab = jnp.stack(bands).reshape(3, NAT_QR, NAT_H, GRID_W, n_keys)
    valid = np.broadcast_to(valid, (3, NAT_QR, GRID_W, NAT_BAND, GRID_W)).reshape(3, NAT_QR, 1, GRID_W, n_keys)
    return jnp.where(valid, tab, NEG_INF)


def _nat_kernel(q_ref, k_ref, v_ref, kc_ref, vc_ref, bias_ref, o_ref):
    rb = pl.program_id(2)
    rows_n = SEQ // GRID_W
    band_start = jnp.clip(rb * NAT_QR - WIN_R // 2, 0, rows_n - NAT_BAND)
    start = pl.multiple_of(band_start * GRID_W, GRID_W)
    n_keys = NAT_BAND * GRID_W
    for g in range(NAT_G):
        hs = slice(g * NAT_DH, (g + 1) * NAT_DH)
        q = q_ref[:, hs]
        bias = bias_ref[0, :, g].reshape(NAT_QR * GRID_W, n_keys)
        s_loc = _dot_nt(q, k_ref[pl.ds(start, n_keys), hs]) + bias
        s_ctx = _dot_nt(q, kc_ref[:, hs])
        m = jnp.maximum(jnp.max(s_loc, axis=1, keepdims=True), jnp.max(s_ctx, axis=1, keepdims=True))
        p_loc = jnp.exp(s_loc - m)
        p_ctx = jnp.exp(s_ctx - m)
        denom = jnp.sum(p_loc, axis=1, keepdims=True) + jnp.sum(p_ctx, axis=1, keepdims=True)
        o = _dot(p_loc.astype(BF16), v_ref[pl.ds(start, n_keys), hs]) + _dot(p_ctx.astype(BF16), vc_ref[:, hs])
        o_ref[:, hs] = (o / denom).astype(o_ref.dtype)


def _nat_attention(p, bias_tab):
    n_rb = SEQ // (NAT_QR * GRID_W)
    tq = NAT_QR * GRID_W
    width = NAT_G * NAT_DH
    hq = D // width

    def pattern(rb):
        return jnp.where(rb == 0, 0, jnp.where(rb == n_rb - 1, 2, 1))

    return pl.pallas_call(
        _nat_kernel,
        out_shape=jax.ShapeDtypeStruct((R_LAT, D), BF16),
        grid=(B, NAT_H // NAT_G, n_rb),
        in_specs=[
            pl.BlockSpec((tq, width), lambda b, h, rb: (b * n_rb + rb, h)),
            pl.BlockSpec((SEQ, width), lambda b, h, rb: (b, hq + h)),
            pl.BlockSpec((SEQ, width), lambda b, h, rb: (b, 2 * hq + h)),
            pl.BlockSpec((CTX, width), lambda b, h, rb: (R_LAT // CTX + b, hq + h)),
            pl.BlockSpec((CTX, width), lambda b, h, rb: (R_LAT // CTX + b, 2 * hq + h)),
            pl.BlockSpec((1, NAT_QR, NAT_G, GRID_W, NAT_BAND * GRID_W), lambda b, h, rb: (pattern(rb), 0, h, 0, 0)),
        ],
        out_specs=pl.BlockSpec((tq, width), lambda b, h, rb: (b * n_rb + rb, h)),
        compiler_params=_cparams(3),
        name="nat_attention",
    )(p, p, p, p, p, bias_tab)


MIX_TM = 256


def _layer_norm(u, g, b):
    mu = jnp.mean(u, axis=-1, keepdims=True)
    var = jnp.mean(jnp.square(u - mu), axis=-1, keepdims=True)
    return (u - mu) * lax.rsqrt(var + LN_EPS) * g + b


def _mix_out_kernel(*refs, gla, lat_tiles):
    if gla:
        of_ref, ob_ref, r_ref, ng_ref = refs[:4]
        refs = refs[4:]
    else:
        x_ref = refs[0]
        refs = refs[1:]
    (h_ref, hc_ref, mod_ref, w_ref, lng_ref, lnb_ref, wr_ref, br_ref,
     h1_ref, f_ref, rt_ref, cnt_ref, y_even, y_odd, cnt_scr) = refs
    s = pl.program_id(0)

    @pl.when(s == 0)
    def _():
        cnt_scr[...] = jnp.zeros_like(cnt_scr)
        y_odd[...] = jnp.zeros_like(y_odd)

    def step(y_new, y_prev):
        if gla:
            parts = []
            for hd in range(GLA_H):
                cs = slice(hd * GLA_DV, (hd + 1) * GLA_DV)
                o = of_ref[:, cs].astype(F32) + ob_ref[:, cs].astype(F32)
                o = o * lax.rsqrt(jnp.mean(jnp.square(o), axis=-1, keepdims=True) + LN_EPS) * ng_ref[...]
                r = r_ref[:, cs].astype(F32)
                parts.append((o * (r * jax.nn.sigmoid(r))).astype(BF16))
            x = jnp.concatenate(parts, axis=1)
        else:
            x = x_ref[...]
        y_new[...] = _dot(x, w_ref[...])

        gate1 = mod_ref[0, 2:3, :]
        h_in = jnp.where(s - 1 < lat_tiles, h_ref[...], hc_ref[...])
        h1 = _layer_norm(ALPHA * h_in + gate1 * y_prev[...], lng_ref[...], lnb_ref[...])
        h1_ref[...] = h1
        f = h1 * (1.0 + mod_ref[0, 4:5, :]) + mod_ref[0, 3:4, :]
        f_ref[...] = _pack_rows(f)
        live = jnp.where(s > 0, 1.0, 0.0)
        logits = _dot(f.astype(BF16), wr_ref[...]) + br_ref[...]
        route, counts = _route_tile(logits, cnt_scr[...], live)
        rt_ref[...] = route
        cnt_scr[...] = counts
        cnt_ref[...] = jnp.broadcast_to(counts, cnt_ref.shape)

    @pl.when(s % 2 == 0)
    def _():
        step(y_even, y_odd)

    @pl.when(s % 2 == 1)
    def _():
        step(y_odd, y_even)


RT_E, RT_W, RT_RANK = 0, 2, 4


def _route_tile(logits, run, live):
    tm = logits.shape[0]
    lane = lax.broadcasted_iota(jnp.int32, (tm, LANE), 1).astype(F32)

    def first_max(vals):
        top = jnp.max(vals, axis=1, keepdims=True)
        return top, jnp.min(jnp.where(vals == top, lane, float(LANE)), axis=1, keepdims=True)

    gl = jnp.where(lane < N_GROUPS, logits, NEG_INF)
    g_max, g_idx = first_max(gl)
    g_p = 1.0 / jnp.sum(jnp.exp(gl - g_max), axis=1, keepdims=True)
    lo = N_GROUPS + EPG * g_idx
    el = jnp.where((lane >= lo) & (lane < lo + EPG), logits, NEG_INF)
    e1, i1 = first_max(el)
    e2, i2 = first_max(jnp.where(lane == i1, NEG_INF, el))
    t = jnp.exp(e2 - e1)
    p1 = 1.0 / (1.0 + t)
    x1 = i1 - N_GROUPS
    x2 = i2 - N_GROUPS
    onehot = jnp.where((lane == x1) | (lane == x2), 1.0, 0.0)
    row = lax.broadcasted_iota(jnp.int32, (tm, tm), 0)
    col = lax.broadcasted_iota(jnp.int32, (tm, tm), 1)
    before = jnp.where(col < row, 1.0, 0.0).astype(BF16)
    seen = _dot(before, onehot.astype(BF16)) + run
    rank1 = jnp.sum(jnp.where(lane == x1, seen, 0.0), axis=1, keepdims=True)
    rank2 = jnp.sum(jnp.where(lane == x2, seen, 0.0), axis=1, keepdims=True)
    route = jnp.zeros((tm, LANE), F32)
    for k, val in ((RT_E, x1), (RT_E + 1, x2), (RT_W, g_p * p1), (RT_W + 1, g_p * p1 * t),
                   (RT_RANK, rank1), (RT_RANK + 1, rank2)):
        route = jnp.where(lane == k, val, route)
    return route, run + live * jnp.sum(onehot, axis=0, keepdims=True)


def _mix_out(mix_inputs, stream, mod, w_out, ln_g, ln_b, w_router, b_router, rows, gla):
    n_tiles = rows // MIX_TM
    row_spec = lambda width: pl.BlockSpec((MIX_TM, width), lambda i: (jnp.minimum(i, n_tiles - 1), 0))
    done_spec = lambda width: pl.BlockSpec((MIX_TM, width), lambda i: (jnp.maximum(i - 1, 0), 0))
    full = lambda a: pl.BlockSpec(a.shape, lambda i: (0,) * a.ndim)
    if gla:
        o_f, o_b, p, norm_g = mix_inputs
        in_specs = [row_spec(GLA_VD), row_spec(GLA_VD),
                    pl.BlockSpec((MIX_TM, GLA_VD),
                                 lambda i: (jnp.minimum(i, n_tiles - 1), (2 * GLA_QK + GLA_VD) // GLA_VD)),
                    full(norm_g)]
        args = [o_f, o_b, p, norm_g]
    else:
        (x,) = mix_inputs
        in_specs = [row_spec(D)]
        args = [x]
    in_specs += [*_stream_specs(stream, MIX_TM, lag=1),
                 pl.BlockSpec((1, 6, D), lambda i: (_mod_row(MIX_TM)(jnp.maximum(i - 1, 0)), 0, 0)),
                 full(w_out), full(ln_g), full(ln_b), full(w_router), full(b_router)]
    args += [stream[0], stream[1], mod, w_out, ln_g, ln_b, w_router, b_router]
    return pl.pallas_call(
        functools.partial(_mix_out_kernel, gla=gla, lat_tiles=R_LAT // MIX_TM),
        out_shape=[jax.ShapeDtypeStruct((rows, D), F32), jax.ShapeDtypeStruct((rows, D_PACK), jnp.uint32),
                   jax.ShapeDtypeStruct((rows, LANE), F32), jax.ShapeDtypeStruct((SUBLANE, LANE), F32)],
        grid=(n_tiles + 1,),
        in_specs=in_specs,
        out_specs=[done_spec(D), done_spec(D_PACK), done_spec(LANE), pl.BlockSpec((SUBLANE, LANE), lambda i: (0, 0))],
        scratch_shapes=[pltpu.VMEM((MIX_TM, D), F32), pltpu.VMEM((MIX_TM, D), F32), pltpu.VMEM((1, LANE), F32)],
        compiler_params=_cparams(1),
        name="gla_out_ln" if gla else "nat_out_ln",
    )(*args)


def _plan(route, counts_row):
    t = route.shape[0]
    e = route[:, RT_E:RT_E + 2].astype(jnp.int32)
    rank = route[:, RT_RANK:RT_RANK + 2].astype(jnp.int32)
    counts = counts_row[0, :N_EXPERTS].astype(jnp.int32)
    tiles_per = (counts + MOE_TM - 1) // MOE_TM
    tile_end = jnp.cumsum(tiles_per)
    tile_start = tile_end - tiles_per
    n_tiles = 2 * t // MOE_TM + N_EXPERTS
    ids = jnp.arange(n_tiles, dtype=jnp.int32)
    used = tile_end[-1]
    tile_valid = (ids < used).astype(jnp.int32)
    lookup = jnp.minimum(ids, used - 1)
    tile_expert = jnp.sum((tile_end[None, :] <= lookup[:, None]).astype(jnp.int32), axis=1)
    tile_expert = jnp.minimum(tile_expert, N_EXPERTS - 1).astype(jnp.int32)
    base = tile_start * MOE_TM
    experts = jnp.arange(N_EXPERTS, dtype=jnp.int32)
    pos = jnp.sum(jnp.where(e[..., None] == experts, base, 0), axis=-1) + rank
    trailing = n_tiles - 1 - experts
    fill_tile = jnp.concatenate([jnp.maximum(tile_end - 1, 0), trailing]).astype(jnp.int32)
    fill_on = jnp.concatenate([tiles_per > 0, trailing >= used]).astype(jnp.int32)
    return tile_expert, tile_valid, pos.reshape(-1).astype(jnp.int32), fill_tile, fill_on, n_tiles


def _start_row_copies(pos_ref, tile, rows, make):
    base = 2 * tile * rows
    for r in range(rows):
        make(r, pos_ref[base + 2 * r], False).start(priority=0)
        make(r, pos_ref[base + 2 * r + 1], True).start(priority=1)


DISP_TM = 256


def _dispatch_kernel(pos_ref, lt_ref, ht_ref, f_ref, xs_hbm, sbuf, zbuf, sem, zsem):
    i = pl.program_id(0)
    n = pl.num_programs(0)
    slot = i % 2

    def fill(e):
        return pltpu.make_async_copy(zbuf, xs_hbm.at[pl.ds(lt_ref[e] * MOE_TM, MOE_TM), :], zsem)

    @pl.when(i == 0)
    def _():
        zbuf[...] = jnp.zeros_like(zbuf)
        for e in range(2 * N_EXPERTS):
            @pl.when(ht_ref[e] != 0)
            def _():
                fill(e).start()
        for e in range(2 * N_EXPERTS):
            @pl.when(ht_ref[e] != 0)
            def _():
                fill(e).wait()

    def wait_scatter(sl):
        for _ in range(2):
            pltpu.make_async_copy(sbuf.at[sl], xs_hbm.at[pl.ds(0, DISP_TM), :], sem.at[sl]).wait()

    @pl.when(i >= 2)
    def _():
        wait_scatter(slot)

    sbuf[slot] = f_ref[...]

    def make(r, dst, second):
        return pltpu.make_async_copy(sbuf.at[slot, pl.ds(r, 1), :], xs_hbm.at[pl.ds(dst, 1), :], sem.at[slot])

    _start_row_copies(pos_ref, i, DISP_TM, make)

    @pl.when(i == n - 1)
    def _():
        wait_scatter(slot)

        @pl.when(n >= 2)
        def _():
            wait_scatter(1 - slot)


def _dispatch(f, pos, last_tile, has_tile, cap):
    t = f.shape[0]
    grid_spec = pltpu.PrefetchScalarGridSpec(
        num_scalar_prefetch=3,
        grid=(t // DISP_TM,),
        in_specs=[pl.BlockSpec((DISP_TM, D_PACK), lambda i, *_: (i, 0))],
        out_specs=pl.BlockSpec(memory_space=pl.ANY),
        scratch_shapes=[
            pltpu.VMEM((2, DISP_TM, D_PACK), jnp.uint32),
            pltpu.VMEM((MOE_TM, D_PACK), jnp.uint32),
            pltpu.SemaphoreType.DMA((2,)),
            pltpu.SemaphoreType.DMA,
        ],
    )
    return pl.pallas_call(
        _dispatch_kernel,
        out_shape=jax.ShapeDtypeStruct((cap, D_PACK), jnp.uint32),
        grid_spec=grid_spec,
        compiler_params=_cparams(1),
        name="moe_dispatch",
    )(pos, last_tile, has_tile, f)


def _moe_kernel(te_ref, tv_ref, x_ref, w1_ref, w3_ref, w2_ref, y_ref):
    i = pl.program_id(0)

    @pl.when(tv_ref[i] != 0)
    def _():
        x = _unpack_rows(x_ref[...]).astype(BF16)
        h1 = _dot(x, w1_ref[...])
        h3 = _dot(x, w3_ref[...])
        a = (h1 * jax.nn.sigmoid(h1) * h3).astype(BF16)
        y_ref[...] = _pack_rows(_dot(a, w2_ref[...]))

    @pl.when(tv_ref[i] == 0)
    def _():
        y_ref[...] = jnp.zeros_like(y_ref)


def _moe_experts(xs, tile_expert, tile_valid, w1, w3, w2):
    cap = xs.shape[0]
    grid_spec = pltpu.PrefetchScalarGridSpec(
        num_scalar_prefetch=2,
        grid=(cap // MOE_TM,),
        in_specs=[
            pl.BlockSpec((MOE_TM, D_PACK), lambda i, te, tv: (i * tv[i], 0)),
            pl.BlockSpec((D, D_EXPERT), lambda i, te, tv: (te[i], 0)),
            pl.BlockSpec((D, D_EXPERT), lambda i, te, tv: (te[i], 0)),
            pl.BlockSpec((D_EXPERT, D), lambda i, te, tv: (te[i], 0)),
        ],
        out_specs=pl.BlockSpec((MOE_TM, D_PACK), lambda i, te, tv: (i, 0)),
    )
    return pl.pallas_call(
        _moe_kernel,
        out_shape=jax.ShapeDtypeStruct((cap, D_PACK), jnp.uint32),
        grid_spec=grid_spec,
        compiler_params=_cparams(1),
        name="moe_experts",
    )(tile_expert, tile_valid, xs, w1, w3, w2)


FIN_TM = 256


def _final_kernel(pos_ref, h_ref, rt_ref, mod_ref, lng_ref, lnb_ref, ys_hbm, o_ref, abuf, bbuf, sem):
    i = pl.program_id(0)
    n = pl.num_programs(0)
    slot = i % 2

    def start_gather(tile, sl):
        def make(r, src, second):
            buf = bbuf if second else abuf
            return pltpu.make_async_copy(ys_hbm.at[pl.ds(src, 1), :], buf.at[sl, pl.ds(r, 1), :], sem.at[sl])

        _start_row_copies(pos_ref, tile, FIN_TM, make)

    @pl.when(i == 0)
    def _():
        start_gather(0, 0)

    @pl.when(i + 1 < n)
    def _():
        start_gather(i + 1, 1 - slot)

    for buf in (abuf, bbuf):
        pltpu.make_async_copy(ys_hbm.at[pl.ds(0, FIN_TM), :], buf.at[slot], sem.at[slot]).wait()

    y = (_unpack_rows(abuf[slot]) * rt_ref[:, RT_W:RT_W + 1]
         + _unpack_rows(bbuf[slot]) * rt_ref[:, RT_W + 1:RT_W + 2])
    gate2 = mod_ref[0, 5:6, :]
    o_ref[...] = _layer_norm(ALPHA * h_ref[...] + gate2 * y, lng_ref[...], lnb_ref[...])


def _ffn_residual(h1, route, pos, ys, mod, ln_g, ln_b):
    rows = h1.shape[0]
    grid_spec = pltpu.PrefetchScalarGridSpec(
        num_scalar_prefetch=1,
        grid=(rows // FIN_TM,),
        in_specs=[
            pl.BlockSpec((FIN_TM, D), lambda i, *_: (i, 0)),
            pl.BlockSpec((FIN_TM, LANE), lambda i, *_: (i, 0)),
            pl.BlockSpec((1, 6, D), lambda i, *_: (_mod_row(FIN_TM)(i), 0, 0)),
            pl.BlockSpec((1, D), lambda i, *_: (0, 0)),
            pl.BlockSpec((1, D), lambda i, *_: (0, 0)),
            pl.BlockSpec(memory_space=pl.ANY),
        ],
        out_specs=pl.BlockSpec((FIN_TM, D), lambda i, *_: (i, 0)),
        scratch_shapes=[
            pltpu.VMEM((2, FIN_TM, D_PACK), jnp.uint32),
            pltpu.VMEM((2, FIN_TM, D_PACK), jnp.uint32),
            pltpu.SemaphoreType.DMA((2,)),
        ],
    )
    return pl.pallas_call(
        _final_kernel,
        out_shape=jax.ShapeDtypeStruct((rows, D), F32),
        grid_spec=grid_spec,
        compiler_params=_cparams(1),
        name="ffn_residual_ln",
    )(pos, h1, route, mod, ln_g, ln_b, ys)


def _moe_ffn(h1, f, route, counts, mod, w1, w3, w2, ln_g, ln_b):
    tile_expert, tile_valid, pos, fill_tile, fill_on, n_tiles = _plan(route, counts)
    xs = _dispatch(f, pos, fill_tile, fill_on, n_tiles * MOE_TM)
    ys = _moe_experts(xs, tile_expert, tile_valid, w1, w3, w2)
    return _ffn_residual(h1, route, pos, ys, mod, ln_g, ln_b)


def _router_weights(w_group, b_group, w_expert, b_expert):
    pad = LANE - N_GROUPS - N_EXPERTS
    w = jnp.concatenate([w_group, w_expert, jnp.zeros((D, pad), F32)], axis=1).astype(BF16)
    b = jnp.concatenate([b_group, b_expert, jnp.zeros((pad,), F32)])[None, :]
    return w, b


def kernel(x, c, ctx, c_ctx, ada_w, ada_b, ln_g, ln_b, gla_w_in, gla_w_gate, gla_b_gate, gla_norm_g, gla_w_out,
           nat_w_in, nat_rpb, nat_w_out, moe_w_group, moe_b_group, moe_w_expert, moe_b_expert,
           moe_w1, moe_w3, moe_w2):
    stream = (x.reshape(R_LAT, D), ctx.reshape(R_CTX, D), 0)
    cc = jnp.concatenate([c, c_ctx[None, :], jnp.zeros((SUBLANE - B - 1, D), F32)], axis=0)
    mod = _ada_table(cc, ada_w, ada_b).reshape(DEPTH, SUBLANE, 6, D)
    w1_all = moe_w1.reshape(DEPTH * N_EXPERTS * D, D_EXPERT)
    w3_all = moe_w3.reshape(DEPTH * N_EXPERTS * D, D_EXPERT)
    w2_all = moe_w2.reshape(DEPTH * N_EXPERTS * D_EXPERT, D)

    def expert_casts(layer):
        up, down = N_EXPERTS * D, N_EXPERTS * D_EXPERT
        return ((w1_all, up, layer * up), (w3_all, up, layer * up), (w2_all, down, layer * down))

    w_in = gla_w_in[0]
    w_glr = jnp.concatenate([w_in[:, GLA_MAIN:], jnp.zeros((D, LANE - 2 * GLA_RANK), F32)], axis=1).astype(BF16)
    p, glr = _proj(stream, mod[0], w_in.astype(BF16), GLA_MAIN, GLA_QK, GLA_DK ** -0.5, w_glr)
    wg = jnp.zeros((2, LANE, GLA_QK), F32)
    wg = wg.at[0, :GLA_RANK].set(gla_w_gate[0, 0]).at[1, GLA_RANK:2 * GLA_RANK].set(gla_w_gate[0, 1]).astype(BF16)
    bg = gla_b_gate[0][:, None, :]
    cos, sin = _rope_tables()
    o_f, *experts0 = _gla_scan(p, glr, wg, bg, cos, sin, False, expert_casts(0))
    o_b, *experts1 = _gla_scan(p, glr, wg, bg, cos, sin, True, expert_casts(1))
    w_r, b_r = _router_weights(moe_w_group[0], moe_b_group[0], moe_w_expert[0], moe_b_expert[0])
    h1, f, route, counts = _mix_out((o_f, o_b, p, gla_norm_g[0][None, :]), stream, mod[0], gla_w_out[0].astype(BF16),
                                    ln_g[0, 0][None, :], ln_b[0, 0][None, :], w_r, b_r, R_ALL, gla=True)
    h = _moe_ffn(h1, f, route, counts, mod[0], *experts0, ln_g[0, 1][None, :], ln_b[0, 1][None, :])

    stream = (h, h, R_LAT)
    p = _proj(stream, mod[1], nat_w_in[0].astype(BF16), 3 * D, D, NAT_DH ** -0.5)
    o = _nat_attention(p, _nat_bias_table(nat_rpb[0]))
    w_r, b_r = _router_weights(moe_w_group[1], moe_b_group[1], moe_w_expert[1], moe_b_expert[1])
    h1, f, route, counts = _mix_out((o,), stream, mod[1], nat_w_out[0].astype(BF16),
                                    ln_g[1, 0][None, :], ln_b[1, 0][None, :], w_r, b_r, R_LAT, gla=False)
    out = _moe_ffn(h1, f, route, counts, mod[1], *experts1, ln_g[1, 1][None, :], ln_b[1, 1][None, :])
    return out.reshape(B, SEQ, D)
```

```python
import functools

import jax
import jax.numpy as jnp
import numpy as np
from jax import lax
from jax.experimental import pallas as pl
from jax.experimental.pallas import tpu as pltpu

F32 = jnp.float32
BF16 = jnp.bfloat16

D = 2048
B = 4
SEQ = 4096
DEPTH = 2
GRID_W = 64
CTX = 256
R_LAT = B * SEQ
R_CTX = B * CTX
R_ALL = R_LAT + R_CTX

GLA_H = 4
GLA_DK = 256
GLA_DV = 512
GLA_CHUNK = 64
GLA_RANK = 16
GLA_QK = GLA_H * GLA_DK
GLA_VD = GLA_H * GLA_DV
GLA_MAIN = 2 * GLA_QK + 2 * GLA_VD
GLA_GATE_NORM = 16.0
ROPE_BASE = 10000.0

NAT_H = 16
NAT_DH = 128
WIN_R = 8
WIN_C = 16
NAT_QR = 4
NAT_BAND = 12
NAT_G = 8

N_GROUPS = 4
EPG = 4
N_EXPERTS = 16
D_EXPERT = 1024
MOE_TM = 256

ALPHA = (2 * DEPTH) ** 0.25
LN_EPS = 1e-5
NEG_INF = -1e30

LANE = 128
SUBLANE = 8
VMEM_LIMIT = 56 * 1024 * 1024


def _cparams(n_axes):
    return pltpu.CompilerParams(dimension_semantics=("arbitrary",) * n_axes,
                                vmem_limit_bytes=VMEM_LIMIT)


def _split_bf16(x):
    hi = x.astype(BF16)
    lo = (x - hi.astype(F32)).astype(BF16)
    return hi, lo


def _dot(a, b):
    return jnp.dot(a, b, preferred_element_type=F32)


def _dot_hi(a, b):
    ah, al = _split_bf16(a)
    bh, bl = _split_bf16(b)
    return _dot(ah, bh) + _dot(al, bh) + _dot(ah, bl)


def _dot_nt(a, b):
    return lax.dot_general(a, b, (((1,), (1,)), ((), ())), preferred_element_type=F32)


def _dot_tn(a, b):
    return lax.dot_general(a, b, (((0,), (0,)), ((), ())), preferred_element_type=F32)


D_PACK = D // 2


def _pack_rows(x):
    lo = pltpu.bitcast(x[:, :D_PACK].astype(BF16).astype(F32), jnp.uint32)
    hi = pltpu.bitcast(x[:, D_PACK:].astype(BF16).astype(F32), jnp.uint32)
    return (lo >> 16) | hi


def _unpack_rows(u):
    lo = pltpu.bitcast(u << 16, F32)
    hi = pltpu.bitcast(u & jnp.uint32(0xFFFF0000), F32)
    return jnp.concatenate([lo, hi], axis=1)


CAST_STEPS = 64


def _cast_specs(casts, step_of):
    in_specs, out_shape, out_specs, args = [], [], [], []
    for arr, n_rows, first_row in casts:
        c_rows = n_rows // CAST_STEPS
        first = first_row // c_rows
        chunk = lambda *idx: jnp.minimum(step_of(*idx), CAST_STEPS - 1)
        in_specs.append(pl.BlockSpec((c_rows, arr.shape[1]), lambda *idx, first=first: (first + chunk(*idx), 0)))
        out_shape.append(jax.ShapeDtypeStruct((n_rows, arr.shape[1]), BF16))
        out_specs.append(pl.BlockSpec((c_rows, arr.shape[1]), lambda *idx: (chunk(*idx), 0)))
        args.append(arr)
    return in_specs, out_shape, out_specs, args


def _mod_row(tile_rows):
    return lambda i: jnp.minimum((i * tile_rows) // SEQ, B)


def _stream_specs(stream, tile_rows, lag=0):
    n_lat = R_LAT // tile_rows
    base = stream[2] // tile_rows
    return (pl.BlockSpec((tile_rows, D), lambda i, *_: (jnp.clip(i - lag, 0, n_lat - 1), 0)),
            pl.BlockSpec((tile_rows, D), lambda i, *_: (base + jnp.maximum(i - lag - n_lat, 0), 0)))


ADA_TN = 1024


def _ada_kernel(c_ref, w_ref, b_ref, o_ref):
    x = c_ref[...]
    s = x * jax.nn.sigmoid(x)
    o_ref[0] = _dot_hi(s, w_ref[0]) + b_ref[0]


def _ada_table(cc, ada_w, ada_b):
    n = 6 * D
    return pl.pallas_call(
        _ada_kernel,
        out_shape=jax.ShapeDtypeStruct((DEPTH, SUBLANE, n), F32),
        grid=(DEPTH, n // ADA_TN),
        in_specs=[
            pl.BlockSpec((SUBLANE, D), lambda l, j: (0, 0)),
            pl.BlockSpec((1, D, ADA_TN), lambda l, j: (l, 0, j)),
            pl.BlockSpec((1, 1, ADA_TN), lambda l, j: (l, 0, j)),
        ],
        out_specs=pl.BlockSpec((1, SUBLANE, ADA_TN), lambda l, j: (l, 0, j)),
        compiler_params=_cparams(2),
        name="ada_table",
    )(cc, ada_w, ada_b.reshape(DEPTH, 1, n))


PROJ_TM = 1024
PROJ_TN = 1024


def _proj_kernel(h_ref, hc_ref, mod_ref, w_ref, *rest, with_extra, q_tiles, q_scale):
    if with_extra:
        wx_ref, o_ref, ox_ref, a_scr = rest
    else:
        o_ref, a_scr = rest
    j = pl.program_id(1)

    @pl.when(j == 0)
    def _():
        sh = mod_ref[0, 0:1, :]
        sc = mod_ref[0, 1:2, :]
        h = jnp.where(pl.program_id(0) < R_LAT // PROJ_TM, h_ref[...], hc_ref[...])
        a_scr[...] = (h * (1.0 + sc) + sh).astype(BF16)
        if with_extra:
            ox_ref[...] = _dot(a_scr[...], wx_ref[...])

    scale = jnp.where(j < q_tiles, q_scale, 1.0)
    o_ref[...] = (_dot(a_scr[...], w_ref[...]) * scale).astype(o_ref.dtype)


def _proj(stream, mod, w, n, q_cols, q_scale, w_extra=None):
    rows = R_ALL
    with_extra = w_extra is not None
    in_specs = [
        *_stream_specs(stream, PROJ_TM),
        pl.BlockSpec((1, 6, D), lambda i, j: (_mod_row(PROJ_TM)(i), 0, 0)),
        pl.BlockSpec((D, PROJ_TN), lambda i, j: (0, j)),
    ]
    out_shape = [jax.ShapeDtypeStruct((rows, n), BF16)]
    out_specs = [pl.BlockSpec((PROJ_TM, PROJ_TN), lambda i, j: (i, j))]
    args = [stream[0], stream[1], mod, w]
    if with_extra:
        in_specs.append(pl.BlockSpec((D, LANE), lambda i, j: (0, 0)))
        out_shape.append(jax.ShapeDtypeStruct((rows, LANE), F32))
        out_specs.append(pl.BlockSpec((PROJ_TM, LANE), lambda i, j: (i, 0)))
        args.append(w_extra)
    out = pl.pallas_call(
        functools.partial(_proj_kernel, with_extra=with_extra, q_tiles=q_cols // PROJ_TN, q_scale=q_scale),
        out_shape=out_shape,
        grid=(rows // PROJ_TM, n // PROJ_TN),
        in_specs=in_specs,
        out_specs=out_specs,
        scratch_shapes=[pltpu.VMEM((PROJ_TM, D), BF16)],
        compiler_params=_cparams(2),
        name="mod_proj",
    )(*args)
    return out if with_extra else out[0]


GLA_TB = 256
GLA_STEPS = 1 + SEQ // GLA_TB


def _rope_tables():
    half = GLA_DK // 4
    freqs = ROPE_BASE ** (-np.arange(half, dtype=np.float64) / half)
    t = np.arange(SEQ)
    ang_r = (t // GRID_W)[:, None] * freqs
    ang_c = (t % GRID_W)[:, None] * freqs
    cos = np.concatenate([np.cos(ang_r)] * 2 + [np.cos(ang_c)] * 2, axis=1)
    sin = np.concatenate([-np.sin(ang_r), np.sin(ang_r), -np.sin(ang_c), np.sin(ang_c)], axis=1)
    cos = np.concatenate([cos, np.ones((GLA_TB, GLA_DK))], axis=0)
    sin = np.concatenate([sin, np.zeros((GLA_TB, GLA_DK))], axis=0)
    return jnp.asarray(cos, F32), jnp.asarray(sin, F32)


def _rope(x, cos, sin):
    half = LANE // 2
    parts = [pltpu.roll(x[:, g * LANE:(g + 1) * LANE], half, 1) for g in range(x.shape[1] // LANE)]
    return x * cos + jnp.concatenate(parts, axis=1) * sin


def _gla_kernel(q_ref, k_ref, v_ref, glr_ref, wg_ref, bg_ref, cos_ref, sin_ref, *rest, reverse, n_cast):
    cast_in, o_ref, cast_out, s_scr = rest[:n_cast], rest[n_cast], rest[n_cast + 1:-1], rest[-1]
    for src, dst in zip(cast_in, cast_out):
        dst[...] = src[...].astype(BF16)

    @pl.when(pl.program_id(1) == 0)
    def _():
        s_scr[...] = jnp.zeros_like(s_scr)

    row = lax.broadcasted_iota(jnp.int32, (GLA_CHUNK, GLA_CHUNK), 0)
    col = lax.broadcasted_iota(jnp.int32, (GLA_CHUNK, GLA_CHUNK), 1)
    keep = (col >= row) if reverse else (col <= row)
    tri = jnp.where(keep, 1.0, 0.0).astype(BF16)
    ones = jnp.ones((GLA_CHUNK, LANE), BF16)
    n_chunks = GLA_TB // GLA_CHUNK
    order = range(n_chunks - 1, -1, -1) if reverse else range(n_chunks)
    for ci in order:
        sl = slice(ci * GLA_CHUNK, (ci + 1) * GLA_CHUNK)
        cos = cos_ref[sl, :]
        sin = sin_ref[sl, :]
        z = _dot(glr_ref[sl, :].astype(BF16), wg_ref[0]) + bg_ref[0]
        g = (jnp.minimum(z, 0.0) - jnp.log1p(jnp.exp(-jnp.abs(z)))) * (1.0 / GLA_GATE_NORM)
        g_hi, g_lo = _split_bf16(g)
        cum_all = _dot(tri, g_hi) + _dot(tri, g_lo)
        decay_all = jnp.exp(_dot_tn(g_hi, ones) + _dot_tn(g_lo, ones))
        for hd in range(GLA_H):
            ks = slice(hd * GLA_DK, (hd + 1) * GLA_DK)
            vs = slice(hd * GLA_DV, (hd + 1) * GLA_DV)
            q = _rope(q_ref[sl, ks].astype(F32), cos, sin)
            k = _rope(k_ref[sl, ks].astype(F32), cos, sin)
            v = v_ref[sl, vs]
            cum = cum_all[:, ks]
            tot = cum[0:1, :] if reverse else cum[GLA_CHUNK - 1:GLA_CHUNK, :]
            q_dec = (q * jnp.exp(cum)).astype(BF16)
            k_inv = (k * jnp.exp(-cum)).astype(BF16)
            k_end = (k * jnp.exp(tot - cum)).astype(BF16)
            att = jnp.where(keep, _dot_nt(q_dec, k_inv), 0.0).astype(BF16)
            state = s_scr[hd]
            o_ref[sl, vs] = (_dot(att, v) + _dot(q_dec, state.astype(BF16))).astype(o_ref.dtype)
            decay = jnp.concatenate([decay_all[ks, :]] * (GLA_DV // LANE), axis=1)
            s_scr[hd] = state * decay + _dot_tn(k_end, v)


def _gla_scan(p, glr, wg, bg, cos, sin, reverse, casts):
    lat_blocks = SEQ // GLA_TB

    def tblock(s):
        if reverse:
            return jnp.where(s == 0, 0, GLA_STEPS - s)
        return s

    def rows(b, s):
        tb = tblock(s)
        return jnp.where(tb == 0, R_LAT // GLA_TB + b, b * lat_blocks + tb - 1)

    def rope_rows(s):
        tb = tblock(s)
        return jnp.where(tb == 0, lat_blocks, tb - 1)

    d = 1 if reverse else 0
    c_in, c_shape, c_out, c_args = _cast_specs(casts, lambda b, s: b * GLA_STEPS + s)
    return pl.pallas_call(
        functools.partial(_gla_kernel, reverse=reverse, n_cast=len(casts)),
        out_shape=[jax.ShapeDtypeStruct((R_ALL, GLA_VD), BF16), *c_shape],
        grid=(B, GLA_STEPS),
        in_specs=[
            pl.BlockSpec((GLA_TB, GLA_QK), lambda b, s: (rows(b, s), 0)),
            pl.BlockSpec((GLA_TB, GLA_QK), lambda b, s: (rows(b, s), 1)),
            pl.BlockSpec((GLA_TB, GLA_VD), lambda b, s: (rows(b, s), 2 * GLA_QK // GLA_VD)),
            pl.BlockSpec((GLA_TB, LANE), lambda b, s: (rows(b, s), 0)),
            pl.BlockSpec((1, LANE, GLA_QK), lambda b, s: (d, 0, 0)),
            pl.BlockSpec((1, 1, GLA_QK), lambda b, s: (d, 0, 0)),
            pl.BlockSpec((GLA_TB, GLA_DK), lambda b, s: (rope_rows(s), 0)),
            pl.BlockSpec((GLA_TB, GLA_DK), lambda b, s: (rope_rows(s), 0)),
            *c_in,
        ],
        out_specs=[pl.BlockSpec((GLA_TB, GLA_VD), lambda b, s: (rows(b, s), 0)), *c_out],
        scratch_shapes=[pltpu.VMEM((GLA_H, GLA_DK, GLA_DV), F32)],
        compiler_params=_cparams(2),
        name="gla_scan_bwd" if reverse else "gla_scan_fwd",
    )(p, p, p, glr, wg, bg, cos, sin, *c_args)


def _nat_bias_table(rpb):
    qc = np.arange(GRID_W)[:, None]
    kc = np.arange(GRID_W)[None, :]
    cs = np.clip(qc - WIN_C // 2, 0, GRID_W - WIN_C)
    col_ok = (kc >= cs) & (kc < cs + WIN_C)
    dc = np.clip(kc - qc + WIN_C - 1, 0, 2 * WIN_C - 2)
    pick = (dc[:, :, None] == np.arange(2 * WIN_C - 1)).astype(np.float32)
    tc = jnp.einsum('had,qkd->hqak', rpb.astype(F32), jnp.asarray(pick), precision=lax.Precision.HIGHEST)
    tc = jnp.where(col_ok[None, :, None, :], tc, NEG_INF)
    n_dr = 2 * WIN_R - 1
    pad = NAT_BAND - 1
    strip = jnp.pad(tc.reshape(NAT_H, GRID_W, n_dr * GRID_W), ((0, 0), (0, 0), (pad * GRID_W, pad * GRID_W)),
                    constant_values=NEG_INF)
    rows_n = SEQ // GRID_W
    half = WIN_R // 2
    n_keys = NAT_BAND * GRID_W
    bands, valid = [], np.zeros((3, NAT_QR, 1, NAT_BAND, 1), bool)
    for p_i, (r0, bs) in enumerate(((0, 0), (NAT_QR, NAT_QR - half), (rows_n - NAT_QR, rows_n - NAT_BAND))):
        for i in range(NAT_QR):
            qr = r0 + i
            rs = min(max(qr - half, 0), rows_n - WIN_R)
            first = (pad + bs - qr + WIN_R - 1) * GRID_W
            bands.append(strip[:, :, first:first + n_keys])
            valid[p_i, i, 0, :, 0] = [rs <= kr < rs + WIN_R for kr in range(bs, bs + NAT_BAND)]
    tab = jnp.stack(bands).reshape(3, NAT_QR, NAT_H, GRID_W, n_keys)
    valid = np.broadcast_to(valid, (3, NAT_QR, GRID_W, NAT_BAND, GRID_W)).reshape(3, NAT_QR, 1, GRID_W, n_keys)
    return jnp.where(valid, tab, NEG_INF)


def _nat_kernel(q_ref, k_ref, v_ref, kc_ref, vc_ref, bias_ref, o_ref):
    rb = pl.program_id(2)
    rows_n = SEQ // GRID_W
    band_start = jnp.clip(rb * NAT_QR - WIN_R // 2, 0, rows_n - NAT_BAND)
    start = pl.multiple_of(band_start * GRID_W, GRID_W)
    n_keys = NAT_BAND * GRID_W
    for g in range(NAT_G):
        hs = slice(g * NAT_DH, (g + 1) * NAT_DH)
        q = q_ref[:, hs]
        bias = bias_ref[0, :, g].reshape(NAT_QR * GRID_W, n_keys)
        s_loc = _dot_nt(q, k_ref[pl.ds(start, n_keys), hs]) + bias
        s_ctx = _dot_nt(q, kc_ref[:, hs])
        m = jnp.maximum(jnp.max(s_loc, axis=1, keepdims=True), jnp.max(s_ctx, axis=1, keepdims=True))
        p_loc = jnp.exp(s_loc - m)
        p_ctx = jnp.exp(s_ctx - m)
        denom = jnp.sum(p_loc, axis=1, keepdims=True) + jnp.sum(p_ctx, axis=1, keepdims=True)
        o = _dot(p_loc.astype(BF16), v_ref[pl.ds(start, n_keys), hs]) + _dot(p_ctx.astype(BF16), vc_ref[:, hs])
        o_ref[:, hs] = (o / denom).astype(o_ref.dtype)


def _nat_attention(p, bias_tab):
    n_rb = SEQ // (NAT_QR * GRID_W)
    tq = NAT_QR * GRID_W
    width = NAT_G * NAT_DH
    hq = D // width

    def pattern(rb):
        return jnp.where(rb == 0, 0, jnp.where(rb == n_rb - 1, 2, 1))

    return pl.pallas_call(
        _nat_kernel,
        out_shape=jax.ShapeDtypeStruct((R_LAT, D), BF16),
        grid=(B, NAT_H // NAT_G, n_rb),
        in_specs=[
            pl.BlockSpec((tq, width), lambda b, h, rb: (b * n_rb + rb, h)),
            pl.BlockSpec((SEQ, width), lambda b, h, rb: (b, hq + h)),
            pl.BlockSpec((SEQ, width), lambda b, h, rb: (b, 2 * hq + h)),
            pl.BlockSpec((CTX, width), lambda b, h, rb: (R_LAT // CTX + b, hq + h)),
            pl.BlockSpec((CTX, width), lambda b, h, rb: (R_LAT // CTX + b, 2 * hq + h)),
            pl.BlockSpec((1, NAT_QR, NAT_G, GRID_W, NAT_BAND * GRID_W), lambda b, h, rb: (pattern(rb), 0, h, 0, 0)),
        ],
        out_specs=pl.BlockSpec((tq, width), lambda b, h, rb: (b * n_rb + rb, h)),
        compiler_params=_cparams(3),
        name="nat_attention",
    )(p, p, p, p, p, bias_tab)


MIX_TM = 256


def _layer_norm(u, g, b):
    mu = jnp.mean(u, axis=-1, keepdims=True)
    var = jnp.mean(jnp.square(u - mu), axis=-1, keepdims=True)
    return (u - mu) * lax.rsqrt(var + LN_EPS) * g + b


def _mix_out_kernel(*refs, gla, lat_tiles):
    if gla:
        of_ref, ob_ref, r_ref, ng_ref = refs[:4]
        refs = refs[4:]
    else:
        x_ref = refs[0]
        refs = refs[1:]
    (h_ref, hc_ref, mod_ref, w_ref, lng_ref, lnb_ref, wr_ref, br_ref,
     h1_ref, f_ref, rt_ref, cnt_ref, y_even, y_odd, cnt_scr) = refs
    s = pl.program_id(0)

    @pl.when(s == 0)
    def _():
        cnt_scr[...] = jnp.zeros_like(cnt_scr)
        y_odd[...] = jnp.zeros_like(y_odd)

    def step(y_new, y_prev):
        if gla:
            parts = []
            for hd in range(GLA_H):
                cs = slice(hd * GLA_DV, (hd + 1) * GLA_DV)
                o = of_ref[:, cs].astype(F32) + ob_ref[:, cs].astype(F32)
                o = o * lax.rsqrt(jnp.mean(jnp.square(o), axis=-1, keepdims=True) + LN_EPS) * ng_ref[...]
                r = r_ref[:, cs].astype(F32)
                parts.append((o * (r * jax.nn.sigmoid(r))).astype(BF16))
            x = jnp.concatenate(parts, axis=1)
        else:
            x = x_ref[...]
        y_new[...] = _dot(x, w_ref[...])

        gate1 = mod_ref[0, 2:3, :]
        h_in = jnp.where(s - 1 < lat_tiles, h_ref[...], hc_ref[...])
        h1 = _layer_norm(ALPHA * h_in + gate1 * y_prev[...], lng_ref[...], lnb_ref[...])
        h1_ref[...] = h1
        f = h1 * (1.0 + mod_ref[0, 4:5, :]) + mod_ref[0, 3:4, :]
        f_ref[...] = _pack_rows(f)
        live = jnp.where(s > 0, 1.0, 0.0)
        logits = _dot(f.astype(BF16), wr_ref[...]) + br_ref[...]
        route, counts = _route_tile(logits, cnt_scr[...], live)
        rt_ref[...] = route
        cnt_scr[...] = counts
        cnt_ref[...] = jnp.broadcast_to(counts, cnt_ref.shape)

    @pl.when(s % 2 == 0)
    def _():
        step(y_even, y_odd)

    @pl.when(s % 2 == 1)
    def _():
        step(y_odd, y_even)


RT_E, RT_W, RT_RANK = 0, 2, 4


def _route_tile(logits, run, live):
    tm = logits.shape[0]
    lane = lax.broadcasted_iota(jnp.int32, (tm, LANE), 1).astype(F32)

    def first_max(vals):
        top = jnp.max(vals, axis=1, keepdims=True)
        return top, jnp.min(jnp.where(vals == top, lane, float(LANE)), axis=1, keepdims=True)

    gl = jnp.where(lane < N_GROUPS, logits, NEG_INF)
    g_max, g_idx = first_max(gl)
    g_p = 1.0 / jnp.sum(jnp.exp(gl - g_max), axis=1, keepdims=True)
    lo = N_GROUPS + EPG * g_idx
    el = jnp.where((lane >= lo) & (lane < lo + EPG), logits, NEG_INF)
    e1, i1 = first_max(el)
    e2, i2 = first_max(jnp.where(lane == i1, NEG_INF, el))
    t = jnp.exp(e2 - e1)
    p1 = 1.0 / (1.0 + t)
    x1 = i1 - N_GROUPS
    x2 = i2 - N_GROUPS
    onehot = jnp.where((lane == x1) | (lane == x2), 1.0, 0.0)
    row = lax.broadcasted_iota(jnp.int32, (tm, tm), 0)
    col = lax.broadcasted_iota(jnp.int32, (tm, tm), 1)
    before = jnp.where(col < row, 1.0, 0.0).astype(BF16)
    seen = _dot(before, onehot.astype(BF16)) + run
    rank1 = jnp.sum(jnp.where(lane == x1, seen, 0.0), axis=1, keepdims=True)
    rank2 = jnp.sum(jnp.where(lane == x2, seen, 0.0), axis=1, keepdims=True)
    route = jnp.zeros((tm, LANE), F32)
    for k, val in ((RT_E, x1), (RT_E + 1, x2), (RT_W, g_p * p1), (RT_W + 1, g_p * p1 * t),
                   (RT_RANK, rank1), (RT_RANK + 1, rank2)):
        route = jnp.where(lane == k, val, route)
    return route, run + live * jnp.sum(onehot, axis=0, keepdims=True)


def _mix_out(mix_inputs, stream, mod, w_out, ln_g, ln_b, w_router, b_router, rows, gla):
    n_tiles = rows // MIX_TM
    row_spec = lambda width: pl.BlockSpec((MIX_TM, width), lambda i: (jnp.minimum(i, n_tiles - 1), 0))
    done_spec = lambda width: pl.BlockSpec((MIX_TM, width), lambda i: (jnp.maximum(i - 1, 0), 0))
    full = lambda a: pl.BlockSpec(a.shape, lambda i: (0,) * a.ndim)
    if gla:
        o_f, o_b, p, norm_g = mix_inputs
        in_specs = [row_spec(GLA_VD), row_spec(GLA_VD),
                    pl.BlockSpec((MIX_TM, GLA_VD),
                                 lambda i: (jnp.minimum(i, n_tiles - 1), (2 * GLA_QK + GLA_VD) // GLA_VD)),
                    full(norm_g)]
        args = [o_f, o_b, p, norm_g]
    else:
        (x,) = mix_inputs
        in_specs = [row_spec(D)]
        args = [x]
    in_specs += [*_stream_specs(stream, MIX_TM, lag=1),
                 pl.BlockSpec((1, 6, D), lambda i: (_mod_row(MIX_TM)(jnp.maximum(i - 1, 0)), 0, 0)),
                 full(w_out), full(ln_g), full(ln_b), full(w_router), full(b_router)]
    args += [stream[0], stream[1], mod, w_out, ln_g, ln_b, w_router, b_router]
    return pl.pallas_call(
        functools.partial(_mix_out_kernel, gla=gla, lat_tiles=R_LAT // MIX_TM),
        out_shape=[jax.ShapeDtypeStruct((rows, D), F32), jax.ShapeDtypeStruct((rows, D_PACK), jnp.uint32),
                   jax.ShapeDtypeStruct((rows, LANE), F32), jax.ShapeDtypeStruct((SUBLANE, LANE), F32)],
        grid=(n_tiles + 1,),
        in_specs=in_specs,
        out_specs=[done_spec(D), done_spec(D_PACK), done_spec(LANE), pl.BlockSpec((SUBLANE, LANE), lambda i: (0, 0))],
        scratch_shapes=[pltpu.VMEM((MIX_TM, D), F32), pltpu.VMEM((MIX_TM, D), F32), pltpu.VMEM((1, LANE), F32)],
        compiler_params=_cparams(1),
        name="gla_out_ln" if gla else "nat_out_ln",
    )(*args)


def _plan(route, counts_row):
    t = route.shape[0]
    e = route[:, RT_E:RT_E + 2].astype(jnp.int32)
    rank = route[:, RT_RANK:RT_RANK + 2].astype(jnp.int32)
    counts = counts_row[0, :N_EXPERTS].astype(jnp.int32)
    tiles_per = (counts + MOE_TM - 1) // MOE_TM
    tile_end = jnp.cumsum(tiles_per)
    tile_start = tile_end - tiles_per
    n_tiles = 2 * t // MOE_TM + N_EXPERTS
    ids = jnp.arange(n_tiles, dtype=jnp.int32)
    used = tile_end[-1]
    tile_valid = (ids < used).astype(jnp.int32)
    lookup = jnp.minimum(ids, used - 1)
    tile_expert = jnp.sum((tile_end[None, :] <= lookup[:, None]).astype(jnp.int32), axis=1)
    tile_expert = jnp.minimum(tile_expert, N_EXPERTS - 1).astype(jnp.int32)
    base = tile_start * MOE_TM
    experts = jnp.arange(N_EXPERTS, dtype=jnp.int32)
    pos = jnp.sum(jnp.where(e[..., None] == experts, base, 0), axis=-1) + rank
    trailing = n_tiles - 1 - experts
    fill_tile = jnp.concatenate([jnp.maximum(tile_end - 1, 0), trailing]).astype(jnp.int32)
    fill_on = jnp.concatenate([tiles_per > 0, trailing >= used]).astype(jnp.int32)
    return tile_expert, tile_valid, pos.reshape(-1).astype(jnp.int32), fill_tile, fill_on, n_tiles


def _start_row_copies(pos_ref, tile, rows, make, priorities):
    base = 2 * tile * rows
    for r in range(rows):
        make(r, pos_ref[base + 2 * r], False).start(priority=priorities[0])
        make(r, pos_ref[base + 2 * r + 1], True).start(priority=priorities[1])


DISP_TM = 256


def _dispatch_kernel(pos_ref, lt_ref, ht_ref, f_ref, xs_hbm, sbuf, zbuf, sem, zsem):
    i = pl.program_id(0)
    n = pl.num_programs(0)
    slot = i % 2

    def fill(e):
        return pltpu.make_async_copy(zbuf, xs_hbm.at[pl.ds(lt_ref[e] * MOE_TM, MOE_TM), :], zsem)

    @pl.when(i == 0)
    def _():
        zbuf[...] = jnp.zeros_like(zbuf)
        for e in range(2 * N_EXPERTS):
            @pl.when(ht_ref[e] != 0)
            def _():
                fill(e).start()
        for e in range(2 * N_EXPERTS):
            @pl.when(ht_ref[e] != 0)
            def _():
                fill(e).wait()

    def wait_scatter(sl):
        for _ in range(2):
            pltpu.make_async_copy(sbuf.at[sl], xs_hbm.at[pl.ds(0, DISP_TM), :], sem.at[sl]).wait()

    @pl.when(i >= 2)
    def _():
        wait_scatter(slot)

    sbuf[slot] = f_ref[...]

    def make(r, dst, second):
        return pltpu.make_async_copy(sbuf.at[slot, pl.ds(r, 1), :], xs_hbm.at[pl.ds(dst, 1), :], sem.at[slot])

    _start_row_copies(pos_ref, i, DISP_TM, make, (0, 1))

    @pl.when(i == n - 1)
    def _():
        wait_scatter(slot)

        @pl.when(n >= 2)
        def _():
            wait_scatter(1 - slot)


def _dispatch(f, pos, last_tile, has_tile, cap):
    t = f.shape[0]
    grid_spec = pltpu.PrefetchScalarGridSpec(
        num_scalar_prefetch=3,
        grid=(t // DISP_TM,),
        in_specs=[pl.BlockSpec((DISP_TM, D_PACK), lambda i, *_: (i, 0))],
        out_specs=pl.BlockSpec(memory_space=pl.ANY),
        scratch_shapes=[
            pltpu.VMEM((2, DISP_TM, D_PACK), jnp.uint32),
            pltpu.VMEM((MOE_TM, D_PACK), jnp.uint32),
            pltpu.SemaphoreType.DMA((2,)),
            pltpu.SemaphoreType.DMA,
        ],
    )
    return pl.pallas_call(
        _dispatch_kernel,
        out_shape=jax.ShapeDtypeStruct((cap, D_PACK), jnp.uint32),
        grid_spec=grid_spec,
        compiler_params=_cparams(1),
        name="moe_dispatch",
    )(pos, last_tile, has_tile, f)


def _moe_kernel(te_ref, tv_ref, x_ref, w1_ref, w3_ref, w2_ref, y_ref):
    i = pl.program_id(0)

    @pl.when(tv_ref[i] != 0)
    def _():
        x = _unpack_rows(x_ref[...]).astype(BF16)
        h1 = _dot(x, w1_ref[...])
        h3 = _dot(x, w3_ref[...])
        a = (h1 * jax.nn.sigmoid(h1) * h3).astype(BF16)
        y_ref[...] = _pack_rows(_dot(a, w2_ref[...]))

    @pl.when(tv_ref[i] == 0)
    def _():
        y_ref[...] = jnp.zeros_like(y_ref)


def _moe_experts(xs, tile_expert, tile_valid, w1, w3, w2):
    cap = xs.shape[0]
    grid_spec = pltpu.PrefetchScalarGridSpec(
        num_scalar_prefetch=2,
        grid=(cap // MOE_TM,),
        in_specs=[
            pl.BlockSpec((MOE_TM, D_PACK), lambda i, te, tv: (i * tv[i], 0)),
            pl.BlockSpec((D, D_EXPERT), lambda i, te, tv: (te[i], 0)),
            pl.BlockSpec((D, D_EXPERT), lambda i, te, tv: (te[i], 0)),
            pl.BlockSpec((D_EXPERT, D), lambda i, te, tv: (te[i], 0)),
        ],
        out_specs=pl.BlockSpec((MOE_TM, D_PACK), lambda i, te, tv: (i, 0)),
    )
    return pl.pallas_call(
        _moe_kernel,
        out_shape=jax.ShapeDtypeStruct((cap, D_PACK), jnp.uint32),
        grid_spec=grid_spec,
        compiler_params=_cparams(1),
        name="moe_experts",
    )(tile_expert, tile_valid, xs, w1, w3, w2)


FIN_TM = 256


def _final_kernel(pos_ref, h_ref, rt_ref, mod_ref, lng_ref, lnb_ref, ys_hbm, o_ref, abuf, bbuf, sem):
    i = pl.program_id(0)
    n = pl.num_programs(0)
    slot = i % 2

    def start_gather(tile, sl):
        def make(r, src, second):
            buf = bbuf if second else abuf
            return pltpu.make_async_copy(ys_hbm.at[pl.ds(src, 1), :], buf.at[sl, pl.ds(r, 1), :], sem.at[sl])

        _start_row_copies(pos_ref, tile, FIN_TM, make, (1, 1))

    @pl.when(i == 0)
    def _():
        start_gather(0, 0)

    @pl.when(i + 1 < n)
    def _():
        start_gather(i + 1, 1 - slot)

    for buf in (abuf, bbuf):
        pltpu.make_async_copy(ys_hbm.at[pl.ds(0, FIN_TM), :], buf.at[slot], sem.at[slot]).wait()

    y = (_unpack_rows(abuf[slot]) * rt_ref[:, RT_W:RT_W + 1]
         + _unpack_rows(bbuf[slot]) * rt_ref[:, RT_W + 1:RT_W + 2])
    gate2 = mod_ref[0, 5:6, :]
    o_ref[...] = _layer_norm(ALPHA * h_ref[...] + gate2 * y, lng_ref[...], lnb_ref[...])


def _ffn_residual(h1, route, pos, ys, mod, ln_g, ln_b):
    rows = h1.shape[0]
    grid_spec = pltpu.PrefetchScalarGridSpec(
        num_scalar_prefetch=1,
        grid=(rows // FIN_TM,),
        in_specs=[
            pl.BlockSpec((FIN_TM, D), lambda i, *_: (i, 0)),
            pl.BlockSpec((FIN_TM, LANE), lambda i, *_: (i, 0)),
            pl.BlockSpec((1, 6, D), lambda i, *_: (_mod_row(FIN_TM)(i), 0, 0)),
            pl.BlockSpec((1, D), lambda i, *_: (0, 0)),
            pl.BlockSpec((1, D), lambda i, *_: (0, 0)),
            pl.BlockSpec(memory_space=pl.ANY),
        ],
        out_specs=pl.BlockSpec((FIN_TM, D), lambda i, *_: (i, 0)),
        scratch_shapes=[
            pltpu.VMEM((2, FIN_TM, D_PACK), jnp.uint32),
            pltpu.VMEM((2, FIN_TM, D_PACK), jnp.uint32),
            pltpu.SemaphoreType.DMA((2,)),
        ],
    )
    return pl.pallas_call(
        _final_kernel,
        out_shape=jax.ShapeDtypeStruct((rows, D), F32),
        grid_spec=grid_spec,
        compiler_params=_cparams(1),
        name="ffn_residual_ln",
    )(pos, h1, route, mod, ln_g, ln_b, ys)


def _moe_ffn(h1, f, route, counts, mod, w1, w3, w2, ln_g, ln_b):
    tile_expert, tile_valid, pos, fill_tile, fill_on, n_tiles = _plan(route, counts)
    xs = _dispatch(f, pos, fill_tile, fill_on, n_tiles * MOE_TM)
    ys = _moe_experts(xs, tile_expert, tile_valid, w1, w3, w2)
    return _ffn_residual(h1, route, pos, ys, mod, ln_g, ln_b)


def _router_weights(w_group, b_group, w_expert, b_expert):
    pad = LANE - N_GROUPS - N_EXPERTS
    w = jnp.concatenate([w_group, w_expert, jnp.zeros((D, pad), F32)], axis=1).astype(BF16)
    b = jnp.concatenate([b_group, b_expert, jnp.zeros((pad,), F32)])[None, :]
    return w, b


def kernel(x, c, ctx, c_ctx, ada_w, ada_b, ln_g, ln_b, gla_w_in, gla_w_gate, gla_b_gate, gla_norm_g, gla_w_out,
           nat_w_in, nat_rpb, nat_w_out, moe_w_group, moe_b_group, moe_w_expert, moe_b_expert,
           moe_w1, moe_w3, moe_w2):
    stream = (x.reshape(R_LAT, D), ctx.reshape(R_CTX, D), 0)
    cc = jnp.concatenate([c, c_ctx[None, :], jnp.zeros((SUBLANE - B - 1, D), F32)], axis=0)
    mod = _ada_table(cc, ada_w, ada_b).reshape(DEPTH, SUBLANE, 6, D)
    w1_all = moe_w1.reshape(DEPTH * N_EXPERTS * D, D_EXPERT)
    w3_all = moe_w3.reshape(DEPTH * N_EXPERTS * D, D_EXPERT)
    w2_all = moe_w2.reshape(DEPTH * N_EXPERTS * D_EXPERT, D)

    def expert_casts(layer):
        up, down = N_EXPERTS * D, N_EXPERTS * D_EXPERT
        return ((w1_all, up, layer * up), (w3_all, up, layer * up), (w2_all, down, layer * down))

    w_in = gla_w_in[0]
    w_glr = jnp.concatenate([w_in[:, GLA_MAIN:], jnp.zeros((D, LANE - 2 * GLA_RANK), F32)], axis=1).astype(BF16)
    p, glr = _proj(stream, mod[0], w_in.astype(BF16), GLA_MAIN, GLA_QK, GLA_DK ** -0.5, w_glr)
    wg = jnp.zeros((2, LANE, GLA_QK), F32)
    wg = wg.at[0, :GLA_RANK].set(gla_w_gate[0, 0]).at[1, GLA_RANK:2 * GLA_RANK].set(gla_w_gate[0, 1]).astype(BF16)
    bg = gla_b_gate[0][:, None, :]
    cos, sin = _rope_tables()
    o_f, *experts0 = _gla_scan(p, glr, wg, bg, cos, sin, False, expert_casts(0))
    o_b, *experts1 = _gla_scan(p, glr, wg, bg, cos, sin, True, expert_casts(1))
    w_r, b_r = _router_weights(moe_w_group[0], moe_b_group[0], moe_w_expert[0], moe_b_expert[0])
    h1, f, route, counts = _mix_out((o_f, o_b, p, gla_norm_g[0][None, :]), stream, mod[0], gla_w_out[0].astype(BF16),
                                    ln_g[0, 0][None, :], ln_b[0, 0][None, :], w_r, b_r, R_ALL, gla=True)
    h = _moe_ffn(h1, f, route, counts, mod[0], *experts0, ln_g[0, 1][None, :], ln_b[0, 1][None, :])

    stream = (h, h, R_LAT)
    p = _proj(stream, mod[1], nat_w_in[0].astype(BF16), 3 * D, D, NAT_DH ** -0.5)
    o = _nat_attention(p, _nat_bias_table(nat_rpb[0]))
    w_r, b_r = _router_weights(moe_w_group[1], moe_b_group[1], moe_w_expert[1], moe_b_expert[1])
    h1, f, route, counts = _mix_out((o,), stream, mod[1], nat_w_out[0].astype(BF16),
                                    ln_g[1, 0][None, :], ln_b[1, 0][None, :], w_r, b_r, R_LAT, gla=False)
    out = _moe_ffn(h1, f, route, counts, mod[1], *experts1, ln_g[1, 1][None, :], ln_b[1, 1][None, :])
    return out.reshape(B, SEQ, D)
```

```python
import functools

import jax
import jax.numpy as jnp
import numpy as np
from jax import lax
from jax.experimental import pallas as pl
from jax.experimental.pallas import tpu as pltpu

F32 = jnp.float32
BF16 = jnp.bfloat16

D = 2048
B = 4
SEQ = 4096
DEPTH = 2
GRID_W = 64
CTX = 256
R_LAT = B * SEQ
R_CTX = B * CTX
R_ALL = R_LAT + R_CTX

GLA_H = 4
GLA_DK = 256
GLA_DV = 512
GLA_CHUNK = 64
GLA_RANK = 16
GLA_QK = GLA_H * GLA_DK
GLA_VD = GLA_H * GLA_DV
GLA_MAIN = 2 * GLA_QK + 2 * GLA_VD
GLA_GATE_NORM = 16.0
ROPE_BASE = 10000.0

NAT_H = 16
NAT_DH = 128
WIN_R = 8
WIN_C = 16
NAT_QR = 4
NAT_BAND = 12
NAT_G = 8

N_GROUPS = 4
EPG = 4
N_EXPERTS = 16
D_EXPERT = 1024
MOE_TM = 256

ALPHA = (2 * DEPTH) ** 0.25
LN_EPS = 1e-5
NEG_INF = -1e30

LANE = 128
SUBLANE = 8
VMEM_LIMIT = 56 * 1024 * 1024


def _cparams(n_axes):
    return pltpu.CompilerParams(dimension_semantics=("arbitrary",) * n_axes,
                                vmem_limit_bytes=VMEM_LIMIT)


def _split_bf16(x):
    hi = x.astype(BF16)
    lo = (x - hi.astype(F32)).astype(BF16)
    return hi, lo


def _dot(a, b):
    return jnp.dot(a, b, preferred_element_type=F32)


def _dot_hi(a, b):
    ah, al = _split_bf16(a)
    bh, bl = _split_bf16(b)
    return _dot(ah, bh) + _dot(al, bh) + _dot(ah, bl)


def _dot_nt(a, b):
    return lax.dot_general(a, b, (((1,), (1,)), ((), ())), preferred_element_type=F32)


def _dot_tn(a, b):
    return lax.dot_general(a, b, (((0,), (0,)), ((), ())), preferred_element_type=F32)


D_PACK = D // 2


def _pack_rows(x):
    lo = pltpu.bitcast(x[:, :D_PACK].astype(BF16).astype(F32), jnp.uint32)
    hi = pltpu.bitcast(x[:, D_PACK:].astype(BF16).astype(F32), jnp.uint32)
    return (lo >> 16) | hi


def _unpack_rows(u):
    lo = pltpu.bitcast(u << 16, F32)
    hi = pltpu.bitcast(u & jnp.uint32(0xFFFF0000), F32)
    return jnp.concatenate([lo, hi], axis=1)


CAST_STEPS = 64


def _cast_specs(casts, step_of):
    in_specs, out_shape, out_specs, args = [], [], [], []
    for arr, n_rows, first_row in casts:
        c_rows = n_rows // CAST_STEPS
        first = first_row // c_rows
        chunk = lambda *idx: jnp.minimum(step_of(*idx), CAST_STEPS - 1)
        in_specs.append(pl.BlockSpec((c_rows, arr.shape[1]), lambda *idx, first=first: (first + chunk(*idx), 0)))
        out_shape.append(jax.ShapeDtypeStruct((n_rows, arr.shape[1]), BF16))
        out_specs.append(pl.BlockSpec((c_rows, arr.shape[1]), lambda *idx: (chunk(*idx), 0)))
        args.append(arr)
    return in_specs, out_shape, out_specs, args


def _mod_row(tile_rows):
    return lambda i: jnp.minimum((i * tile_rows) // SEQ, B)


def _stream_specs(stream, tile_rows, lag=0):
    n_lat = R_LAT // tile_rows
    base = stream[2] // tile_rows
    return (pl.BlockSpec((tile_rows, D), lambda i, *_: (jnp.clip(i - lag, 0, n_lat - 1), 0)),
            pl.BlockSpec((tile_rows, D), lambda i, *_: (base + jnp.maximum(i - lag - n_lat, 0), 0)))


ADA_TN = 1024


def _ada_kernel(c_ref, w_ref, b_ref, o_ref):
    x = c_ref[...]
    s = x * jax.nn.sigmoid(x)
    o_ref[0] = _dot_hi(s, w_ref[0]) + b_ref[0]


def _ada_table(cc, ada_w, ada_b):
    n = 6 * D
    return pl.pallas_call(
        _ada_kernel,
        out_shape=jax.ShapeDtypeStruct((DEPTH, SUBLANE, n), F32),
        grid=(DEPTH, n // ADA_TN),
        in_specs=[
            pl.BlockSpec((SUBLANE, D), lambda l, j: (0, 0)),
            pl.BlockSpec((1, D, ADA_TN), lambda l, j: (l, 0, j)),
            pl.BlockSpec((1, 1, ADA_TN), lambda l, j: (l, 0, j)),
        ],
        out_specs=pl.BlockSpec((1, SUBLANE, ADA_TN), lambda l, j: (l, 0, j)),
        compiler_params=_cparams(2),
        name="ada_table",
    )(cc, ada_w, ada_b.reshape(DEPTH, 1, n))


PROJ_TM = 1024
PROJ_TN = 1024


def _proj_kernel(h_ref, hc_ref, mod_ref, w_ref, *rest, with_extra, q_tiles, q_scale):
    if with_extra:
        wx_ref, o_ref, ox_ref, a_scr = rest
    else:
        o_ref, a_scr = rest
    j = pl.program_id(1)

    @pl.when(j == 0)
    def _():
        sh = mod_ref[0, 0:1, :]
        sc = mod_ref[0, 1:2, :]
        h = jnp.where(pl.program_id(0) < R_LAT // PROJ_TM, h_ref[...], hc_ref[...])
        a_scr[...] = (h * (1.0 + sc) + sh).astype(BF16)
        if with_extra:
            ox_ref[...] = _dot(a_scr[...], wx_ref[...])

    scale = jnp.where(j < q_tiles, q_scale, 1.0)
    o_ref[...] = (_dot(a_scr[...], w_ref[...]) * scale).astype(o_ref.dtype)


def _proj(stream, mod, w, n, q_cols, q_scale, w_extra=None):
    rows = R_ALL
    with_extra = w_extra is not None
    in_specs = [
        *_stream_specs(stream, PROJ_TM),
        pl.BlockSpec((1, 6, D), lambda i, j: (_mod_row(PROJ_TM)(i), 0, 0)),
        pl.BlockSpec((D, PROJ_TN), lambda i, j: (0, j)),
    ]
    out_shape = [jax.ShapeDtypeStruct((rows, n), BF16)]
    out_specs = [pl.BlockSpec((PROJ_TM, PROJ_TN), lambda i, j: (i, j))]
    args = [stream[0], stream[1], mod, w]
    if with_extra:
        in_specs.append(pl.BlockSpec((D, LANE), lambda i, j: (0, 0)))
        out_shape.append(jax.ShapeDtypeStruct((rows, LANE), F32))
        out_specs.append(pl.BlockSpec((PROJ_TM, LANE), lambda i, j: (i, 0)))
        args.append(w_extra)
    out = pl.pallas_call(
        functools.partial(_proj_kernel, with_extra=with_extra, q_tiles=q_cols // PROJ_TN, q_scale=q_scale),
        out_shape=out_shape,
        grid=(rows // PROJ_TM, n // PROJ_TN),
        in_specs=in_specs,
        out_specs=out_specs,
        scratch_shapes=[pltpu.VMEM((PROJ_TM, D), BF16)],
        compiler_params=_cparams(2),
        name="mod_proj",
    )(*args)
    return out if with_extra else out[0]


GLA_TB = 256
GLA_STEPS = 1 + SEQ // GLA_TB


def _rope_tables():
    half = GLA_DK // 4
    freqs = ROPE_BASE ** (-np.arange(half, dtype=np.float64) / half)
    t = np.arange(SEQ)
    ang_r = (t // GRID_W)[:, None] * freqs
    ang_c = (t % GRID_W)[:, None] * freqs
    cos = np.concatenate([np.cos(ang_r)] * 2 + [np.cos(ang_c)] * 2, axis=1)
    sin = np.concatenate([-np.sin(ang_r), np.sin(ang_r), -np.sin(ang_c), np.sin(ang_c)], axis=1)
    cos = np.concatenate([cos, np.ones((GLA_TB, GLA_DK))], axis=0)
    sin = np.concatenate([sin, np.zeros((GLA_TB, GLA_DK))], axis=0)
    return jnp.asarray(cos, F32), jnp.asarray(sin, F32)


def _rope(x, cos, sin):
    half = LANE // 2
    parts = [pltpu.roll(x[:, g * LANE:(g + 1) * LANE], half, 1) for g in range(x.shape[1] // LANE)]
    return x * cos + jnp.concatenate(parts, axis=1) * sin


def _gla_kernel(q_ref, k_ref, v_ref, glr_ref, wg_ref, bg_ref, cos_ref, sin_ref, *rest, reverse, n_cast):
    cast_in, o_ref, cast_out, s_scr = rest[:n_cast], rest[n_cast], rest[n_cast + 1:-1], rest[-1]
    for src, dst in zip(cast_in, cast_out):
        dst[...] = src[...].astype(BF16)

    @pl.when(pl.program_id(1) == 0)
    def _():
        s_scr[...] = jnp.zeros_like(s_scr)

    row = lax.broadcasted_iota(jnp.int32, (GLA_CHUNK, GLA_CHUNK), 0)
    col = lax.broadcasted_iota(jnp.int32, (GLA_CHUNK, GLA_CHUNK), 1)
    keep = (col >= row) if reverse else (col <= row)
    tri = jnp.where(keep, 1.0, 0.0).astype(BF16)
    ones = jnp.ones((GLA_CHUNK, LANE), BF16)
    n_chunks = GLA_TB // GLA_CHUNK
    order = range(n_chunks - 1, -1, -1) if reverse else range(n_chunks)
    for ci in order:
        sl = slice(ci * GLA_CHUNK, (ci + 1) * GLA_CHUNK)
        cos = cos_ref[sl, :]
        sin = sin_ref[sl, :]
        z = _dot(glr_ref[sl, :].astype(BF16), wg_ref[0]) + bg_ref[0]
        g = (jnp.minimum(z, 0.0) - jnp.log1p(jnp.exp(-jnp.abs(z)))) * (1.0 / GLA_GATE_NORM)
        g_hi, g_lo = _split_bf16(g)
        cum_all = _dot(tri, g_hi) + _dot(tri, g_lo)
        decay_all = jnp.exp(_dot_tn(g_hi, ones) + _dot_tn(g_lo, ones))
        for hd in range(GLA_H):
            ks = slice(hd * GLA_DK, (hd + 1) * GLA_DK)
            vs = slice(hd * GLA_DV, (hd + 1) * GLA_DV)
            q = _rope(q_ref[sl, ks].astype(F32), cos, sin)
            k = _rope(k_ref[sl, ks].astype(F32), cos, sin)
            v = v_ref[sl, vs]
            cum = cum_all[:, ks]
            tot = cum[0:1, :] if reverse else cum[GLA_CHUNK - 1:GLA_CHUNK, :]
            q_dec = (q * jnp.exp(cum)).astype(BF16)
            k_inv = (k * jnp.exp(-cum)).astype(BF16)
            k_end = (k * jnp.exp(tot - cum)).astype(BF16)
            att = jnp.where(keep, _dot_nt(q_dec, k_inv), 0.0).astype(BF16)
            state = s_scr[hd]
            o_ref[sl, vs] = (_dot(att, v) + _dot(q_dec, state.astype(BF16))).astype(o_ref.dtype)
            decay = jnp.concatenate([decay_all[ks, :]] * (GLA_DV // LANE), axis=1)
            s_scr[hd] = state * decay + _dot_tn(k_end, v)


def _gla_scan(p, glr, wg, bg, cos, sin, reverse, casts):
    lat_blocks = SEQ // GLA_TB

    def tblock(s):
        if reverse:
            return jnp.where(s == 0, 0, GLA_STEPS - s)
        return s

    def rows(b, s):
        tb = tblock(s)
        return jnp.where(tb == 0, R_LAT // GLA_TB + b, b * lat_blocks + tb - 1)

    def rope_rows(s):
        tb = tblock(s)
        return jnp.where(tb == 0, lat_blocks, tb - 1)

    d = 1 if reverse else 0
    c_in, c_shape, c_out, c_args = _cast_specs(casts, lambda b, s: b * GLA_STEPS + s)
    return pl.pallas_call(
        functools.partial(_gla_kernel, reverse=reverse, n_cast=len(casts)),
        out_shape=[jax.ShapeDtypeStruct((R_ALL, GLA_VD), BF16), *c_shape],
        grid=(B, GLA_STEPS),
        in_specs=[
            pl.BlockSpec((GLA_TB, GLA_QK), lambda b, s: (rows(b, s), 0)),
            pl.BlockSpec((GLA_TB, GLA_QK), lambda b, s: (rows(b, s), 1)),
            pl.BlockSpec((GLA_TB, GLA_VD), lambda b, s: (rows(b, s), 2 * GLA_QK // GLA_VD)),
            pl.BlockSpec((GLA_TB, LANE), lambda b, s: (rows(b, s), 0)),
            pl.BlockSpec((1, LANE, GLA_QK), lambda b, s: (d, 0, 0)),
            pl.BlockSpec((1, 1, GLA_QK), lambda b, s: (d, 0, 0)),
            pl.BlockSpec((GLA_TB, GLA_DK), lambda b, s: (rope_rows(s), 0)),
            pl.BlockSpec((GLA_TB, GLA_DK), lambda b, s: (rope_rows(s), 0)),
            *c_in,
        ],
        out_specs=[pl.BlockSpec((GLA_TB, GLA_VD), lambda b, s: (rows(b, s), 0)), *c_out],
        scratch_shapes=[pltpu.VMEM((GLA_H, GLA_DK, GLA_DV), F32)],
        compiler_params=_cparams(2),
        name="gla_scan_bwd" if reverse else "gla_scan_fwd",
    )(p, p, p, glr, wg, bg, cos, sin, *c_args)


def _nat_bias_table(rpb):
    qc = np.arange(GRID_W)[:, None]
    kc = np.arange(GRID_W)[None, :]
    cs = np.clip(qc - WIN_C // 2, 0, GRID_W - WIN_C)
    col_ok = (kc >= cs) & (kc < cs + WIN_C)
    dc = np.clip(kc - qc + WIN_C - 1, 0, 2 * WIN_C - 2)
    pick = (dc[:, :, None] == np.arange(2 * WIN_C - 1)).astype(np.float32)
    tc = jnp.einsum('had,qkd->hqak', rpb.astype(F32), jnp.asarray(pick), precision=lax.Precision.HIGHEST)
    tc = jnp.where(col_ok[None, :, None, :], tc, NEG_INF)
    n_dr = 2 * WIN_R - 1
    pad = NAT_BAND - 1
    strip = jnp.pad(tc.reshape(NAT_H, GRID_W, n_dr * GRID_W), ((0, 0), (0, 0), (pad * GRID_W, pad * GRID_W)),
                    constant_values=NEG_INF)
    rows_n = SEQ // GRID_W
    half = WIN_R // 2
    n_keys = NAT_BAND * GRID_W
    bands, valid = [], np.zeros((3, NAT_QR, 1, NAT_BAND, 1), bool)
    for p_i, (r0, bs) in enumerate(((0, 0), (NAT_QR, NAT_QR - half), (rows_n - NAT_QR, rows_n - NAT_BAND))):
        for i in range(NAT_QR):
            qr = r0 + i
            rs = min(max(qr - half, 0), rows_n - WIN_R)
            first = (pad + bs - qr + WIN_R - 1) * GRID_W
            bands.append(strip[:, :, first:first + n_keys])
            valid[p_i, i, 0, :, 0] = [rs <= kr < rs + WIN_R for kr in range(bs, bs + NAT_BAND)]
    tab = jnp.stack(bands).reshape(3, NAT_QR, NAT_H, GRID_W, n_keys)
    valid = np.broadcast_to(valid, (3, NAT_QR, GRID_W, NAT_BAND, GRID_W)).reshape(3, NAT_QR, 1, GRID_W, n_keys)
    return jnp.where(valid, tab, NEG_INF)


def _nat_kernel(q_ref, k_ref, v_ref, kc_ref, vc_ref, bias_ref, o_ref):
    rb = pl.program_id(2)
    rows_n = SEQ // GRID_W
    band_start = jnp.clip(rb * NAT_QR - WIN_R // 2, 0, rows_n - NAT_BAND)
    start = pl.multiple_of(band_start * GRID_W, GRID_W)
    n_keys = NAT_BAND * GRID_W
    for g in range(NAT_G):
        hs = slice(g * NAT_DH, (g + 1) * NAT_DH)
        q = q_ref[:, hs]
        bias = bias_ref[0, :, g].reshape(NAT_QR * GRID_W, n_keys)
        s_loc = _dot_nt(q, k_ref[pl.ds(start, n_keys), hs]) + bias
        s_ctx = _dot_nt(q, kc_ref[:, hs])
        m = jnp.maximum(jnp.max(s_loc, axis=1, keepdims=True), jnp.max(s_ctx, axis=1, keepdims=True))
        p_loc = jnp.exp(s_loc - m)
        p_ctx = jnp.exp(s_ctx - m)
        denom = jnp.sum(p_loc, axis=1, keepdims=True) + jnp.sum(p_ctx, axis=1, keepdims=True)
        o = _dot(p_loc.astype(BF16), v_ref[pl.ds(start, n_keys), hs]) + _dot(p_ctx.astype(BF16), vc_ref[:, hs])
        o_ref[:, hs] = (o / denom).astype(o_ref.dtype)


def _nat_attention(p, bias_tab):
    n_rb = SEQ // (NAT_QR * GRID_W)
    tq = NAT_QR * GRID_W
    width = NAT_G * NAT_DH
    hq = D // width

    def pattern(rb):
        return jnp.where(rb == 0, 0, jnp.where(rb == n_rb - 1, 2, 1))

    return pl.pallas_call(
        _nat_kernel,
        out_shape=jax.ShapeDtypeStruct((R_LAT, D), BF16),
        grid=(B, NAT_H // NAT_G, n_rb),
        in_specs=[
            pl.BlockSpec((tq, width), lambda b, h, rb: (b * n_rb + rb, h)),
            pl.BlockSpec((SEQ, width), lambda b, h, rb: (b, hq + h)),
            pl.BlockSpec((SEQ, width), lambda b, h, rb: (b, 2 * hq + h)),
            pl.BlockSpec((CTX, width), lambda b, h, rb: (R_LAT // CTX + b, hq + h)),
            pl.BlockSpec((CTX, width), lambda b, h, rb: (R_LAT // CTX + b, 2 * hq + h)),
            pl.BlockSpec((1, NAT_QR, NAT_G, GRID_W, NAT_BAND * GRID_W), lambda b, h, rb: (pattern(rb), 0, h, 0, 0)),
        ],
        out_specs=pl.BlockSpec((tq, width), lambda b, h, rb: (b * n_rb + rb, h)),
        compiler_params=_cparams(3),
        name="nat_attention",
    )(p, p, p, p, p, bias_tab)


MIX_TM = 256


def _layer_norm(u, g, b):
    mu = jnp.mean(u, axis=-1, keepdims=True)
    var = jnp.mean(jnp.square(u - mu), axis=-1, keepdims=True)
    return (u - mu) * lax.rsqrt(var + LN_EPS) * g + b


def _mix_out_kernel(*refs, gla, lat_tiles):
    if gla:
        of_ref, ob_ref, r_ref, ng_ref = refs[:4]
        refs = refs[4:]
    else:
        x_ref = refs[0]
        refs = refs[1:]
    (h_ref, hc_ref, mod_ref, w_ref, lng_ref, lnb_ref, wr_ref, br_ref,
     h1_ref, f_ref, rt_ref, cnt_ref, y_even, y_odd, cnt_scr) = refs
    s = pl.program_id(0)

    @pl.when(s == 0)
    def _():
        cnt_scr[...] = jnp.zeros_like(cnt_scr)
        y_odd[...] = jnp.zeros_like(y_odd)

    def step(y_new, y_prev):
        if gla:
            parts = []
            for hd in range(GLA_H):
                cs = slice(hd * GLA_DV, (hd + 1) * GLA_DV)
                o = of_ref[:, cs].astype(F32) + ob_ref[:, cs].astype(F32)
                o = o * lax.rsqrt(jnp.mean(jnp.square(o), axis=-1, keepdims=True) + LN_EPS) * ng_ref[...]
                r = r_ref[:, cs].astype(F32)
                parts.append((o * (r * jax.nn.sigmoid(r))).astype(BF16))
            x = jnp.concatenate(parts, axis=1)
        else:
            x = x_ref[...]
        y_new[...] = _dot(x, w_ref[...])

        gate1 = mod_ref[0, 2:3, :]
        h_in = jnp.where(s - 1 < lat_tiles, h_ref[...], hc_ref[...])
        h1 = _layer_norm(ALPHA * h_in + gate1 * y_prev[...], lng_ref[...], lnb_ref[...])
        h1_ref[...] = h1
        f = h1 * (1.0 + mod_ref[0, 4:5, :]) + mod_ref[0, 3:4, :]
        f_ref[...] = _pack_rows(f)
        live = jnp.where(s > 0, 1.0, 0.0)
        logits = _dot(f.astype(BF16), wr_ref[...]) + br_ref[...]
        route, counts = _route_tile(logits, cnt_scr[...], live)
        rt_ref[...] = route
        cnt_scr[...] = counts
        cnt_ref[...] = jnp.broadcast_to(counts, cnt_ref.shape)

    @pl.when(s % 2 == 0)
    def _():
        step(y_even, y_odd)

    @pl.when(s % 2 == 1)
    def _():
        step(y_odd, y_even)


RT_E, RT_W, RT_RANK = 0, 2, 4


def _route_tile(logits, run, live):
    tm = logits.shape[0]
    lane = lax.broadcasted_iota(jnp.int32, (tm, LANE), 1).astype(F32)

    def first_max(vals):
        top = jnp.max(vals, axis=1, keepdims=True)
        return top, jnp.min(jnp.where(vals == top, lane, float(LANE)), axis=1, keepdims=True)

    gl = jnp.where(lane < N_GROUPS, logits, NEG_INF)
    g_max, g_idx = first_max(gl)
    g_p = 1.0 / jnp.sum(jnp.exp(gl - g_max), axis=1, keepdims=True)
    lo = N_GROUPS + EPG * g_idx
    el = jnp.where((lane >= lo) & (lane < lo + EPG), logits, NEG_INF)
    e1, i1 = first_max(el)
    e2, i2 = first_max(jnp.where(lane == i1, NEG_INF, el))
    t = jnp.exp(e2 - e1)
    p1 = 1.0 / (1.0 + t)
    x1 = i1 - N_GROUPS
    x2 = i2 - N_GROUPS
    onehot = jnp.where((lane == x1) | (lane == x2), 1.0, 0.0)
    row = lax.broadcasted_iota(jnp.int32, (tm, tm), 0)
    col = lax.broadcasted_iota(jnp.int32, (tm, tm), 1)
    before = jnp.where(col < row, 1.0, 0.0).astype(BF16)
    seen = _dot(before, onehot.astype(BF16)) + run
    rank1 = jnp.sum(jnp.where(lane == x1, seen, 0.0), axis=1, keepdims=True)
    rank2 = jnp.sum(jnp.where(lane == x2, seen, 0.0), axis=1, keepdims=True)
    route = jnp.zeros((tm, LANE), F32)
    for k, val in ((RT_E, x1), (RT_E + 1, x2), (RT_W, g_p * p1), (RT_W + 1, g_p * p1 * t),
                   (RT_RANK, rank1), (RT_RANK + 1, rank2)):
        route = jnp.where(lane == k, val, route)
    return route, run + live * jnp.sum(onehot, axis=0, keepdims=True)


def _mix_out(mix_inputs, stream, mod, w_out, ln_g, ln_b, w_router, b_router, rows, gla):
    n_tiles = rows // MIX_TM
    row_spec = lambda width: pl.BlockSpec((MIX_TM, width), lambda i: (jnp.minimum(i, n_tiles - 1), 0))
    done_spec = lambda width: pl.BlockSpec((MIX_TM, width), lambda i: (jnp.maximum(i - 1, 0), 0))
    full = lambda a: pl.BlockSpec(a.shape, lambda i: (0,) * a.ndim)
    if gla:
        o_f, o_b, p, norm_g = mix_inputs
        in_specs = [row_spec(GLA_VD), row_spec(GLA_VD),
                    pl.BlockSpec((MIX_TM, GLA_VD),
                                 lambda i: (jnp.minimum(i, n_tiles - 1), (2 * GLA_QK + GLA_VD) // GLA_VD)),
                    full(norm_g)]
        args = [o_f, o_b, p, norm_g]
    else:
        (x,) = mix_inputs
        in_specs = [row_spec(D)]
        args = [x]
    in_specs += [*_stream_specs(stream, MIX_TM, lag=1),
                 pl.BlockSpec((1, 6, D), lambda i: (_mod_row(MIX_TM)(jnp.maximum(i - 1, 0)), 0, 0)),
                 full(w_out), full(ln_g), full(ln_b), full(w_router), full(b_router)]
    args += [stream[0], stream[1], mod, w_out, ln_g, ln_b, w_router, b_router]
    return pl.pallas_call(
        functools.partial(_mix_out_kernel, gla=gla, lat_tiles=R_LAT // MIX_TM),
        out_shape=[jax.ShapeDtypeStruct((rows, D), F32), jax.ShapeDtypeStruct((rows, D_PACK), jnp.uint32),
                   jax.ShapeDtypeStruct((rows, LANE), F32), jax.ShapeDtypeStruct((SUBLANE, LANE), F32)],
        grid=(n_tiles + 1,),
        in_specs=in_specs,
        out_specs=[done_spec(D), done_spec(D_PACK), done_spec(LANE), pl.BlockSpec((SUBLANE, LANE), lambda i: (0, 0))],
        scratch_shapes=[pltpu.VMEM((MIX_TM, D), F32), pltpu.VMEM((MIX_TM, D), F32), pltpu.VMEM((1, LANE), F32)],
        compiler_params=_cparams(1),
        name="gla_out_ln" if gla else "nat_out_ln",
    )(*args)


def _plan(route, counts_row):
    t = route.shape[0]
    e = route[:, RT_E:RT_E + 2].astype(jnp.int32)
    rank = route[:, RT_RANK:RT_RANK + 2].astype(jnp.int32)
    counts = counts_row[0, :N_EXPERTS].astype(jnp.int32)
    tiles_per = (counts + MOE_TM - 1) // MOE_TM
    tile_end = jnp.cumsum(tiles_per)
    tile_start = tile_end - tiles_per
    n_tiles = 2 * t // MOE_TM + N_EXPERTS
    ids = jnp.arange(n_tiles, dtype=jnp.int32)
    used = tile_end[-1]
    tile_valid = (ids < used).astype(jnp.int32)
    lookup = jnp.minimum(ids, used - 1)
    tile_expert = jnp.sum((tile_end[None, :] <= lookup[:, None]).astype(jnp.int32), axis=1)
    tile_expert = jnp.minimum(tile_expert, N_EXPERTS - 1).astype(jnp.int32)
    base = tile_start * MOE_TM
    experts = jnp.arange(N_EXPERTS, dtype=jnp.int32)
    pos = jnp.sum(jnp.where(e[..., None] == experts, base, 0), axis=-1) + rank
    trailing = n_tiles - 1 - experts
    fill_tile = jnp.concatenate([jnp.maximum(tile_end - 1, 0), trailing]).astype(jnp.int32)
    fill_on = jnp.concatenate([tiles_per > 0, trailing >= used]).astype(jnp.int32)
    return tile_expert, tile_valid, pos.reshape(-1).astype(jnp.int32), fill_tile, fill_on, n_tiles


def _start_row_copies(pos_ref, tile, rows, make, part=None):
    base = 2 * tile * rows
    for r in range(*(part or (0, rows))):
        make(r, pos_ref[base + 2 * r], False).start(priority=0)
        make(r, pos_ref[base + 2 * r + 1], True).start(priority=1)


DISP_TM = 256


def _dispatch_kernel(pos_ref, lt_ref, ht_ref, f_ref, xs_hbm, sbuf, zbuf, sem, zsem):
    i = pl.program_id(0)
    n = pl.num_programs(0)
    slot = i % 2

    def fill(e):
        return pltpu.make_async_copy(zbuf, xs_hbm.at[pl.ds(lt_ref[e] * MOE_TM, MOE_TM), :], zsem)

    @pl.when(i == 0)
    def _():
        zbuf[...] = jnp.zeros_like(zbuf)
        for e in range(2 * N_EXPERTS):
            @pl.when(ht_ref[e] != 0)
            def _():
                fill(e).start()
        for e in range(2 * N_EXPERTS):
            @pl.when(ht_ref[e] != 0)
            def _():
                fill(e).wait()

    def wait_scatter(sl):
        for _ in range(2):
            pltpu.make_async_copy(sbuf.at[sl], xs_hbm.at[pl.ds(0, DISP_TM), :], sem.at[sl]).wait()

    @pl.when(i >= 2)
    def _():
        wait_scatter(slot)

    sbuf[slot] = f_ref[...]

    def make(r, dst, second):
        return pltpu.make_async_copy(sbuf.at[slot, pl.ds(r, 1), :], xs_hbm.at[pl.ds(dst, 1), :], sem.at[slot])

    _start_row_copies(pos_ref, i, DISP_TM, make)

    @pl.when(i == n - 1)
    def _():
        wait_scatter(slot)

        @pl.when(n >= 2)
        def _():
            wait_scatter(1 - slot)


def _dispatch(f, pos, last_tile, has_tile, cap):
    t = f.shape[0]
    grid_spec = pltpu.PrefetchScalarGridSpec(
        num_scalar_prefetch=3,
        grid=(t // DISP_TM,),
        in_specs=[pl.BlockSpec((DISP_TM, D_PACK), lambda i, *_: (i, 0))],
        out_specs=pl.BlockSpec(memory_space=pl.ANY),
        scratch_shapes=[
            pltpu.VMEM((2, DISP_TM, D_PACK), jnp.uint32),
            pltpu.VMEM((MOE_TM, D_PACK), jnp.uint32),
            pltpu.SemaphoreType.DMA((2,)),
            pltpu.SemaphoreType.DMA,
        ],
    )
    return pl.pallas_call(
        _dispatch_kernel,
        out_shape=jax.ShapeDtypeStruct((cap, D_PACK), jnp.uint32),
        grid_spec=grid_spec,
        compiler_params=_cparams(1),
        name="moe_dispatch",
    )(pos, last_tile, has_tile, f)


def _moe_kernel(te_ref, tv_ref, x_ref, w1_ref, w3_ref, w2_ref, y_ref):
    i = pl.program_id(0)

    @pl.when(tv_ref[i] != 0)
    def _():
        x = _unpack_rows(x_ref[...]).astype(BF16)
        h1 = _dot(x, w1_ref[...])
        h3 = _dot(x, w3_ref[...])
        a = (h1 * jax.nn.sigmoid(h1) * h3).astype(BF16)
        y_ref[...] = _pack_rows(_dot(a, w2_ref[...]))

    @pl.when(tv_ref[i] == 0)
    def _():
        y_ref[...] = jnp.zeros_like(y_ref)


def _moe_experts(xs, tile_expert, tile_valid, w1, w3, w2):
    cap = xs.shape[0]
    grid_spec = pltpu.PrefetchScalarGridSpec(
        num_scalar_prefetch=2,
        grid=(cap // MOE_TM,),
        in_specs=[
            pl.BlockSpec((MOE_TM, D_PACK), lambda i, te, tv: (i * tv[i], 0)),
            pl.BlockSpec((D, D_EXPERT), lambda i, te, tv: (te[i], 0)),
            pl.BlockSpec((D, D_EXPERT), lambda i, te, tv: (te[i], 0)),
            pl.BlockSpec((D_EXPERT, D), lambda i, te, tv: (te[i], 0)),
        ],
        out_specs=pl.BlockSpec((MOE_TM, D_PACK), lambda i, te, tv: (i, 0)),
    )
    return pl.pallas_call(
        _moe_kernel,
        out_shape=jax.ShapeDtypeStruct((cap, D_PACK), jnp.uint32),
        grid_spec=grid_spec,
        compiler_params=_cparams(1),
        name="moe_experts",
    )(tile_expert, tile_valid, xs, w1, w3, w2)


FIN_TM = 256


def _final_kernel(pos_ref, h_ref, rt_ref, mod_ref, lng_ref, lnb_ref, ys_hbm, o_ref, abuf, bbuf, sem):
    i = pl.program_id(0)
    n = pl.num_programs(0)
    slot = i % 2

    def start_gather(tile, sl, part=None):
        def make(r, src, second):
            buf = bbuf if second else abuf
            return pltpu.make_async_copy(ys_hbm.at[pl.ds(src, 1), :], buf.at[sl, pl.ds(r, 1), :], sem.at[sl])

        _start_row_copies(pos_ref, tile, FIN_TM, make, part)

    @pl.when(i == 0)
    def _():
        start_gather(0, 0)

    @pl.when(i + 1 < n)
    def _():
        start_gather(i + 1, 1 - slot, (0, FIN_TM // 2))

    for buf in (abuf, bbuf):
        pltpu.make_async_copy(ys_hbm.at[pl.ds(0, FIN_TM), :], buf.at[slot], sem.at[slot]).wait()

    y = (_unpack_rows(abuf[slot]) * rt_ref[:, RT_W:RT_W + 1]
         + _unpack_rows(bbuf[slot]) * rt_ref[:, RT_W + 1:RT_W + 2])
    gate2 = mod_ref[0, 5:6, :]
    o_ref[...] = _layer_norm(ALPHA * h_ref[...] + gate2 * y, lng_ref[...], lnb_ref[...])

    @pl.when(i + 1 < n)
    def _():
        start_gather(i + 1, 1 - slot, (FIN_TM // 2, FIN_TM))


def _ffn_residual(h1, route, pos, ys, mod, ln_g, ln_b):
    rows = h1.shape[0]
    grid_spec = pltpu.PrefetchScalarGridSpec(
        num_scalar_prefetch=1,
        grid=(rows // FIN_TM,),
        in_specs=[
            pl.BlockSpec((FIN_TM, D), lambda i, *_: (i, 0)),
            pl.BlockSpec((FIN_TM, LANE), lambda i, *_: (i, 0)),
            pl.BlockSpec((1, 6, D), lambda i, *_: (_mod_row(FIN_TM)(i), 0, 0)),
            pl.BlockSpec((1, D), lambda i, *_: (0, 0)),
            pl.BlockSpec((1, D), lambda i, *_: (0, 0)),
            pl.BlockSpec(memory_space=pl.ANY),
        ],
        out_specs=pl.BlockSpec((FIN_TM, D), lambda i, *_: (i, 0)),
        scratch_shapes=[
            pltpu.VMEM((2, FIN_TM, D_PACK), jnp.uint32),
            pltpu.VMEM((2, FIN_TM, D_PACK), jnp.uint32),
            pltpu.SemaphoreType.DMA((2,)),
        ],
    )
    return pl.pallas_call(
        _final_kernel,
        out_shape=jax.ShapeDtypeStruct((rows, D), F32),
        grid_spec=grid_spec,
        compiler_params=_cparams(1),
        name="ffn_residual_ln",
    )(pos, h1, route, mod, ln_g, ln_b, ys)


def _moe_ffn(h1, f, route, counts, mod, w1, w3, w2, ln_g, ln_b):
    tile_expert, tile_valid, pos, fill_tile, fill_on, n_tiles = _plan(route, counts)
    xs = _dispatch(f, pos, fill_tile, fill_on, n_tiles * MOE_TM)
    ys = _moe_experts(xs, tile_expert, tile_valid, w1, w3, w2)
    return _ffn_residual(h1, route, pos, ys, mod, ln_g, ln_b)


def _router_weights(w_group, b_group, w_expert, b_expert):
    pad = LANE - N_GROUPS - N_EXPERTS
    w = jnp.concatenate([w_group, w_expert, jnp.zeros((D, pad), F32)], axis=1).astype(BF16)
    b = jnp.concatenate([b_group, b_expert, jnp.zeros((pad,), F32)])[None, :]
    return w, b


def kernel(x, c, ctx, c_ctx, ada_w, ada_b, ln_g, ln_b, gla_w_in, gla_w_gate, gla_b_gate, gla_norm_g, gla_w_out,
           nat_w_in, nat_rpb, nat_w_out, moe_w_group, moe_b_group, moe_w_expert, moe_b_expert,
           moe_w1, moe_w3, moe_w2):
    stream = (x.reshape(R_LAT, D), ctx.reshape(R_CTX, D), 0)
    cc = jnp.concatenate([c, c_ctx[None, :], jnp.zeros((SUBLANE - B - 1, D), F32)], axis=0)
    mod = _ada_table(cc, ada_w, ada_b).reshape(DEPTH, SUBLANE, 6, D)
    w1_all = moe_w1.reshape(DEPTH * N_EXPERTS * D, D_EXPERT)
    w3_all = moe_w3.reshape(DEPTH * N_EXPERTS * D, D_EXPERT)
    w2_all = moe_w2.reshape(DEPTH * N_EXPERTS * D_EXPERT, D)

    def expert_casts(layer):
        up, down = N_EXPERTS * D, N_EXPERTS * D_EXPERT
        return ((w1_all, up, layer * up), (w3_all, up, layer * up), (w2_all, down, layer * down))

    w_in = gla_w_in[0]
    w_glr = jnp.concatenate([w_in[:, GLA_MAIN:], jnp.zeros((D, LANE - 2 * GLA_RANK), F32)], axis=1).astype(BF16)
    p, glr = _proj(stream, mod[0], w_in.astype(BF16), GLA_MAIN, GLA_QK, GLA_DK ** -0.5, w_glr)
    wg = jnp.zeros((2, LANE, GLA_QK), F32)
    wg = wg.at[0, :GLA_RANK].set(gla_w_gate[0, 0]).at[1, GLA_RANK:2 * GLA_RANK].set(gla_w_gate[0, 1]).astype(BF16)
    bg = gla_b_gate[0][:, None, :]
    cos, sin = _rope_tables()
    o_f, *experts0 = _gla_scan(p, glr, wg, bg, cos, sin, False, expert_casts(0))
    o_b, *experts1 = _gla_scan(p, glr, wg, bg, cos, sin, True, expert_casts(1))
    w_r, b_r = _router_weights(moe_w_group[0], moe_b_group[0], moe_w_expert[0], moe_b_expert[0])
    h1, f, route, counts = _mix_out((o_f, o_b, p, gla_norm_g[0][None, :]), stream, mod[0], gla_w_out[0].astype(BF16),
                                    ln_g[0, 0][None, :], ln_b[0, 0][None, :], w_r, b_r, R_ALL, gla=True)
    h = _moe_ffn(h1, f, route, counts, mod[0], *experts0, ln_g[0, 1][None, :], ln_b[0, 1][None, :])

    stream = (h, h, R_LAT)
    p = _proj(stream, mod[1], nat_w_in[0].astype(BF16), 3 * D, D, NAT_DH ** -0.5)
    o = _nat_attention(p, _nat_bias_table(nat_rpb[0]))
    w_r, b_r = _router_weights(moe_w_group[1], moe_b_group[1], moe_w_expert[1], moe_b_expert[1])
    h1, f, route, counts = _mix_out((o,), stream, mod[1], nat_w_out[0].astype(BF16),
                                    ln_g[1, 0][None, :], ln_b[1, 0][None, :], w_r, b_r, R_LAT, gla=False)
    out = _moe_ffn(h1, f, route, counts, mod[1], *experts1, ln_g[1, 1][None, :], ln_b[1, 1][None, :])
    return out.reshape(B, SEQ, D)
```
